```python
import math
import jax, jax.numpy as jnp
from jax import lax
import numpy as np

D_MODEL = 1024
BATCH = 8
SEQ = 8192
DEPTH = 4
DEC_BATCH = 8
DEC_SEQ = 4096
PAST_LEN = 128

HEAD_DIM = 64
CONV_WIDTH = D_MODEL // 2
CONV_K = 3
SWA_HEADS = D_MODEL // 128
SWA_KV_HEADS = SWA_HEADS // 4
WINDOW = 128
AX_HEADS = D_MODEL // 128
AX_KV_HEADS = AX_HEADS // 4
DIFF_HEADS = D_MODEL // 256
MEM_HEADS = 4
N_MEM = 256
GRID_W = 64
Q_BLOCK = 128
ROPE_THETA = 10000.0
EPS = 1e-6
NEG = -1e30

SWA_Q = SWA_HEADS * HEAD_DIM
SWA_KV = SWA_KV_HEADS * HEAD_DIM
AX_Q = AX_HEADS * HEAD_DIM
AX_KV = AX_KV_HEADS * HEAD_DIM
DIFF_QK = 2 * DIFF_HEADS * HEAD_DIM
DIFF_V = DIFF_HEADS * 2 * HEAD_DIM
MEM_WIDTH = MEM_HEADS * HEAD_DIM
MIX_WIDTH = CONV_WIDTH + SWA_Q + MEM_WIDTH
EVEN_SPLITS = (CONV_WIDTH, CONV_WIDTH, CONV_WIDTH, SWA_Q, SWA_KV, SWA_KV, MEM_WIDTH, MIX_WIDTH)
ODD_SPLITS = (AX_Q, AX_KV, AX_KV, DIFF_QK, DIFF_QK, DIFF_V, MEM_WIDTH, MIX_WIDTH)
IN_WIDTH = sum(EVEN_SPLITS)
N_EVEN = (DEPTH + 1) // 2
N_ODD = DEPTH // 2

kernel_name = 'hybrid_conv_swa_axial_diff_encoder'


def rms_norm(x, g):
    xf = x.astype(jnp.float32)
    y = xf * lax.rsqrt(jnp.mean(xf * xf, axis=-1, keepdims=True) + EPS)
    return (y * g.astype(jnp.float32)).astype(x.dtype)


def heads(t, n):
    return t.reshape(t.shape[:-1] + (n, t.shape[-1] // n))


def split_cols(u, sizes):
    idx = [int(i) for i in np.cumsum(sizes)[:-1]]
    return jnp.split(u, idx, axis=-1)


def rope_angles(pos, dim):
    inv = ROPE_THETA ** (-jnp.arange(0, dim, 2, dtype=jnp.float32) / dim)
    return pos.astype(jnp.float32)[:, None] * inv[None, :]


def apply_rope(x, ang):
    cos = jnp.cos(ang)[None, :, None, :].astype(x.dtype)
    sin = jnp.sin(ang)[None, :, None, :].astype(x.dtype)
    x1, x2 = jnp.split(x, 2, axis=-1)
    return jnp.concatenate([x1 * cos - x2 * sin, x2 * cos + x1 * sin], axis=-1)


def axial_rope(x, row_ang, col_ang):
    half = HEAD_DIM // 2
    return jnp.concatenate([apply_rope(x[..., :half], row_ang),
                            apply_rope(x[..., half:], col_ang)], axis=-1)


def sweep_query_blocks(fn, q):
    b, s = q.shape[:2]
    nb = s // Q_BLOCK
    qb = jnp.moveaxis(q.reshape((b, nb, Q_BLOCK) + q.shape[2:]), 1, 0)
    out = jnp.moveaxis(lax.map(fn, qb), 0, 1)
    return out.reshape((b, s) + out.shape[3:])


def short_conv(gb, gc, hc, w):
    inner = gc * hc
    p = jnp.pad(inner, ((0, 0), (1, 1), (0, 0)))
    conv = p[:, :-2] * w[0] + p[:, 1:-1] * w[1] + p[:, 2:] * w[2]
    return gb * conv


def window_attention(q, k, v, sink):
    b, s, h, d = q.shape
    hkv = k.shape[2]
    g = h // hkv
    nb = s // WINDOW
    qb = q.reshape(b, nb, WINDOW, hkv, g, d)

    def band(t):
        tp = jnp.pad(t.reshape(b, nb, WINDOW, hkv, d), ((0, 0), (1, 1), (0, 0), (0, 0), (0, 0)))
        return jnp.concatenate([tp[:, :-2], tp[:, 1:-1], tp[:, 2:]], axis=2)

    kb, vb = band(k), band(v)
    sc = jnp.einsum('bnqkgd,bnjkd->bnkgqj', qb, kb, preferred_element_type=jnp.float32) * (d ** -0.5)
    blk = jnp.arange(nb)[:, None, None]
    qpos = blk * WINDOW + jnp.arange(WINDOW)[None, :, None]
    kpos = (blk - 1) * WINDOW + jnp.arange(3 * WINDOW)[None, None, :]
    mask = (jnp.abs(qpos - kpos) <= WINDOW) & (kpos >= 0) & (kpos < s)
    sc = jnp.where(mask[None, :, None, None], sc, NEG)
    sink_l = jnp.broadcast_to(sink.astype(jnp.float32).reshape(1, 1, hkv, g, 1, 1), sc.shape[:-1] + (1,))
    p = jax.nn.softmax(jnp.concatenate([sc, sink_l], axis=-1), axis=-1)[..., :-1]
    o = jnp.einsum('bnkgqj,bnjkd->bnqkgd', p.astype(v.dtype), vb)
    return o.reshape(b, s, h * d)


def dense_gqa(q, k, v):
    b, s, h, d = q.shape
    hkv = k.shape[2]
    g = h // hkv
    scale = d ** -0.5

    def block(qi):
        sc = jnp.einsum('bqkgd,bskd->bkgqs', qi, k, preferred_element_type=jnp.float32) * scale
        p = jax.nn.softmax(sc, axis=-1).astype(v.dtype)
        return jnp.einsum('bkgqs,bskd->bqkgd', p, v)

    o = sweep_query_blocks(block, q.reshape(b, s, hkv, g, d))
    return o.reshape(b, s, h * d)


def diff_attention(q, k, v, lam_vec, subln_g, layer):
    b, s, h2, d = q.shape
    h = h2 // 2
    lambda_init = 0.8 - 0.6 * math.exp(-0.3 * layer)
    lv = lam_vec.astype(jnp.float32)
    lam = jnp.exp(jnp.sum(lv[0] * lv[1])) - jnp.exp(jnp.sum(lv[2] * lv[3])) + lambda_init
    scale = d ** -0.5

    def block(qi):
        sc = jnp.einsum('bqhd,bshd->bhqs', qi, k, preferred_element_type=jnp.float32) * scale
        p = jax.nn.softmax(sc, axis=-1).reshape(b, h, 2, qi.shape[1], s)
        a = p[:, :, 0] - lam * p[:, :, 1]
        return jnp.einsum('bhqs,bshe->bqhe', a.astype(v.dtype), v)

    o = sweep_query_blocks(block, q)
    o = rms_norm(o, subln_g) * (1.0 - lambda_init)
    return o.reshape(b, s, h * 2 * d)


def memory_attention(q, mk, mv):
    b, s, hm, d = q.shape
    sc = jnp.einsum('bqhd,bmhd->bhqm', q, mk, preferred_element_type=jnp.float32) * (d ** -0.5)
    p = jax.nn.softmax(sc, axis=-1).astype(mv.dtype)
    return jnp.einsum('bhqm,bmhd->bqhd', p, mv).reshape(b, s, hm * d)


def trunk(x, mem, norm_g, w_in, w_out, mem_norm_g, w_mem_kv, mem_qk_g, conv_w, swa_qk_g,
          swa_sink, ax_qk_g, diff_qk_g, diff_lambda, diff_subln_g):
    b, s, _ = x.shape
    rows = s // GRID_W
    ang_1d = rope_angles(jnp.arange(s), HEAD_DIM)
    row_ang = rope_angles(jnp.repeat(jnp.arange(rows), GRID_W), HEAD_DIM // 2)
    col_ang = rope_angles(jnp.tile(jnp.arange(GRID_W), rows), HEAD_DIM // 2)
    for l in range(DEPTH):
        h = rms_norm(x, norm_g[l])
        u = h @ w_in[l]
        mkv = rms_norm(mem, mem_norm_g[l]) @ w_mem_kv[l]
        mk, mv = jnp.split(mkv, 2, axis=-1)
        mk = rms_norm(heads(mk, MEM_HEADS), mem_qk_g[l, 1])
        mv = heads(mv, MEM_HEADS)
        if l % 2 == 0:
            e = l // 2
            gb, gc, hc, q, k, v, mq, z = split_cols(u, EVEN_SPLITS)
            y1 = short_conv(gb, gc, hc, conv_w[e])
            q = apply_rope(rms_norm(heads(q, SWA_HEADS), swa_qk_g[e, 0]), ang_1d)
            k = apply_rope(rms_norm(heads(k, SWA_KV_HEADS), swa_qk_g[e, 1]), ang_1d)
            y2 = window_attention(q, k, heads(v, SWA_KV_HEADS), swa_sink[e])
        else:
            o = l // 2
            q, k, v, dq, dk, dv, mq, z = split_cols(u, ODD_SPLITS)
            q = axial_rope(rms_norm(heads(q, AX_HEADS), ax_qk_g[o, 0]), row_ang, col_ang)
            k = axial_rope(rms_norm(heads(k, AX_KV_HEADS), ax_qk_g[o, 1]), row_ang, col_ang)
            y1 = dense_gqa(q, k, heads(v, AX_KV_HEADS))
            dq = apply_rope(rms_norm(heads(dq, 2 * DIFF_HEADS), diff_qk_g[o, 0]), ang_1d)
            dk = apply_rope(rms_norm(heads(dk, 2 * DIFF_HEADS), diff_qk_g[o, 1]), ang_1d)
            y2 = diff_attention(dq, dk, heads(dv, DIFF_HEADS), diff_lambda[o], diff_subln_g[o], l)
        mq = rms_norm(heads(mq, MEM_HEADS), mem_qk_g[l, 0])
        ym = memory_attention(mq, mk, mv)
        y = jnp.concatenate([y1, y2, ym], axis=-1)
        x = x + (y * jax.nn.silu(z)) @ w_out[l]
    return x


def setup_inputs(seed: int = 0) -> dict:
    key = jax.random.key(seed)
    ks = jax.random.split(key, 18)
    f32 = jnp.float32
    nrm = lambda k, shp, sc: jax.random.normal(k, shp, f32) * sc
    gain = lambda k, shp: 1.0 + 0.02 * jax.random.normal(k, shp, f32)
    return {
        'x_prompt': nrm(ks[0], (BATCH, SEQ, D_MODEL), 1.0),
        'x_sample': nrm(ks[1], (DEC_BATCH, DEC_SEQ, D_MODEL), 1.0),
        'mem_prompt': nrm(ks[2], (BATCH, N_MEM, D_MODEL), 1.0),
        'mem_sample': nrm(ks[3], (DEC_BATCH, N_MEM, D_MODEL), 1.0),
        'norm_g': gain(ks[4], (DEPTH, D_MODEL)),
        'w_in': nrm(ks[5], (DEPTH, D_MODEL, IN_WIDTH), D_MODEL ** -0.5),
        'w_out': nrm(ks[6], (DEPTH, MIX_WIDTH, D_MODEL), MIX_WIDTH ** -0.5),
        'mem_norm_g': gain(ks[7], (DEPTH, D_MODEL)),
        'w_mem_kv': nrm(ks[8], (DEPTH, D_MODEL, 2 * MEM_WIDTH), D_MODEL ** -0.5),
        'mem_qk_g': gain(ks[9], (DEPTH, 2, HEAD_DIM)),
        'conv_w': nrm(ks[10], (N_EVEN, CONV_K, CONV_WIDTH), CONV_K ** -0.5),
        'swa_qk_g': gain(ks[11], (N_EVEN, 2, HEAD_DIM)),
        'swa_sink': nrm(ks[12], (N_EVEN, SWA_HEADS), 0.5),
        'ax_qk_g': gain(ks[13], (N_ODD, 2, HEAD_DIM)),
        'diff_qk_g': gain(ks[14], (N_ODD, 2, HEAD_DIM)),
        'diff_lambda': nrm(ks[15], (N_ODD, 4, HEAD_DIM), 0.1),
        'diff_subln_g': gain(ks[16], (N_ODD, 2 * HEAD_DIM)),
    }


def reference(x_prompt, x_sample, mem_prompt, mem_sample, norm_g, w_in, w_out, mem_norm_g,
              w_mem_kv, mem_qk_g, conv_w, swa_qk_g, swa_sink, ax_qk_g, diff_qk_g,
              diff_lambda, diff_subln_g):
    y_prompt = trunk(x_prompt, mem_prompt, norm_g, w_in, w_out, mem_norm_g, w_mem_kv, mem_qk_g,
                     conv_w, swa_qk_g, swa_sink, ax_qk_g, diff_qk_g, diff_lambda, diff_subln_g)
    y_sample = trunk(x_sample, mem_sample, norm_g, w_in, w_out, mem_norm_g, w_mem_kv, mem_qk_g,
                     conv_w, swa_qk_g, swa_sink, ax_qk_g, diff_qk_g, diff_lambda, diff_subln_g)
    return (y_prompt, y_sample)
```

```python
import functools
import math

import jax
import jax.numpy as jnp
from jax import lax
from jax.experimental import pallas as pl
from jax.experimental.pallas import tpu as pltpu

F32 = jnp.float32
BF16 = jnp.bfloat16

D_MODEL = 1024
DEPTH = 4
HEAD = 64
LANES = 128
SUBLANES = 8
N_Q_HEADS = 8
GQA_GROUP = 4
DIFF_HEADS = 4
MEM_HEADS = 4
CONV_W = 512
WINDOW = 128
GRID_W = 64
ROPE_THETA = 10000.0
EPS = 1e-6
NEG = -1e30
LOG2E = 1.4426950408889634
Q_SCALE = HEAD ** -0.5 * LOG2E

IN_W = 3840
MIX_W = 1280
GATE_OFF = 2560
MQ_OFF = 2304

ROW_TILE = 512
Q_TILE = 256
VMEM_LIMIT = 56 * 1024 * 1024

_NT = (((1,), (1,)), ((), ()))


def _params(n_axes):
    return pltpu.CompilerParams(dimension_semantics=("arbitrary",) * n_axes, vmem_limit_bytes=VMEM_LIMIT)


def _dot(a, b):
    return jnp.dot(a, b, preferred_element_type=F32)


def _dot_nt(a, b):
    return lax.dot_general(a, b, _NT, preferred_element_type=F32)


def _rms_rows(x, g):
    ms = jnp.mean(x * x, axis=-1, keepdims=True)
    return x * lax.rsqrt(ms + EPS) * g


def _head_sumsq(x, bd):
    w = x.shape[1]
    outs = []
    step = 2 * LANES if w % (2 * LANES) == 0 else LANES
    for c in range(0, w, step):
        sq = x[:, c:c + step]
        sq = sq * sq
        hi = sq.astype(BF16)
        lo = (sq - hi.astype(F32)).astype(BF16)
        b = bd[:step, :step]
        outs.append(_dot(hi, b) + _dot(lo, b))
    return outs[0] if len(outs) == 1 else jnp.concatenate(outs, axis=1)


def _head_rms(x, g, bd):
    return x * lax.rsqrt(_head_sumsq(x, bd) * (1.0 / HEAD) + EPS) * g


def _rope(x, cos, sin_signed, half):
    r = x.shape[0]
    lane = lax.broadcasted_iota(jnp.int32, (r, LANES), 1)
    first = (lane & (2 * half - 1)) < half
    outs = []
    for c in range(0, x.shape[1], LANES):
        xc = x[:, c:c + LANES]
        partner = jnp.where(first, pltpu.roll(xc, LANES - half, 1), pltpu.roll(xc, half, 1))
        outs.append(xc * cos + partner * sin_signed)
    return outs[0] if len(outs) == 1 else jnp.concatenate(outs, axis=1)


def _place_heads(x, target_half, out_ref):
    r = x.shape[0]
    low = lax.broadcasted_iota(jnp.int32, (r, LANES), 1) < HEAD
    for h in range(x.shape[1] // HEAD):
        xc = x[:, (h // 2) * LANES:(h // 2 + 1) * LANES]
        th = target_half(h)
        if th != h % 2:
            xc = pltpu.roll(xc, HEAD, 1)
        keep = low if th == 0 else jnp.logical_not(low)
        out_ref[h] = jnp.where(keep, xc, 0.0).astype(BF16)


def _silu(z):
    return z * (1.0 / (1.0 + jnp.exp(-z)))


def _in_proj_kernel(x_ref, g_ref, w_ref, u_ref):
    h = _rms_rows(x_ref[...], g_ref[...]).astype(BF16)
    u_ref[...] = _dot(h, w_ref[...])


def _in_proj(x2, g, w):
    t = x2.shape[0]
    return pl.pallas_call(
        _in_proj_kernel,
        grid=(t // ROW_TILE,),
        in_specs=[
            pl.BlockSpec((ROW_TILE, D_MODEL), lambda i: (i, 0)),
            pl.BlockSpec((1, D_MODEL), lambda i: (0, 0)),
            pl.BlockSpec((D_MODEL, IN_W), lambda i: (0, 0)),
        ],
        out_specs=pl.BlockSpec((ROW_TILE, IN_W), lambda i: (i, 0)),
        out_shape=jax.ShapeDtypeStruct((t, IN_W), F32),
        compiler_params=_params(1),
        name="in_proj",
    )(x2, g, w)


def _memkv_kernel(mem_ref, g_ref, w_ref, gk_ref, bd_ref, mk_ref, mvt_ref):
    h = _rms_rows(mem_ref[0], g_ref[0]).astype(BF16)
    mkv = _dot(h, w_ref[0])
    half = mkv.shape[1] // 2
    mk = _head_rms(mkv[:, :half], gk_ref[0], bd_ref[...])
    mk_ref[0, 0] = mk.astype(BF16)
    mvt_ref[0, 0] = mkv[:, half:].T.astype(BF16)


def _memkv(mem, mem_norm_g, w_mem_kv, gk_tiled, bd):
    b, n_mem, _ = mem.shape
    width = MEM_HEADS * HEAD
    return pl.pallas_call(
        _memkv_kernel,
        grid=(DEPTH, b),
        in_specs=[
            pl.BlockSpec((1, n_mem, D_MODEL), lambda l, i: (i, 0, 0)),
            pl.BlockSpec((1, 1, D_MODEL), lambda l, i: (l, 0, 0)),
            pl.BlockSpec((1, D_MODEL, 2 * width), lambda l, i: (l, 0, 0)),
            pl.BlockSpec((1, 1, width), lambda l, i: (l, 0, 0)),
            pl.BlockSpec((2 * LANES, 2 * LANES), lambda l, i: (0, 0)),
        ],
        out_specs=[
            pl.BlockSpec((1, 1, n_mem, width), lambda l, i: (l, i, 0, 0)),
            pl.BlockSpec((1, 1, width, n_mem), lambda l, i: (l, i, 0, 0)),
        ],
        out_shape=[
            jax.ShapeDtypeStruct((DEPTH, b, n_mem, width), BF16),
            jax.ShapeDtypeStruct((DEPTH, b, width, n_mem), BF16),
        ],
        compiler_params=_params(2),
        name="mem_kv",
    )(mem, mem_norm_g.reshape(DEPTH, 1, D_MODEL), w_mem_kv, gk_tiled, bd)


def _prep_even_kernel(u_ref, gcp_ref, hcp_ref, gcn_ref, hcn_ref, cos_ref, sin_ref, gq_ref, gk_ref,
                      gmq_ref, cw_ref, bd_ref,
                      y1_ref, qz_ref, k_ref, vt_ref, mqz_ref, gate_ref, conv_sc, *, tiles_per_seq):
    i = pl.program_id(0)
    pos_tile = i % tiles_per_seq
    bd = bd_ref[...]
    cos = cos_ref[...]
    sin = sin_ref[...]
    rows = u_ref.shape[0]

    inner = u_ref[:, 512:1024] * u_ref[:, 1024:1536]
    prev_row = gcp_ref[SUBLANES - 1:SUBLANES, :] * hcp_ref[SUBLANES - 1:SUBLANES, :]
    next_row = gcn_ref[0:1, :] * hcn_ref[0:1, :]
    prev_row = jnp.where(pos_tile == 0, 0.0, prev_row)
    next_row = jnp.where(pos_tile == tiles_per_seq - 1, 0.0, next_row)
    conv_sc[pl.ds(SUBLANES, rows), :] = inner
    conv_sc[pl.ds(SUBLANES - 1, 1), :] = prev_row
    conv_sc[pl.ds(SUBLANES + rows, 1), :] = next_row
    cw = cw_ref[...]
    conv = (conv_sc[pl.ds(SUBLANES - 1, rows), :] * cw[0:1, :] + inner * cw[1:2, :]
            + conv_sc[pl.ds(SUBLANES + 1, rows), :] * cw[2:3, :])
    z = u_ref[:, GATE_OFF:IN_W]
    y1_ref[...] = (u_ref[:, 0:512] * conv * _silu(z[:, 0:512])).astype(BF16)
    gate_ref[...] = _silu(z[:, 512:MIX_W])

    q = _rope(_head_rms(u_ref[:, 1536:2048], gq_ref[...], bd), cos, sin, HEAD // 2) * Q_SCALE
    _place_heads(q, lambda h: h // GQA_GROUP, qz_ref)
    k = _rope(_head_rms(u_ref[:, 2048:2176], gk_ref[...], bd), cos, sin, HEAD // 2)
    k_ref[...] = k.astype(BF16)
    vt_ref[0] = u_ref[:, 2176:2304].T.astype(BF16)

    mq = _head_rms(u_ref[:, MQ_OFF:GATE_OFF], gmq_ref[...], bd) * Q_SCALE
    _place_heads(mq, lambda h: h % 2, mqz_ref)


def _prep_even(u, seq, cos, sin, gq, gk, gmq, conv_w, bd):
    t = u.shape[0]
    n_tiles = t // ROW_TILE
    tiles_per_seq = seq // ROW_TILE
    sub_per_tile = ROW_TILE // SUBLANES
    last_sub = t // SUBLANES - 1
    prev_map = lambda c: (lambda i: (jnp.maximum(i * sub_per_tile - 1, 0), c))
    next_map = lambda c: (lambda i: (jnp.minimum((i + 1) * sub_per_tile, last_sub), c))
    row = lambda w: pl.BlockSpec((1, w), lambda i: (0, 0))
    tab = pl.BlockSpec((ROW_TILE, LANES), lambda i: (i % tiles_per_seq, 0))
    return pl.pallas_call(
        functools.partial(_prep_even_kernel, tiles_per_seq=tiles_per_seq),
        grid=(n_tiles,),
        in_specs=[
            pl.BlockSpec((ROW_TILE, IN_W), lambda i: (i, 0)),
            pl.BlockSpec((SUBLANES, CONV_W), prev_map(1)),
            pl.BlockSpec((SUBLANES, CONV_W), prev_map(2)),
            pl.BlockSpec((SUBLANES, CONV_W), next_map(1)),
            pl.BlockSpec((SUBLANES, CONV_W), next_map(2)),
            tab, tab, row(512), row(LANES), row(256),
            pl.BlockSpec((3, CONV_W), lambda i: (0, 0)),
            pl.BlockSpec((2 * LANES, 2 * LANES), lambda i: (0, 0)),
        ],
        out_specs=[
            pl.BlockSpec((ROW_TILE, CONV_W), lambda i: (i, 0)),
            pl.BlockSpec((N_Q_HEADS, ROW_TILE, LANES), lambda i: (0, i, 0)),
            pl.BlockSpec((ROW_TILE, LANES), lambda i: (i, 0)),
            pl.BlockSpec((1, LANES, ROW_TILE), lambda i: (i, 0, 0)),
            pl.BlockSpec((MEM_HEADS, ROW_TILE, LANES), lambda i: (0, i, 0)),
            pl.BlockSpec((ROW_TILE, 768), lambda i: (i, 0)),
        ],
        out_shape=[
            jax.ShapeDtypeStruct((t, CONV_W), BF16),
            jax.ShapeDtypeStruct((N_Q_HEADS, t, LANES), BF16),
            jax.ShapeDtypeStruct((t, LANES), BF16),
            jax.ShapeDtypeStruct((n_tiles, LANES, ROW_TILE), BF16),
            jax.ShapeDtypeStruct((MEM_HEADS, t, LANES), BF16),
            jax.ShapeDtypeStruct((t, 768), F32),
        ],
        scratch_shapes=[pltpu.VMEM((ROW_TILE + 2 * SUBLANES, CONV_W), F32)],
        compiler_params=_params(1),
        name="prep_even",
    )(u, u, u, u, u, cos, sin, gq, gk, gmq, conv_w, bd)


def _mem_attention_t(mqz_ref, mk_ref, mvt_ref):
    outs = []
    for h in range(MEM_HEADS):
        grp = h // 2
        s = _dot_nt(mk_ref[0, :, grp * LANES:(grp + 1) * LANES], mqz_ref[h])
        m = jnp.max(s, axis=0, keepdims=True)
        p = jnp.exp2(s - m)
        l = jnp.sum(p, axis=0, keepdims=True)
        o = _dot(mvt_ref[0, h * HEAD:(h + 1) * HEAD, :], p.astype(BF16))
        outs.append(o * (1.0 / l))
    return jnp.concatenate(outs, axis=0)


def _window_kernel(sink_ref, qz_ref, kp_ref, kc_ref, kn_ref, vp_ref, vc_ref, vn_ref, mqz_ref, mk_ref,
                   mvt_ref, g2_ref, gm_ref, y2_ref, ym_ref, *, seq):
    tq = qz_ref.shape[1]
    t0 = pl.program_id(1) * tq
    kband = jnp.concatenate([kp_ref[...], kc_ref[...], kn_ref[...]], axis=0)
    vband = jnp.concatenate([vp_ref[0], vc_ref[0], vn_ref[0]], axis=1)
    nk = tq + 2 * WINDOW
    j = lax.broadcasted_iota(jnp.int32, (nk, tq), 0)
    i = lax.broadcasted_iota(jnp.int32, (nk, tq), 1)
    kpos = j + (t0 - WINDOW)
    mask = (j >= i) & (j <= i + 2 * WINDOW) & (kpos >= 0) & (kpos < seq)
    outs = []
    for g in range(N_Q_HEADS):
        kv = g // GQA_GROUP
        s = jnp.where(mask, _dot_nt(kband, qz_ref[g]), NEG)
        sink = sink_ref[g] * LOG2E
        m = jnp.maximum(jnp.max(s, axis=0, keepdims=True), sink)
        p = jnp.exp2(s - m)
        l = jnp.sum(p, axis=0, keepdims=True) + jnp.exp2(sink - m)
        o = _dot(vband[kv * HEAD:(kv + 1) * HEAD, :], p.astype(BF16))
        outs.append(o * (1.0 / l))
    y2 = jnp.concatenate(outs, axis=0).T
    y2_ref[...] = (y2 * g2_ref[...]).astype(BF16)
    ym = _mem_attention_t(mqz_ref, mk_ref, mvt_ref).T
    ym_ref[...] = (ym * gm_ref[...]).astype(BF16)


def _window_attention(sink, qz, k, vt, mqz, mk, mvt, gate, batch, seq):
    t = k.shape[0]
    tq = Q_TILE
    nq = seq // tq
    w_per_q = tq // WINDOW
    w_per_seq = seq // WINDOW
    q_per_chunk = ROW_TILE // tq
    w_per_chunk = ROW_TILE // WINDOW
    n_mem = mk.shape[1]

    def prev_w(b, q):
        return b * w_per_seq + jnp.maximum(q * w_per_q - 1, 0)

    def next_w(b, q):
        return b * w_per_seq + jnp.minimum((q + 1) * w_per_q, w_per_seq - 1)

    tok = lambda b, q: b * nq + q
    return pl.pallas_call(
        functools.partial(_window_kernel, seq=seq),
        grid=(batch, nq),
        in_specs=[
            pl.BlockSpec(memory_space=pltpu.SMEM),
            pl.BlockSpec((N_Q_HEADS, tq, LANES), lambda b, q: (0, tok(b, q), 0)),
            pl.BlockSpec((WINDOW, LANES), lambda b, q: (prev_w(b, q), 0)),
            pl.BlockSpec((tq, LANES), lambda b, q: (tok(b, q), 0)),
            pl.BlockSpec((WINDOW, LANES), lambda b, q: (next_w(b, q), 0)),
            pl.BlockSpec((1, LANES, WINDOW), lambda b, q: (prev_w(b, q) // w_per_chunk, 0, prev_w(b, q) % w_per_chunk)),
            pl.BlockSpec((1, LANES, tq), lambda b, q: (tok(b, q) // q_per_chunk, 0, tok(b, q) % q_per_chunk)),
            pl.BlockSpec((1, LANES, WINDOW), lambda b, q: (next_w(b, q) // w_per_chunk, 0, next_w(b, q) % w_per_chunk)),
            pl.BlockSpec((MEM_HEADS, tq, LANES), lambda b, q: (0, tok(b, q), 0)),
            pl.BlockSpec((1, n_mem, MEM_HEADS * HEAD), lambda b, q: (b, 0, 0)),
            pl.BlockSpec((1, MEM_HEADS * HEAD, n_mem), lambda b, q: (b, 0, 0)),
            pl.BlockSpec((tq, 512), lambda b, q: (tok(b, q), 0)),
            pl.BlockSpec((tq, 256), lambda b, q: (tok(b, q), 2)),
        ],
        out_specs=[
            pl.BlockSpec((tq, 512), lambda b, q: (tok(b, q), 0)),
            pl.BlockSpec((tq, 256), lambda b, q: (tok(b, q), 0)),
        ],
        out_shape=[
            jax.ShapeDtypeStruct((t, 512), BF16),
            jax.ShapeDtypeStruct((t, 256), BF16),
        ],
        compiler_params=_params(2),
        name="window_attention",
    )(sink, qz, k, k, k, vt, vt, vt, mqz, mk, mvt, gate, gate)


def _prep_odd_kernel(u_ref, cos1_ref, sin1_ref, cosa_ref, sina_ref, gq_ref, gk_ref, gdq_ref, gdk_ref,
                     gmq_ref, bd_ref,
                     qz_ref, k_ref, vt_ref, dqz_ref, dk_ref, dvt_ref, mqz_ref, gate_ref):
    bd = bd_ref[...]
    cos1, sin1 = cos1_ref[...], sin1_ref[...]
    cosa, sina = cosa_ref[...], sina_ref[...]
    gate_ref[...] = _silu(u_ref[:, GATE_OFF:IN_W])

    q = _rope(_head_rms(u_ref[:, 0:512], gq_ref[...], bd), cosa, sina, HEAD // 4) * Q_SCALE
    _place_heads(q, lambda h: h // GQA_GROUP, qz_ref)
    k = _rope(_head_rms(u_ref[:, 512:640], gk_ref[...], bd), cosa, sina, HEAD // 4)
    k_ref[...] = k.astype(BF16)
    vt_ref[0] = u_ref[:, 640:768].T.astype(BF16)

    dq = _rope(_head_rms(u_ref[:, 768:1280], gdq_ref[...], bd), cos1, sin1, HEAD // 2) * Q_SCALE
    _place_heads(dq, lambda h: h % 2, dqz_ref)
    dk = _rope(_head_rms(u_ref[:, 1280:1792], gdk_ref[...], bd), cos1, sin1, HEAD // 2)
    dk_ref[...] = dk.astype(BF16)
    dvt_ref[0] = u_ref[:, 1792:2304].T.astype(BF16)

    mq = _head_rms(u_ref[:, MQ_OFF:GATE_OFF], gmq_ref[...], bd) * Q_SCALE
    _place_heads(mq, lambda h: h % 2, mqz_ref)


def _prep_odd(u, seq, cos1, sin1, cosa, sina, gq, gk, gdq, gdk, gmq, bd):
    t = u.shape[0]
    n_tiles = t // ROW_TILE
    tiles_per_seq = seq // ROW_TILE
    row = lambda w: pl.BlockSpec((1, w), lambda i: (0, 0))
    tab = pl.BlockSpec((ROW_TILE, LANES), lambda i: (i % tiles_per_seq, 0))
    return pl.pallas_call(
        _prep_odd_kernel,
        grid=(n_tiles,),
        in_specs=[
            pl.BlockSpec((ROW_TILE, IN_W), lambda i: (i, 0)),
            tab, tab, tab, tab, row(512), row(LANES), row(512), row(512), row(256),
            pl.BlockSpec((2 * LANES, 2 * LANES), lambda i: (0, 0)),
        ],
        out_specs=[
            pl.BlockSpec((N_Q_HEADS, ROW_TILE, LANES), lambda i: (0, i, 0)),
            pl.BlockSpec((ROW_TILE, LANES), lambda i: (i, 0)),
            pl.BlockSpec((1, LANES, ROW_TILE), lambda i: (i, 0, 0)),
            pl.BlockSpec((N_Q_HEADS, ROW_TILE, LANES), lambda i: (0, i, 0)),
            pl.BlockSpec((ROW_TILE, 512), lambda i: (i, 0)),
            pl.BlockSpec((1, 512, ROW_TILE), lambda i: (i, 0, 0)),
            pl.BlockSpec((MEM_HEADS, ROW_TILE, LANES), lambda i: (0, i, 0)),
            pl.BlockSpec((ROW_TILE, MIX_W), lambda i: (i, 0)),
        ],
        out_shape=[
            jax.ShapeDtypeStruct((N_Q_HEADS, t, LANES), BF16),
            jax.ShapeDtypeStruct((t, LANES), BF16),
            jax.ShapeDtypeStruct((n_tiles, LANES, ROW_TILE), BF16),
            jax.ShapeDtypeStruct((N_Q_HEADS, t, LANES), BF16),
            jax.ShapeDtypeStruct((t, 512), BF16),
            jax.ShapeDtypeStruct((n_tiles, 512, ROW_TILE), BF16),
            jax.ShapeDtypeStruct((MEM_HEADS, t, LANES), BF16),
            jax.ShapeDtypeStruct((t, MIX_W), F32),
        ],
        compiler_params=_params(1),
        name="prep_odd",
    )(u, cos1, sin1, cosa, sina, gq, gk, gdq, gdk, gmq, bd)


def _flash_t(qz_ref, k_ref, vt_ref, v_rows, m_sc, l_sc, acc_sc):
    n_maps = qz_ref.shape[0]
    n_chunks, _, chunk = vt_ref.shape
    m_sc[...] = jnp.full(m_sc.shape, NEG, F32)
    l_sc[...] = jnp.zeros(l_sc.shape, F32)
    acc_sc[...] = jnp.zeros(acc_sc.shape, F32)

    def body(c, carry):
        kc = k_ref[pl.ds(pl.multiple_of(c * chunk, chunk), chunk), :]
        vc = vt_ref[c]
        for g in range(n_maps):
            r0, nr = v_rows(g)
            s = _dot_nt(kc, qz_ref[g])
            m_old = m_sc[g]
            m_new = jnp.maximum(m_old, jnp.max(s, axis=0, keepdims=True))
            alpha = jnp.exp2(m_old - m_new)
            p = jnp.exp2(s - m_new)
            l_sc[g] = alpha * l_sc[g] + jnp.sum(p, axis=0, keepdims=True)
            acc_sc[g] = alpha * acc_sc[g] + _dot(vc[r0:r0 + nr, :], p.astype(BF16))
            m_sc[g] = m_new
        return carry

    lax.fori_loop(0, n_chunks, body, 0)


def _axial_kernel(qz_ref, k_ref, vt_ref, mqz_ref, mk_ref, mvt_ref, g1_ref, gm_ref, y1_ref, ym_ref,
                  m_sc, l_sc, acc_sc):
    _flash_t(qz_ref, k_ref, vt_ref, lambda g: ((g // GQA_GROUP) * HEAD, HEAD), m_sc, l_sc, acc_sc)
    outs = [acc_sc[g] * (1.0 / l_sc[g]) for g in range(N_Q_HEADS)]
    y1 = jnp.concatenate(outs, axis=0).T
    y1_ref[...] = (y1 * g1_ref[...]).astype(BF16)
    ym = _mem_attention_t(mqz_ref, mk_ref, mvt_ref).T
    ym_ref[...] = (ym * gm_ref[...]).astype(BF16)


def _axial_attention(qz, k, vt, mqz, mk, mvt, gate, batch, seq):
    t = k.shape[0]
    tq = Q_TILE
    nq = seq // tq
    n_chunks = seq // ROW_TILE
    n_mem = mk.shape[1]
    tok = lambda b, q: b * nq + q
    return pl.pallas_call(
        _axial_kernel,
        grid=(batch, nq),
        in_specs=[
            pl.BlockSpec((N_Q_HEADS, tq, LANES), lambda b, q: (0, tok(b, q), 0)),
            pl.BlockSpec((seq, LANES), lambda b, q: (b, 0)),
            pl.BlockSpec((n_chunks, LANES, ROW_TILE), lambda b, q: (b, 0, 0)),
            pl.BlockSpec((MEM_HEADS, tq, LANES), lambda b, q: (0, tok(b, q), 0)),
            pl.BlockSpec((1, n_mem, MEM_HEADS * HEAD), lambda b, q: (b, 0, 0)),
            pl.BlockSpec((1, MEM_HEADS * HEAD, n_mem), lambda b, q: (b, 0, 0)),
            pl.BlockSpec((tq, 512), lambda b, q: (tok(b, q), 0)),
            pl.BlockSpec((tq, 256), lambda b, q: (tok(b, q), 4)),
        ],
        out_specs=[
            pl.BlockSpec((tq, 512), lambda b, q: (tok(b, q), 0)),
            pl.BlockSpec((tq, 256), lambda b, q: (tok(b, q), 0)),
        ],
        out_shape=[
            jax.ShapeDtypeStruct((t, 512), BF16),
            jax.ShapeDtypeStruct((t, 256), BF16),
        ],
        scratch_shapes=[
            pltpu.VMEM((N_Q_HEADS, 1, tq), F32),
            pltpu.VMEM((N_Q_HEADS, 1, tq), F32),
            pltpu.VMEM((N_Q_HEADS, HEAD, tq), F32),
        ],
        compiler_params=_params(2),
        name="axial_attention",
    )(qz, k, vt, mqz, mk, mvt, gate, gate)


def _diff_kernel(qz_ref, k_ref, vt_ref, lam_ref, sg_ref, g2_ref, y2_ref, m_sc, l_sc, acc_sc,
                 *, lambda_init):
    _flash_t(qz_ref, k_ref, vt_ref, lambda g: (0, 2 * HEAD), m_sc, l_sc, acc_sc)
    lv = lam_ref[...]
    lam = (jnp.exp(jnp.sum(lv[0:1] * lv[1:2], axis=-1, keepdims=True))
           - jnp.exp(jnp.sum(lv[2:3] * lv[3:4], axis=-1, keepdims=True)) + lambda_init)
    o = acc_sc[0] * (1.0 / l_sc[0]) - lam * (acc_sc[1] * (1.0 / l_sc[1]))
    ms = jnp.mean(o * o, axis=0, keepdims=True)
    on = (o * lax.rsqrt(ms + EPS)).T
    y2_ref[...] = (on * sg_ref[...] * (1.0 - lambda_init) * g2_ref[...]).astype(BF16)


def _diff_attention(dqz, dk, dvt, lam, subln_g, gate, batch, seq, layer):
    t = dk.shape[0]
    tq = Q_TILE
    nq = seq // tq
    n_chunks = seq // ROW_TILE
    lambda_init = 0.8 - 0.6 * math.exp(-0.3 * layer)
    tok = lambda b, q: b * nq + q
    return pl.pallas_call(
        functools.partial(_diff_kernel, lambda_init=lambda_init),
        grid=(batch, DIFF_HEADS, nq),
        in_specs=[
            pl.BlockSpec((2, tq, LANES), lambda b, h, q: (h, tok(b, q), 0)),
            pl.BlockSpec((seq, LANES), lambda b, h, q: (b, h)),
            pl.BlockSpec((n_chunks, LANES, ROW_TILE), lambda b, h, q: (b, h, 0)),
            pl.BlockSpec((4, HEAD), lambda b, h, q: (0, 0)),
            pl.BlockSpec((1, 2 * HEAD), lambda b, h, q: (0, 0)),
            pl.BlockSpec((tq, LANES), lambda b, h, q: (tok(b, q), 4 + h)),
        ],
        out_specs=pl.BlockSpec((tq, LANES), lambda b, h, q: (tok(b, q), h)),
        out_shape=jax.ShapeDtypeStruct((t, 512), BF16),
        scratch_shapes=[
            pltpu.VMEM((2, 1, tq), F32),
            pltpu.VMEM((2, 1, tq), F32),
            pltpu.VMEM((2, 2 * HEAD, tq), F32),
        ],
        compiler_params=_params(3),
        name="diff_attention",
    )(dqz, dk, dvt, lam, subln_g.reshape(1, 2 * HEAD), gate)


def _out_proj_kernel(y1_ref, y2_ref, ym_ref, x_ref, w_ref, o_ref):
    acc = _dot(y1_ref[...], w_ref[0:512, :])
    acc = acc + _dot(y2_ref[...], w_ref[512:1024, :])
    acc = acc + _dot(ym_ref[...], w_ref[1024:MIX_W, :])
    o_ref[...] = x_ref[...] + acc


def _out_proj(y1, y2, ym, x2, w):
    t = x2.shape[0]
    return pl.pallas_call(
        _out_proj_kernel,
        grid=(t // ROW_TILE,),
        in_specs=[
            pl.BlockSpec((ROW_TILE, 512), lambda i: (i, 0)),
            pl.BlockSpec((ROW_TILE, 512), lambda i: (i, 0)),
            pl.BlockSpec((ROW_TILE, 256), lambda i: (i, 0)),
            pl.BlockSpec((ROW_TILE, D_MODEL), lambda i: (i, 0)),
            pl.BlockSpec((MIX_W, D_MODEL), lambda i: (0, 0)),
        ],
        out_specs=pl.BlockSpec((ROW_TILE, D_MODEL), lambda i: (i, 0)),
        out_shape=jax.ShapeDtypeStruct((t, D_MODEL), F32),
        compiler_params=_params(1),
        name="out_proj",
    )(y1, y2, ym, x2, w)


def _rope_angles(pos, dim):
    inv = ROPE_THETA ** (-jnp.arange(0, dim, 2, dtype=F32) / dim)
    return pos.astype(F32)[:, None] * inv[None, :]


def _rope_tables(seq):
    pos = jnp.arange(seq)
    a1 = _rope_angles(pos, HEAD)
    cos1 = jnp.concatenate([jnp.cos(a1), jnp.cos(a1)], axis=-1)
    sin1 = jnp.concatenate([-jnp.sin(a1), jnp.sin(a1)], axis=-1)
    ar = _rope_angles(pos // GRID_W, HEAD // 2)
    ac = _rope_angles(pos % GRID_W, HEAD // 2)
    cosa = jnp.concatenate([jnp.cos(ar), jnp.cos(ar), jnp.cos(ac), jnp.cos(ac)], axis=-1)
    sina = jnp.concatenate([-jnp.sin(ar), jnp.sin(ar), -jnp.sin(ac), jnp.sin(ac)], axis=-1)
    rep = lambda a: jnp.tile(a, (1, LANES // HEAD))
    return rep(cos1), rep(sin1), rep(cosa), rep(sina)


def _tile_gain(g, width):
    return jnp.tile(g.astype(F32), width // HEAD).reshape(1, width)


def _trunk(x, mem, p):
    batch, seq, _ = x.shape
    x2 = x.reshape(batch * seq, D_MODEL)
    cos1, sin1, cosa, sina = _rope_tables(seq)
    bd = p["bd"]
    gk_mem = jnp.stack([_tile_gain(p["mem_qk_g"][l, 1], 256) for l in range(DEPTH)])
    mk_all, mvt_all = _memkv(mem, p["mem_norm_g"], p["w_mem_kv"], gk_mem, bd)
    for l in range(DEPTH):
        u = _in_proj(x2, p["norm_g"][l].reshape(1, D_MODEL), p["w_in"][l])
        gmq = _tile_gain(p["mem_qk_g"][l, 0], 256)
        mk, mvt = mk_all[l], mvt_all[l]
        if l % 2 == 0:
            e = l // 2
            y1, qz, k, vt, mqz, gate = _prep_even(
                u, seq, cos1, sin1, _tile_gain(p["swa_qk_g"][e, 0], 512),
                _tile_gain(p["swa_qk_g"][e, 1], LANES), gmq, p["conv_w"][e], bd)
            y2, ym = _window_attention(p["swa_sink"][e], qz, k, vt, mqz, mk, mvt, gate, batch, seq)
        else:
            o = l // 2
            qz, k, vt, dqz, dk, dvt, mqz, gate = _prep_odd(
                u, seq, cos1, sin1, cosa, sina, _tile_gain(p["ax_qk_g"][o, 0], 512),
                _tile_gain(p["ax_qk_g"][o, 1], LANES), _tile_gain(p["diff_qk_g"][o, 0], 512),
                _tile_gain(p["diff_qk_g"][o, 1], 512), gmq, bd)
            y1, ym = _axial_attention(qz, k, vt, mqz, mk, mvt, gate, batch, seq)
            y2 = _diff_attention(dqz, dk, dvt, p["diff_lambda"][o], p["diff_subln_g"][o], gate,
                                 batch, seq, l)
        x2 = _out_proj(y1, y2, ym, x2, p["w_out"][l])
    return x2.reshape(batch, seq, D_MODEL)


def kernel(x_prompt, x_sample, mem_prompt, mem_sample, norm_g, w_in, w_out, mem_norm_g, w_mem_kv,
           mem_qk_g, conv_w, swa_qk_g, swa_sink, ax_qk_g, diff_qk_g, diff_lambda, diff_subln_g):
    grp = jnp.arange(2 * LANES) // HEAD
    p = dict(
        norm_g=norm_g, w_in=w_in.astype(BF16), w_out=w_out.astype(BF16), mem_norm_g=mem_norm_g,
        w_mem_kv=w_mem_kv.astype(BF16), mem_qk_g=mem_qk_g, conv_w=conv_w, swa_qk_g=swa_qk_g,
        swa_sink=swa_sink, ax_qk_g=ax_qk_g, diff_qk_g=diff_qk_g, diff_lambda=diff_lambda,
        diff_subln_g=diff_subln_g, bd=(grp[:, None] == grp[None, :]).astype(BF16))
    return (_trunk(x_prompt, mem_prompt, p), _trunk(x_sample, mem_sample, p))
```

```python
import functools
import math

import jax
import jax.numpy as jnp
from jax import lax
from jax.experimental import pallas as pl
from jax.experimental.pallas import tpu as pltpu

F32 = jnp.float32
BF16 = jnp.bfloat16

D_MODEL = 1024
DEPTH = 4
HEAD = 64
LANES = 128
SUBLANES = 8
N_Q_HEADS = 8
GQA_GROUP = 4
DIFF_HEADS = 4
MEM_HEADS = 4
CONV_W = 512
WINDOW = 128
GRID_W = 64
ROPE_THETA = 10000.0
EPS = 1e-6
NEG = -1e30
LOG2E = 1.4426950408889634
Q_SCALE = HEAD ** -0.5 * LOG2E
SAFE_SCORE_BOUND = 60.0

IN_W = 3840
MIX_W = 1280
GATE_OFF = 2560
MQ_OFF = 2304

ROW_TILE = 512
Q_TILE = 256
STREAM_Q_TILE = 512
KV_CHUNK = 256
KV_UNROLL = 16
VMEM_LIMIT = 56 * 1024 * 1024

_NT = (((1,), (1,)), ((), ()))


def _params(n_axes):
    return pltpu.CompilerParams(dimension_semantics=("arbitrary",) * n_axes, vmem_limit_bytes=VMEM_LIMIT)


def _dot(a, b):
    return jnp.dot(a, b, preferred_element_type=F32)


def _dot_nt(a, b):
    return lax.dot_general(a, b, _NT, preferred_element_type=F32)


def _rms_rows(x, g):
    ms = jnp.mean(x * x, axis=-1, keepdims=True)
    return x * lax.rsqrt(ms + EPS) * g


def _head_sumsq(x, bd):
    w = x.shape[1]
    outs = []
    step = 2 * LANES if w % (2 * LANES) == 0 else LANES
    for c in range(0, w, step):
        sq = x[:, c:c + step]
        sq = sq * sq
        hi = sq.astype(BF16)
        lo = (sq - hi.astype(F32)).astype(BF16)
        b = bd[:step, :step]
        outs.append(_dot(hi, b) + _dot(lo, b))
    return outs[0] if len(outs) == 1 else jnp.concatenate(outs, axis=1)


def _head_rms(x, g, bd):
    return x * lax.rsqrt(_head_sumsq(x, bd) * (1.0 / HEAD) + EPS) * g


def _rope(x, cos, sin_signed, half):
    r = x.shape[0]
    lane = lax.broadcasted_iota(jnp.int32, (r, LANES), 1)
    first = (lane & (2 * half - 1)) < half
    outs = []
    for c in range(0, x.shape[1], LANES):
        xc = x[:, c:c + LANES]
        partner = jnp.where(first, pltpu.roll(xc, LANES - half, 1), pltpu.roll(xc, half, 1))
        outs.append(xc * cos + partner * sin_signed)
    return outs[0] if len(outs) == 1 else jnp.concatenate(outs, axis=1)


def _place_heads(x, target_half, out_ref):
    r = x.shape[0]
    low = lax.broadcasted_iota(jnp.int32, (r, LANES), 1) < HEAD
    for h in range(x.shape[1] // HEAD):
        xc = x[:, (h // 2) * LANES:(h // 2 + 1) * LANES]
        th = target_half(h)
        if th != h % 2:
            xc = pltpu.roll(xc, HEAD, 1)
        keep = low if th == 0 else jnp.logical_not(low)
        out_ref[h] = jnp.where(keep, xc, 0.0).astype(BF16)


def _silu(z):
    return z * (1.0 / (1.0 + jnp.exp(-z)))


def _in_proj_kernel(x_ref, g_ref, w_ref, u_ref):
    h = _rms_rows(x_ref[...], g_ref[...]).astype(BF16)
    u_ref[...] = _dot(h, w_ref[...])


def _in_proj(x2, g, w):
    t = x2.shape[0]
    return pl.pallas_call(
        _in_proj_kernel,
        grid=(t // ROW_TILE,),
        in_specs=[
            pl.BlockSpec((ROW_TILE, D_MODEL), lambda i: (i, 0)),
            pl.BlockSpec((1, D_MODEL), lambda i: (0, 0)),
            pl.BlockSpec((D_MODEL, IN_W), lambda i: (0, 0)),
        ],
        out_specs=pl.BlockSpec((ROW_TILE, IN_W), lambda i: (i, 0)),
        out_shape=jax.ShapeDtypeStruct((t, IN_W), F32),
        compiler_params=_params(1),
        name="in_proj",
    )(x2, g, w)


def _memkv_kernel(mem_ref, g_ref, w_ref, gk_ref, bd_ref, mk_ref, mvt_ref):
    h = _rms_rows(mem_ref[0], g_ref[0]).astype(BF16)
    mkv = _dot(h, w_ref[0])
    half = mkv.shape[1] // 2
    mk = _head_rms(mkv[:, :half], gk_ref[0], bd_ref[...])
    mk_ref[0, 0] = mk.astype(BF16)
    mvt_ref[0, 0] = mkv[:, half:].T.astype(BF16)


def _memkv(mem, mem_norm_g, w_mem_kv, gk_tiled, bd):
    b, n_mem, _ = mem.shape
    width = MEM_HEADS * HEAD
    return pl.pallas_call(
        _memkv_kernel,
        grid=(DEPTH, b),
        in_specs=[
            pl.BlockSpec((1, n_mem, D_MODEL), lambda l, i: (i, 0, 0)),
            pl.BlockSpec((1, 1, D_MODEL), lambda l, i: (l, 0, 0)),
            pl.BlockSpec((1, D_MODEL, 2 * width), lambda l, i: (l, 0, 0)),
            pl.BlockSpec((1, 1, width), lambda l, i: (l, 0, 0)),
            pl.BlockSpec((2 * LANES, 2 * LANES), lambda l, i: (0, 0)),
        ],
        out_specs=[
            pl.BlockSpec((1, 1, n_mem, width), lambda l, i: (l, i, 0, 0)),
            pl.BlockSpec((1, 1, width, n_mem), lambda l, i: (l, i, 0, 0)),
        ],
        out_shape=[
            jax.ShapeDtypeStruct((DEPTH, b, n_mem, width), BF16),
            jax.ShapeDtypeStruct((DEPTH, b, width, n_mem), BF16),
        ],
        compiler_params=_params(2),
        name="mem_kv",
    )(mem, mem_norm_g.reshape(DEPTH, 1, D_MODEL), w_mem_kv, gk_tiled, bd)


def _prep_even_kernel(u_ref, gcp_ref, hcp_ref, gcn_ref, hcn_ref, cos_ref, sin_ref, gq_ref, gk_ref,
                      gmq_ref, cw_ref, bd_ref,
                      y1_ref, qz_ref, k_ref, vt_ref, mqz_ref, gate_ref, conv_sc, *, tiles_per_seq):
    i = pl.program_id(0)
    pos_tile = i % tiles_per_seq
    bd = bd_ref[...]
    cos = cos_ref[...]
    sin = sin_ref[...]
    rows = u_ref.shape[0]

    inner = u_ref[:, 512:1024] * u_ref[:, 1024:1536]
    prev_row = gcp_ref[SUBLANES - 1:SUBLANES, :] * hcp_ref[SUBLANES - 1:SUBLANES, :]
    next_row = gcn_ref[0:1, :] * hcn_ref[0:1, :]
    prev_row = jnp.where(pos_tile == 0, 0.0, prev_row)
    next_row = jnp.where(pos_tile == tiles_per_seq - 1, 0.0, next_row)
    conv_sc[pl.ds(SUBLANES, rows), :] = inner
    conv_sc[pl.ds(SUBLANES - 1, 1), :] = prev_row
    conv_sc[pl.ds(SUBLANES + rows, 1), :] = next_row
    cw = cw_ref[...]
    conv = (conv_sc[pl.ds(SUBLANES - 1, rows), :] * cw[0:1, :] + inner * cw[1:2, :]
            + conv_sc[pl.ds(SUBLANES + 1, rows), :] * cw[2:3, :])
    z = u_ref[:, GATE_OFF:IN_W]
    y1_ref[...] = (u_ref[:, 0:512] * conv * _silu(z[:, 0:512])).astype(BF16)
    gate_ref[...] = _silu(z[:, 512:MIX_W])

    q = _rope(_head_rms(u_ref[:, 1536:2048], gq_ref[...], bd), cos, sin, HEAD // 2) * Q_SCALE
    _place_heads(q, lambda h: h // GQA_GROUP, qz_ref)
    k = _rope(_head_rms(u_ref[:, 2048:2176], gk_ref[...], bd), cos, sin, HEAD // 2)
    k_ref[...] = k.astype(BF16)
    vt_ref[0] = u_ref[:, 2176:2304].T.astype(BF16)

    mq = _head_rms(u_ref[:, MQ_OFF:GATE_OFF], gmq_ref[...], bd) * Q_SCALE
    _place_heads(mq, lambda h: h % 2, mqz_ref)


def _prep_even(u, seq, cos, sin, gq, gk, gmq, conv_w, bd):
    t = u.shape[0]
    n_tiles = t // ROW_TILE
    tiles_per_seq = seq // ROW_TILE
    sub_per_tile = ROW_TILE // SUBLANES
    last_sub = t // SUBLANES - 1
    prev_map = lambda c: (lambda i: (jnp.maximum(i * sub_per_tile - 1, 0), c))
    next_map = lambda c: (lambda i: (jnp.minimum((i + 1) * sub_per_tile, last_sub), c))
    row = lambda w: pl.BlockSpec((1, w), lambda i: (0, 0))
    tab = pl.BlockSpec((ROW_TILE, LANES), lambda i: (i % tiles_per_seq, 0))
    return pl.pallas_call(
        functools.partial(_prep_even_kernel, tiles_per_seq=tiles_per_seq),
        grid=(n_tiles,),
        in_specs=[
            pl.BlockSpec((ROW_TILE, IN_W), lambda i: (i, 0)),
            pl.BlockSpec((SUBLANES, CONV_W), prev_map(1)),
            pl.BlockSpec((SUBLANES, CONV_W), prev_map(2)),
            pl.BlockSpec((SUBLANES, CONV_W), next_map(1)),
            pl.BlockSpec((SUBLANES, CONV_W), next_map(2)),
            tab, tab, row(512), row(LANES), row(256),
            pl.BlockSpec((3, CONV_W), lambda i: (0, 0)),
            pl.BlockSpec((2 * LANES, 2 * LANES), lambda i: (0, 0)),
        ],
        out_specs=[
            pl.BlockSpec((ROW_TILE, CONV_W), lambda i: (i, 0)),
            pl.BlockSpec((N_Q_HEADS, ROW_TILE, LANES), lambda i: (0, i, 0)),
            pl.BlockSpec((ROW_TILE, LANES), lambda i: (i, 0)),
            pl.BlockSpec((1, LANES, ROW_TILE), lambda i: (i, 0, 0)),
            pl.BlockSpec((MEM_HEADS, ROW_TILE, LANES), lambda i: (0, i, 0)),
            pl.BlockSpec((ROW_TILE, 768), lambda i: (i, 0)),
        ],
        out_shape=[
            jax.ShapeDtypeStruct((t, CONV_W), BF16),
            jax.ShapeDtypeStruct((N_Q_HEADS, t, LANES), BF16),
            jax.ShapeDtypeStruct((t, LANES), BF16),
            jax.ShapeDtypeStruct((n_tiles, LANES, ROW_TILE), BF16),
            jax.ShapeDtypeStruct((MEM_HEADS, t, LANES), BF16),
            jax.ShapeDtypeStruct((t, 768), F32),
        ],
        scratch_shapes=[pltpu.VMEM((ROW_TILE + 2 * SUBLANES, CONV_W), F32)],
        compiler_params=_params(1),
        name="prep_even",
    )(u, u, u, u, u, cos, sin, gq, gk, gmq, conv_w, bd)


def _mem_attention_t(mqz_ref, mk_ref, mvt_ref):
    outs = []
    for h in range(MEM_HEADS):
        grp = h // 2
        s = _dot_nt(mk_ref[0, :, grp * LANES:(grp + 1) * LANES], mqz_ref[h])
        m = jnp.max(s, axis=0, keepdims=True)
        p = jnp.exp2(s - m)
        l = jnp.sum(p, axis=0, keepdims=True)
        o = _dot(mvt_ref[0, h * HEAD:(h + 1) * HEAD, :], p.astype(BF16))
        outs.append(o * (1.0 / l))
    return jnp.concatenate(outs, axis=0)


def _window_kernel(sink_ref, qz_ref, kp_ref, kc_ref, kn_ref, vp_ref, vc_ref, vn_ref, mqz_ref, mk_ref,
                   mvt_ref, g2_ref, gm_ref, y2_ref, ym_ref, *, seq):
    tq = qz_ref.shape[1]
    t0 = pl.program_id(1) * tq
    kband = jnp.concatenate([kp_ref[...], kc_ref[...], kn_ref[...]], axis=0)
    vband = jnp.concatenate([vp_ref[0], vc_ref[0], vn_ref[0]], axis=1)
    nk = tq + 2 * WINDOW
    j = lax.broadcasted_iota(jnp.int32, (nk, tq), 0)
    i = lax.broadcasted_iota(jnp.int32, (nk, tq), 1)
    kpos = j + (t0 - WINDOW)
    mask = (j >= i) & (j <= i + 2 * WINDOW) & (kpos >= 0) & (kpos < seq)
    outs = []
    for g in range(N_Q_HEADS):
        kv = g // GQA_GROUP
        s = jnp.where(mask, _dot_nt(kband, qz_ref[g]), NEG)
        sink = sink_ref[g] * LOG2E
        m = jnp.maximum(jnp.max(s, axis=0, keepdims=True), sink)
        p = jnp.exp2(s - m)
        l = jnp.sum(p, axis=0, keepdims=True) + jnp.exp2(sink - m)
        o = _dot(vband[kv * HEAD:(kv + 1) * HEAD, :], p.astype(BF16))
        outs.append(o * (1.0 / l))
    y2 = jnp.concatenate(outs, axis=0).T
    y2_ref[...] = (y2 * g2_ref[...]).astype(BF16)
    ym = _mem_attention_t(mqz_ref, mk_ref, mvt_ref).T
    ym_ref[...] = (ym * gm_ref[...]).astype(BF16)


def _window_attention(sink, qz, k, vt, mqz, mk, mvt, gate, batch, seq):
    t = k.shape[0]
    tq = Q_TILE
    nq = seq // tq
    w_per_q = tq // WINDOW
    w_per_seq = seq // WINDOW
    q_per_chunk = ROW_TILE // tq
    w_per_chunk = ROW_TILE // WINDOW
    n_mem = mk.shape[1]

    def prev_w(b, q):
        return b * w_per_seq + jnp.maximum(q * w_per_q - 1, 0)

    def next_w(b, q):
        return b * w_per_seq + jnp.minimum((q + 1) * w_per_q, w_per_seq - 1)

    tok = lambda b, q: b * nq + q
    return pl.pallas_call(
        functools.partial(_window_kernel, seq=seq),
        grid=(batch, nq),
        in_specs=[
            pl.BlockSpec(memory_space=pltpu.SMEM),
            pl.BlockSpec((N_Q_HEADS, tq, LANES), lambda b, q: (0, tok(b, q), 0)),
            pl.BlockSpec((WINDOW, LANES), lambda b, q: (prev_w(b, q), 0)),
            pl.BlockSpec((tq, LANES), lambda b, q: (tok(b, q), 0)),
            pl.BlockSpec((WINDOW, LANES), lambda b, q: (next_w(b, q), 0)),
            pl.BlockSpec((1, LANES, WINDOW), lambda b, q: (prev_w(b, q) // w_per_chunk, 0, prev_w(b, q) % w_per_chunk)),
            pl.BlockSpec((1, LANES, tq), lambda b, q: (tok(b, q) // q_per_chunk, 0, tok(b, q) % q_per_chunk)),
            pl.BlockSpec((1, LANES, WINDOW), lambda b, q: (next_w(b, q) // w_per_chunk, 0, next_w(b, q) % w_per_chunk)),
            pl.BlockSpec((MEM_HEADS, tq, LANES), lambda b, q: (0, tok(b, q), 0)),
            pl.BlockSpec((1, n_mem, MEM_HEADS * HEAD), lambda b, q: (b, 0, 0)),
            pl.BlockSpec((1, MEM_HEADS * HEAD, n_mem), lambda b, q: (b, 0, 0)),
            pl.BlockSpec((tq, 512), lambda b, q: (tok(b, q), 0)),
            pl.BlockSpec((tq, 256), lambda b, q: (tok(b, q), 2)),
        ],
        out_specs=[
            pl.BlockSpec((tq, 512), lambda b, q: (tok(b, q), 0)),
            pl.BlockSpec((tq, 256), lambda b, q: (tok(b, q), 0)),
        ],
        out_shape=[
            jax.ShapeDtypeStruct((t, 512), BF16),
            jax.ShapeDtypeStruct((t, 256), BF16),
        ],
        compiler_params=_params(2),
        name="window_attention",
    )(sink, qz, k, k, k, vt, vt, vt, mqz, mk, mvt, gate, gate)


def _prep_odd_kernel(u_ref, cos1_ref, sin1_ref, cosa_ref, sina_ref, gq_ref, gk_ref, gdq_ref, gdk_ref,
                     gmq_ref, bd_ref,
                     qz_ref, k_ref, vt_ref, dqz_ref, dk_ref, dvt_ref, mqz_ref, gate_ref):
    bd = bd_ref[...]
    cos1, sin1 = cos1_ref[...], sin1_ref[...]
    cosa, sina = cosa_ref[...], sina_ref[...]
    gate_ref[...] = _silu(u_ref[:, GATE_OFF:IN_W])

    q = _rope(_head_rms(u_ref[:, 0:512], gq_ref[...], bd), cosa, sina, HEAD // 4) * Q_SCALE
    _place_heads(q, lambda h: h // GQA_GROUP, qz_ref)
    k = _rope(_head_rms(u_ref[:, 512:640], gk_ref[...], bd), cosa, sina, HEAD // 4)
    k_ref[...] = k.astype(BF16)
    vt_ref[0] = u_ref[:, 640:768].T.astype(BF16)

    dq = _rope(_head_rms(u_ref[:, 768:1280], gdq_ref[...], bd), cos1, sin1, HEAD // 2) * Q_SCALE
    _place_heads(dq, lambda h: h % 2, dqz_ref)
    dk = _rope(_head_rms(u_ref[:, 1280:1792], gdk_ref[...], bd), cos1, sin1, HEAD // 2)
    dk_ref[...] = dk.astype(BF16)
    dvt_ref[0] = u_ref[:, 1792:2304].T.astype(BF16)

    mq = _head_rms(u_ref[:, MQ_OFF:GATE_OFF], gmq_ref[...], bd) * Q_SCALE
    _place_heads(mq, lambda h: h % 2, mqz_ref)


def _prep_odd(u, seq, cos1, sin1, cosa, sina, gq, gk, gdq, gdk, gmq, bd):
    t = u.shape[0]
    n_tiles = t // ROW_TILE
    tiles_per_seq = seq // ROW_TILE
    row = lambda w: pl.BlockSpec((1, w), lambda i: (0, 0))
    tab = pl.BlockSpec((ROW_TILE, LANES), lambda i: (i % tiles_per_seq, 0))
    return pl.pallas_call(
        _prep_odd_kernel,
        grid=(n_tiles,),
        in_specs=[
            pl.BlockSpec((ROW_TILE, IN_W), lambda i: (i, 0)),
            tab, tab, tab, tab, row(512), row(LANES), row(512), row(512), row(256),
            pl.BlockSpec((2 * LANES, 2 * LANES), lambda i: (0, 0)),
        ],
        out_specs=[
            pl.BlockSpec((N_Q_HEADS, ROW_TILE, LANES), lambda i: (0, i, 0)),
            pl.BlockSpec((ROW_TILE, LANES), lambda i: (i, 0)),
            pl.BlockSpec((1, LANES, ROW_TILE), lambda i: (i, 0, 0)),
            pl.BlockSpec((N_Q_HEADS, ROW_TILE, LANES), lambda i: (0, i, 0)),
            pl.BlockSpec((ROW_TILE, 512), lambda i: (i, 0)),
            pl.BlockSpec((1, 512, ROW_TILE), lambda i: (i, 0, 0)),
            pl.BlockSpec((MEM_HEADS, ROW_TILE, LANES), lambda i: (0, i, 0)),
            pl.BlockSpec((ROW_TILE, MIX_W), lambda i: (i, 0)),
        ],
        out_shape=[
            jax.ShapeDtypeStruct((N_Q_HEADS, t, LANES), BF16),
            jax.ShapeDtypeStruct((t, LANES), BF16),
            jax.ShapeDtypeStruct((n_tiles, LANES, ROW_TILE), BF16),
            jax.ShapeDtypeStruct((N_Q_HEADS, t, LANES), BF16),
            jax.ShapeDtypeStruct((t, 512), BF16),
            jax.ShapeDtypeStruct((n_tiles, 512, ROW_TILE), BF16),
            jax.ShapeDtypeStruct((MEM_HEADS, t, LANES), BF16),
            jax.ShapeDtypeStruct((t, MIX_W), F32),
        ],
        compiler_params=_params(1),
        name="prep_odd",
    )(u, cos1, sin1, cosa, sina, gq, gk, gdq, gdk, gmq, bd)


def _flash_t(qz_ref, k_ref, vt_ref, v_rows, m_sc, l_sc, acc_sc):
    n_maps = qz_ref.shape[0]
    n_chunks, _, chunk = vt_ref.shape
    m_sc[...] = jnp.full(m_sc.shape, NEG, F32)
    l_sc[...] = jnp.zeros(l_sc.shape, F32)
    acc_sc[...] = jnp.zeros(acc_sc.shape, F32)

    def body(c, carry):
        kc = k_ref[pl.ds(pl.multiple_of(c * chunk, chunk), chunk), :]
        vc = vt_ref[c]
        for g in range(n_maps):
            r0, nr = v_rows(g)
            s = _dot_nt(kc, qz_ref[g])
            m_old = m_sc[g]
            m_new = jnp.maximum(m_old, jnp.max(s, axis=0, keepdims=True))
            alpha = jnp.exp2(m_old - m_new)
            p = jnp.exp2(s - m_new)
            l_sc[g] = alpha * l_sc[g] + jnp.sum(p, axis=0, keepdims=True)
            acc_sc[g] = alpha * acc_sc[g] + _dot(vc[r0:r0 + nr, :], p.astype(BF16))
            m_sc[g] = m_new
        return carry

    lax.fori_loop(0, n_chunks, body, 0)


def _stream_t(qz_ref, k_ref, vt_ref, v_rows, acc_sc):
    n_maps, tq, _ = qz_ref.shape
    n_tiles, _, tile = vt_ref.shape
    per_tile = tile // KV_CHUNK
    unroll = math.gcd(KV_UNROLL, n_tiles * per_tile)
    assert unroll % per_tile == 0
    outs = []
    for g in range(n_maps):
        r0, nr = v_rows(g)
        qg = qz_ref[g]
        acc_sc[g] = jnp.zeros((nr, tq), F32)

        def body(i, l8, g=g, r0=r0, nr=nr, qg=qg):
            where = [(i * (unroll // per_tile) + u // per_tile, (u % per_tile) * KV_CHUNK)
                     for u in range(unroll)]
            scores = []
            for t, off in where:
                kc = k_ref[pl.ds(pl.multiple_of(t * tile + off, KV_CHUNK), KV_CHUNK), :]
                scores.append(_dot_nt(kc, qg))
            pv = None
            for (t, off), s in zip(where, scores):
                p = jnp.exp2(s)
                l8 = l8 + jnp.sum(p.reshape(KV_CHUNK // SUBLANES, SUBLANES, tq), axis=0)
                d = _dot(vt_ref[t, r0:r0 + nr, off:off + KV_CHUNK], p.astype(BF16))
                pv = d if pv is None else pv + d
            acc_sc[g] += pv
            return l8

        l8 = lax.fori_loop(0, n_tiles * per_tile // unroll, body, jnp.zeros((SUBLANES, tq), F32))
        outs.append(acc_sc[g] * (1.0 / jnp.sum(l8, axis=0, keepdims=True)))
    return outs


def _axial_kernel(qz_ref, k_ref, vt_ref, mqz_ref, mk_ref, mvt_ref, g1_ref, gm_ref, y1_ref, ym_ref,
                  *scratch, bounded):
    v_rows = lambda g: ((g // GQA_GROUP) * HEAD, HEAD)
    if bounded:
        outs = _stream_t(qz_ref, k_ref, vt_ref, v_rows, *scratch)
    else:
        m_sc, l_sc, acc_sc = scratch
        _flash_t(qz_ref, k_ref, vt_ref, v_rows, m_sc, l_sc, acc_sc)
        outs = [acc_sc[g] * (1.0 / l_sc[g]) for g in range(N_Q_HEADS)]
    y1 = jnp.concatenate(outs, axis=0).T
    y1_ref[...] = (y1 * g1_ref[...]).astype(BF16)
    ym = _mem_attention_t(mqz_ref, mk_ref, mvt_ref).T
    ym_ref[...] = (ym * gm_ref[...]).astype(BF16)


def _axial_attention(qz, k, vt, mqz, mk, mvt, gate, batch, seq, bounded):
    t = k.shape[0]
    tq = STREAM_Q_TILE if bounded else Q_TILE
    nq = seq // tq
    n_chunks = seq // ROW_TILE
    n_mem = mk.shape[1]
    tok = lambda b, q: b * nq + q
    scratch = [pltpu.VMEM((N_Q_HEADS, HEAD, tq), F32)] if bounded else [
        pltpu.VMEM((N_Q_HEADS, 1, tq), F32),
        pltpu.VMEM((N_Q_HEADS, 1, tq), F32),
        pltpu.VMEM((N_Q_HEADS, HEAD, tq), F32),
    ]
    return pl.pallas_call(
        functools.partial(_axial_kernel, bounded=bounded),
        grid=(batch, nq),
        in_specs=[
            pl.BlockSpec((N_Q_HEADS, tq, LANES), lambda b, q: (0, tok(b, q), 0)),
            pl.BlockSpec((seq, LANES), lambda b, q: (b, 0)),
            pl.BlockSpec((n_chunks, LANES, ROW_TILE), lambda b, q: (b, 0, 0)),
            pl.BlockSpec((MEM_HEADS, tq, LANES), lambda b, q: (0, tok(b, q), 0)),
            pl.BlockSpec((1, n_mem, MEM_HEADS * HEAD), lambda b, q: (b, 0, 0)),
            pl.BlockSpec((1, MEM_HEADS * HEAD, n_mem), lambda b, q: (b, 0, 0)),
            pl.BlockSpec((tq, 512), lambda b, q: (tok(b, q), 0)),
            pl.BlockSpec((tq, 256), lambda b, q: (tok(b, q), 4)),
        ],
        out_specs=[
            pl.BlockSpec((tq, 512), lambda b, q: (tok(b, q), 0)),
            pl.BlockSpec((tq, 256), lambda b, q: (tok(b, q), 0)),
        ],
        out_shape=[
            jax.ShapeDtypeStruct((t, 512), BF16),
            jax.ShapeDtypeStruct((t, 256), BF16),
        ],
        scratch_shapes=scratch,
        compiler_params=_params(2),
        name="axial_attention" if bounded else "axial_attention_online",
    )(qz, k, vt, mqz, mk, mvt, gate, gate)


def _diff_kernel(qz_ref, k_ref, vt_ref, lam_ref, sg_ref, g2_ref, y2_ref, *scratch, lambda_init, bounded):
    v_rows = lambda g: (0, 2 * HEAD)
    if bounded:
        o0, o1 = _stream_t(qz_ref, k_ref, vt_ref, v_rows, *scratch)
    else:
        m_sc, l_sc, acc_sc = scratch
        _flash_t(qz_ref, k_ref, vt_ref, v_rows, m_sc, l_sc, acc_sc)
        o0, o1 = (acc_sc[g] * (1.0 / l_sc[g]) for g in range(2))
    lv = lam_ref[...]
    lam = (jnp.exp(jnp.sum(lv[0:1] * lv[1:2], axis=-1, keepdims=True))
           - jnp.exp(jnp.sum(lv[2:3] * lv[3:4], axis=-1, keepdims=True)) + lambda_init)
    o = o0 - lam * o1
    ms = jnp.mean(o * o, axis=0, keepdims=True)
    on = (o * lax.rsqrt(ms + EPS)).T
    y2_ref[...] = (on * sg_ref[...] * (1.0 - lambda_init) * g2_ref[...]).astype(BF16)


def _diff_attention(dqz, dk, dvt, lam, subln_g, gate, batch, seq, layer, bounded):
    t = dk.shape[0]
    tq = STREAM_Q_TILE if bounded else Q_TILE
    nq = seq // tq
    n_chunks = seq // ROW_TILE
    lambda_init = 0.8 - 0.6 * math.exp(-0.3 * layer)
    tok = lambda b, q: b * nq + q
    scratch = [pltpu.VMEM((2, 2 * HEAD, tq), F32)] if bounded else [
        pltpu.VMEM((2, 1, tq), F32),
        pltpu.VMEM((2, 1, tq), F32),
        pltpu.VMEM((2, 2 * HEAD, tq), F32),
    ]
    return pl.pallas_call(
        functools.partial(_diff_kernel, lambda_init=lambda_init, bounded=bounded),
        grid=(batch, DIFF_HEADS, nq),
        in_specs=[
            pl.BlockSpec((2, tq, LANES), lambda b, h, q: (h, tok(b, q), 0)),
            pl.BlockSpec((seq, LANES), lambda b, h, q: (b, h)),
            pl.BlockSpec((n_chunks, LANES, ROW_TILE), lambda b, h, q: (b, h, 0)),
            pl.BlockSpec((4, HEAD), lambda b, h, q: (0, 0)),
            pl.BlockSpec((1, 2 * HEAD), lambda b, h, q: (0, 0)),
            pl.BlockSpec((tq, LANES), lambda b, h, q: (tok(b, q), 4 + h)),
        ],
        out_specs=pl.BlockSpec((tq, LANES), lambda b, h, q: (tok(b, q), h)),
        out_shape=jax.ShapeDtypeStruct((t, 512), BF16),
        scratch_shapes=scratch,
        compiler_params=_params(3),
        name="diff_attention" if bounded else "diff_attention_online",
    )(dqz, dk, dvt, lam, subln_g.reshape(1, 2 * HEAD), gate)


def _out_proj_kernel(y1_ref, y2_ref, ym_ref, x_ref, w_ref, o_ref):
    acc = _dot(y1_ref[...], w_ref[0:512, :])
    acc = acc + _dot(y2_ref[...], w_ref[512:1024, :])
    acc = acc + _dot(ym_ref[...], w_ref[1024:MIX_W, :])
    o_ref[...] = x_ref[...] + acc


def _out_proj(y1, y2, ym, x2, w):
    t = x2.shape[0]
    return pl.pallas_call(
        _out_proj_kernel,
        grid=(t // ROW_TILE,),
        in_specs=[
            pl.BlockSpec((ROW_TILE, 512), lambda i: (i, 0)),
            pl.BlockSpec((ROW_TILE, 512), lambda i: (i, 0)),
            pl.BlockSpec((ROW_TILE, 256), lambda i: (i, 0)),
            pl.BlockSpec((ROW_TILE, D_MODEL), lambda i: (i, 0)),
            pl.BlockSpec((MIX_W, D_MODEL), lambda i: (0, 0)),
        ],
        out_specs=pl.BlockSpec((ROW_TILE, D_MODEL), lambda i: (i, 0)),
        out_shape=jax.ShapeDtypeStruct((t, D_MODEL), F32),
        compiler_params=_params(1),
        name="out_proj",
    )(y1, y2, ym, x2, w)


def _rope_angles(pos, dim):
    inv = ROPE_THETA ** (-jnp.arange(0, dim, 2, dtype=F32) / dim)
    return pos.astype(F32)[:, None] * inv[None, :]


def _rope_tables(seq):
    pos = jnp.arange(seq)
    a1 = _rope_angles(pos, HEAD)
    cos1 = jnp.concatenate([jnp.cos(a1), jnp.cos(a1)], axis=-1)
    sin1 = jnp.concatenate([-jnp.sin(a1), jnp.sin(a1)], axis=-1)
    ar = _rope_angles(pos // GRID_W, HEAD // 2)
    ac = _rope_angles(pos % GRID_W, HEAD // 2)
    cosa = jnp.concatenate([jnp.cos(ar), jnp.cos(ar), jnp.cos(ac), jnp.cos(ac)], axis=-1)
    sina = jnp.concatenate([-jnp.sin(ar), jnp.sin(ar), -jnp.sin(ac), jnp.sin(ac)], axis=-1)
    rep = lambda a: jnp.tile(a, (1, LANES // HEAD))
    return rep(cos1), rep(sin1), rep(cosa), rep(sina)


def _scores_bounded(qk_g):
    bound = HEAD * Q_SCALE * 1.01 * jnp.max(jnp.abs(qk_g[0])) * jnp.max(jnp.abs(qk_g[1]))
    return bound <= SAFE_SCORE_BOUND


def _tile_gain(g, width):
    return jnp.tile(g.astype(F32), width // HEAD).reshape(1, width)


def _trunk(x, mem, p):
    batch, seq, _ = x.shape
    x2 = x.reshape(batch * seq, D_MODEL)
    cos1, sin1, cosa, sina = _rope_tables(seq)
    bd = p["bd"]
    gk_mem = jnp.stack([_tile_gain(p["mem_qk_g"][l, 1], 256) for l in range(DEPTH)])
    mk_all, mvt_all = _memkv(mem, p["mem_norm_g"], p["w_mem_kv"], gk_mem, bd)
    for l in range(DEPTH):
        u = _in_proj(x2, p["norm_g"][l].reshape(1, D_MODEL), p["w_in"][l])
        gmq = _tile_gain(p["mem_qk_g"][l, 0], 256)
        mk, mvt = mk_all[l], mvt_all[l]
        if l % 2 == 0:
            e = l // 2
            y1, qz, k, vt, mqz, gate = _prep_even(
                u, seq, cos1, sin1, _tile_gain(p["swa_qk_g"][e, 0], 512),
                _tile_gain(p["swa_qk_g"][e, 1], LANES), gmq, p["conv_w"][e], bd)
            y2, ym = _window_attention(p["swa_sink"][e], qz, k, vt, mqz, mk, mvt, gate, batch, seq)
        else:
            o = l // 2
            qz, k, vt, dqz, dk, dvt, mqz, gate = _prep_odd(
                u, seq, cos1, sin1, cosa, sina, _tile_gain(p["ax_qk_g"][o, 0], 512),
                _tile_gain(p["ax_qk_g"][o, 1], LANES), _tile_gain(p["diff_qk_g"][o, 0], 512),
                _tile_gain(p["diff_qk_g"][o, 1], 512), gmq, bd)
            y1, ym = lax.cond(
                _scores_bounded(p["ax_qk_g"][o]),
                functools.partial(_axial_attention, batch=batch, seq=seq, bounded=True),
                functools.partial(_axial_attention, batch=batch, seq=seq, bounded=False),
                qz, k, vt, mqz, mk, mvt, gate)
            y2 = lax.cond(
                _scores_bounded(p["diff_qk_g"][o]),
                functools.partial(_diff_attention, batch=batch, seq=seq, layer=l, bounded=True),
                functools.partial(_diff_attention, batch=batch, seq=seq, layer=l, bounded=False),
                dqz, dk, dvt, p["diff_lambda"][o], p["diff_subln_g"][o], gate)
        x2 = _out_proj(y1, y2, ym, x2, p["w_out"][l])
    return x2.reshape(batch, seq, D_MODEL)


def kernel(x_prompt, x_sample, mem_prompt, mem_sample, norm_g, w_in, w_out, mem_norm_g, w_mem_kv,
           mem_qk_g, conv_w, swa_qk_g, swa_sink, ax_qk_g, diff_qk_g, diff_lambda, diff_subln_g):
    grp = jnp.arange(2 * LANES) // HEAD
    p = dict(
        norm_g=norm_g, w_in=w_in.astype(BF16), w_out=w_out.astype(BF16), mem_norm_g=mem_norm_g,
        w_mem_kv=w_mem_kv.astype(BF16), mem_qk_g=mem_qk_g, conv_w=conv_w, swa_qk_g=swa_qk_g,
        swa_sink=swa_sink, ax_qk_g=ax_qk_g, diff_qk_g=diff_qk_g, diff_lambda=diff_lambda,
        diff_subln_g=diff_subln_g, bd=(grp[:, None] == grp[None, :]).astype(BF16))
    return (_trunk(x_prompt, mem_prompt, p), _trunk(x_sample, mem_sample, p))
```

```python
import functools
import math

import jax
import jax.numpy as jnp
from jax import lax
from jax.experimental import pallas as pl
from jax.experimental.pallas import tpu as pltpu

F32 = jnp.float32
BF16 = jnp.bfloat16

D_MODEL = 1024
DEPTH = 4
HEAD = 64
LANES = 128
SUBLANES = 8
N_Q_HEADS = 8
GQA_GROUP = 4
DIFF_HEADS = 4
MEM_HEADS = 4
CONV_W = 512
WINDOW = 128
GRID_W = 64
ROPE_THETA = 10000.0
EPS = 1e-6
NEG = -1e30
LOG2E = 1.4426950408889634
Q_SCALE = HEAD ** -0.5 * LOG2E
SAFE_SCORE_BOUND = 60.0

IN_W = 3840
MIX_W = 1280
GATE_OFF = 2560
MQ_OFF = 2304

ROW_TILE = 512
Q_TILE = 256
STREAM_Q_TILE = 512
KV_CHUNK = 256
KV_UNROLL = 16
VMEM_LIMIT = 56 * 1024 * 1024

_NT = (((1,), (1,)), ((), ()))


def _params(n_axes):
    return pltpu.CompilerParams(dimension_semantics=("arbitrary",) * n_axes, vmem_limit_bytes=VMEM_LIMIT)


def _dot(a, b):
    return jnp.dot(a, b, preferred_element_type=F32)


def _dot_nt(a, b):
    return lax.dot_general(a, b, _NT, preferred_element_type=F32)


def _rms_rows(x, g):
    ms = jnp.mean(x * x, axis=-1, keepdims=True)
    return x * lax.rsqrt(ms + EPS) * g


def _head_sumsq(x, bd):
    w = x.shape[1]
    outs = []
    step = 2 * LANES if w % (2 * LANES) == 0 else LANES
    for c in range(0, w, step):
        sq = x[:, c:c + step]
        sq = sq * sq
        hi = sq.astype(BF16)
        lo = (sq - hi.astype(F32)).astype(BF16)
        b = bd[:step, :step]
        outs.append(_dot(hi, b) + _dot(lo, b))
    return outs[0] if len(outs) == 1 else jnp.concatenate(outs, axis=1)


def _head_rms(x, g, bd):
    return x * lax.rsqrt(_head_sumsq(x, bd) * (1.0 / HEAD) + EPS) * g


def _rope(x, cos, sin_signed, half):
    r = x.shape[0]
    lane = lax.broadcasted_iota(jnp.int32, (r, LANES), 1)
    first = (lane & (2 * half - 1)) < half
    outs = []
    for c in range(0, x.shape[1], LANES):
        xc = x[:, c:c + LANES]
        partner = jnp.where(first, pltpu.roll(xc, LANES - half, 1), pltpu.roll(xc, half, 1))
        outs.append(xc * cos + partner * sin_signed)
    return outs[0] if len(outs) == 1 else jnp.concatenate(outs, axis=1)


def _place_heads(x, target_half, out_ref):
    r = x.shape[0]
    low = lax.broadcasted_iota(jnp.int32, (r, LANES), 1) < HEAD
    for h in range(x.shape[1] // HEAD):
        xc = x[:, (h // 2) * LANES:(h // 2 + 1) * LANES]
        th = target_half(h)
        if th != h % 2:
            xc = pltpu.roll(xc, HEAD, 1)
        keep = low if th == 0 else jnp.logical_not(low)
        out_ref[h] = jnp.where(keep, xc, 0.0).astype(BF16)


def _silu(z):
    return z * (1.0 / (1.0 + jnp.exp(-z)))


def _project(h, w_ref, splits):
    return [_dot(h, w_ref[:, a:b]) for a, b in splits]


def _memkv_kernel(mem_ref, g_ref, w_ref, gk_ref, bd_ref, mk_ref, mvt_ref):
    h = _rms_rows(mem_ref[0], g_ref[0]).astype(BF16)
    mkv = _dot(h, w_ref[0])
    half = mkv.shape[1] // 2
    mk = _head_rms(mkv[:, :half], gk_ref[0], bd_ref[...])
    mk_ref[0, 0] = mk.astype(BF16)
    mvt_ref[0, 0] = mkv[:, half:].T.astype(BF16)


def _memkv(mem, mem_norm_g, w_mem_kv, gk_tiled, bd):
    b, n_mem, _ = mem.shape
    width = MEM_HEADS * HEAD
    return pl.pallas_call(
        _memkv_kernel,
        grid=(DEPTH, b),
        in_specs=[
            pl.BlockSpec((1, n_mem, D_MODEL), lambda l, i: (i, 0, 0)),
            pl.BlockSpec((1, 1, D_MODEL), lambda l, i: (l, 0, 0)),
            pl.BlockSpec((1, D_MODEL, 2 * width), lambda l, i: (l, 0, 0)),
            pl.BlockSpec((1, 1, width), lambda l, i: (l, 0, 0)),
            pl.BlockSpec((2 * LANES, 2 * LANES), lambda l, i: (0, 0)),
        ],
        out_specs=[
            pl.BlockSpec((1, 1, n_mem, width), lambda l, i: (l, i, 0, 0)),
            pl.BlockSpec((1, 1, width, n_mem), lambda l, i: (l, i, 0, 0)),
        ],
        out_shape=[
            jax.ShapeDtypeStruct((DEPTH, b, n_mem, width), BF16),
            jax.ShapeDtypeStruct((DEPTH, b, width, n_mem), BF16),
        ],
        compiler_params=_params(2),
        name="mem_kv",
    )(mem, mem_norm_g.reshape(DEPTH, 1, D_MODEL), w_mem_kv, gk_tiled, bd)


def _proj_even_kernel(x_ref, xp_ref, xn_ref, g_ref, w_ref, cos_ref, sin_ref, gq_ref, gk_ref,
                      gmq_ref, cw_ref, bd_ref,
                      y1_ref, qz_ref, k_ref, vt_ref, mqz_ref, gate_ref, conv_sc, *, tiles_per_seq):
    i = pl.program_id(0)
    pos_tile = i % tiles_per_seq
    bd = bd_ref[...]
    cos = cos_ref[...]
    sin = sin_ref[...]
    rows = x_ref.shape[0]

    x_ext = jnp.concatenate([x_ref[...], xp_ref[...], xn_ref[...]], axis=0)
    h_ext = _rms_rows(x_ext, g_ref[...]).astype(BF16)
    h = h_ext[:rows]
    gb, qkv, mq, z = _project(h, w_ref, [(0, 512), (1536, 2304), (MQ_OFF, GATE_OFF), (GATE_OFF, IN_W)])
    gchc = _dot(h_ext, w_ref[:, 512:1536])

    inner_ext = gchc[:, 0:512] * gchc[:, 512:1024]
    inner = inner_ext[:rows]
    prev_row = inner_ext[rows + SUBLANES - 1:rows + SUBLANES]
    next_row = inner_ext[rows + SUBLANES:rows + SUBLANES + 1]
    prev_row = jnp.where(pos_tile == 0, 0.0, prev_row)
    next_row = jnp.where(pos_tile == tiles_per_seq - 1, 0.0, next_row)
    conv_sc[pl.ds(SUBLANES, rows), :] = inner
    conv_sc[pl.ds(SUBLANES - 1, 1), :] = prev_row
    conv_sc[pl.ds(SUBLANES + rows, 1), :] = next_row
    cw = cw_ref[...]
    conv = (conv_sc[pl.ds(SUBLANES - 1, rows), :] * cw[0:1, :] + inner * cw[1:2, :]
            + conv_sc[pl.ds(SUBLANES + 1, rows), :] * cw[2:3, :])
    y1_ref[...] = (gb * conv * _silu(z[:, 0:512])).astype(BF16)
    gate_ref[...] = _silu(z[:, 512:MIX_W])

    q = _rope(_head_rms(qkv[:, 0:512], gq_ref[...], bd), cos, sin, HEAD // 2) * Q_SCALE
    _place_heads(q, lambda h: h // GQA_GROUP, qz_ref)
    k = _rope(_head_rms(qkv[:, 512:640], gk_ref[...], bd), cos, sin, HEAD // 2)
    k_ref[...] = k.astype(BF16)
    vt_ref[0] = qkv[:, 640:768].T.astype(BF16)

    mq = _head_rms(mq, gmq_ref[...], bd) * Q_SCALE
    _place_heads(mq, lambda h: h % 2, mqz_ref)


def _proj_even(x2, seq, g, w, cos, sin, gq, gk, gmq, conv_w, bd):
    t = x2.shape[0]
    n_tiles = t // ROW_TILE
    tiles_per_seq = seq // ROW_TILE
    sub_per_tile = ROW_TILE // SUBLANES
    last_sub = t // SUBLANES - 1
    row = lambda w: pl.BlockSpec((1, w), lambda i: (0, 0))
    tab = pl.BlockSpec((ROW_TILE, LANES), lambda i: (i % tiles_per_seq, 0))
    return pl.pallas_call(
        functools.partial(_proj_even_kernel, tiles_per_seq=tiles_per_seq),
        grid=(n_tiles,),
        in_specs=[
            pl.BlockSpec((ROW_TILE, D_MODEL), lambda i: (i, 0)),
            pl.BlockSpec((SUBLANES, D_MODEL), lambda i: (jnp.maximum(i * sub_per_tile - 1, 0), 0)),
            pl.BlockSpec((SUBLANES, D_MODEL), lambda i: (jnp.minimum((i + 1) * sub_per_tile, last_sub), 0)),
            row(D_MODEL),
            pl.BlockSpec((D_MODEL, IN_W), lambda i: (0, 0)),
            tab, tab, row(512), row(LANES), row(256),
            pl.BlockSpec((3, CONV_W), lambda i: (0, 0)),
            pl.BlockSpec((2 * LANES, 2 * LANES), lambda i: (0, 0)),
        ],
        out_specs=[
            pl.BlockSpec((ROW_TILE, CONV_W), lambda i: (i, 0)),
            pl.BlockSpec((N_Q_HEADS, ROW_TILE, LANES), lambda i: (0, i, 0)),
            pl.BlockSpec((ROW_TILE, LANES), lambda i: (i, 0)),
            pl.BlockSpec((1, LANES, ROW_TILE), lambda i: (i, 0, 0)),
            pl.BlockSpec((MEM_HEADS, ROW_TILE, LANES), lambda i: (0, i, 0)),
            pl.BlockSpec((ROW_TILE, 768), lambda i: (i, 0)),
        ],
        out_shape=[
            jax.ShapeDtypeStruct((t, CONV_W), BF16),
            jax.ShapeDtypeStruct((N_Q_HEADS, t, LANES), BF16),
            jax.ShapeDtypeStruct((t, LANES), BF16),
            jax.ShapeDtypeStruct((n_tiles, LANES, ROW_TILE), BF16),
            jax.ShapeDtypeStruct((MEM_HEADS, t, LANES), BF16),
            jax.ShapeDtypeStruct((t, 768), F32),
        ],
        scratch_shapes=[pltpu.VMEM((ROW_TILE + 2 * SUBLANES, CONV_W), F32)],
        compiler_params=_params(1),
        name="proj_even",
    )(x2, x2, x2, g, w, cos, sin, gq, gk, gmq, conv_w, bd)


def _mem_attention_t(mqz_ref, mk_ref, mvt_ref, bounded=False):
    outs = []
    if bounded:
        tq = mqz_ref.shape[1]
        pairs = []
        for grp in range(MEM_HEADS // 2):
            q2 = jnp.concatenate([mqz_ref[2 * grp], mqz_ref[2 * grp + 1]], axis=0)
            pairs.append(jnp.exp2(_dot_nt(mk_ref[0, :, grp * LANES:(grp + 1) * LANES], q2)))
        for h in range(MEM_HEADS):
            p = pairs[h // 2][:, (h % 2) * tq:(h % 2 + 1) * tq]
            l = jnp.sum(p, axis=0, keepdims=True)
            o = _dot(mvt_ref[0, h * HEAD:(h + 1) * HEAD, :], p.astype(BF16))
            outs.append(o * (1.0 / l))
        return jnp.concatenate(outs, axis=0)
    for h in range(MEM_HEADS):
        grp = h // 2
        s = _dot_nt(mk_ref[0, :, grp * LANES:(grp + 1) * LANES], mqz_ref[h])
        m = jnp.max(s, axis=0, keepdims=True)
        p = jnp.exp2(s - m)
        l = jnp.sum(p, axis=0, keepdims=True)
        o = _dot(mvt_ref[0, h * HEAD:(h + 1) * HEAD, :], p.astype(BF16))
        outs.append(o * (1.0 / l))
    return jnp.concatenate(outs, axis=0)


def _window_kernel(sink_ref, qz_ref, kp_ref, kc_ref, kn_ref, vp_ref, vc_ref, vn_ref, mqz_ref, mk_ref,
                   mvt_ref, g2_ref, gm_ref, y2_ref, ym_ref, *, seq):
    tq = qz_ref.shape[1]
    t0 = pl.program_id(1) * tq
    kband = jnp.concatenate([kp_ref[...], kc_ref[...], kn_ref[...]], axis=0)
    vband = jnp.concatenate([vp_ref[0], vc_ref[0], vn_ref[0]], axis=1)
    nk = tq + 2 * WINDOW
    j = lax.broadcasted_iota(jnp.int32, (nk, tq), 0)
    i = lax.broadcasted_iota(jnp.int32, (nk, tq), 1)
    kpos = j + (t0 - WINDOW)
    mask = (j >= i) & (j <= i + 2 * WINDOW) & (kpos >= 0) & (kpos < seq)
    outs = []
    for g in range(N_Q_HEADS):
        kv = g // GQA_GROUP
        s = jnp.where(mask, _dot_nt(kband, qz_ref[g]), NEG)
        sink = sink_ref[g] * LOG2E
        m = jnp.maximum(jnp.max(s, axis=0, keepdims=True), sink)
        p = jnp.exp2(s - m)
        l = jnp.sum(p, axis=0, keepdims=True) + jnp.exp2(sink - m)
        o = _dot(vband[kv * HEAD:(kv + 1) * HEAD, :], p.astype(BF16))
        outs.append(o * (1.0 / l))
    y2 = jnp.concatenate(outs, axis=0).T
    y2_ref[...] = (y2 * g2_ref[...]).astype(BF16)
    ym = _mem_attention_t(mqz_ref, mk_ref, mvt_ref).T
    ym_ref[...] = (ym * gm_ref[...]).astype(BF16)


def _window_stream_kernel(sink_ref, qz_ref, kp_ref, kc_ref, kn_ref, vp_ref, vc_ref, vn_ref, mqz_ref,
                          mk_ref, mvt_ref, g2_ref, gm_ref, y2_ref, ym_ref, *, seq):
    tq = qz_ref.shape[1]
    t0 = pl.program_id(1) * tq
    kband = jnp.concatenate([kp_ref[...], kc_ref[...], kn_ref[...]], axis=0)
    vband = jnp.concatenate([vp_ref[0], vc_ref[0], vn_ref[0]], axis=1)
    nk = 3 * WINDOW
    j = lax.broadcasted_iota(jnp.int32, (nk, WINDOW), 0)
    i = lax.broadcasted_iota(jnp.int32, (nk, WINDOW), 1)
    in_band = (j >= i) & (j <= i + 2 * WINDOW)
    sink_p = [jnp.exp2(jnp.full((1, WINDOW), sink_ref[g] * LOG2E, F32)) for g in range(N_Q_HEADS)]
    scores = []
    for a in range(tq // WINDOW):
        for kv in range(N_Q_HEADS // GQA_GROUP):
            q4 = jnp.concatenate([qz_ref[kv * GQA_GROUP + r, a * WINDOW:(a + 1) * WINDOW, :]
                                  for r in range(GQA_GROUP)], axis=0)
            scores.append(_dot_nt(kband[a * WINDOW:a * WINDOW + nk], q4))
    blocks = [[None] * (tq // WINDOW) for _ in range(N_Q_HEADS)]
    for a in range(tq // WINDOW):
        kpos = j + (t0 + (a - 1) * WINDOW)
        bias1 = jnp.where(in_band & (kpos >= 0) & (kpos < seq), 0.0, NEG)
        bias = jnp.concatenate([bias1] * GQA_GROUP, axis=1)
        for kv in range(N_Q_HEADS // GQA_GROUP):
            p = jnp.exp2(scores[a * (N_Q_HEADS // GQA_GROUP) + kv] + bias)
            sinks = jnp.concatenate(sink_p[kv * GQA_GROUP:(kv + 1) * GQA_GROUP], axis=1)
            l = jnp.sum(p, axis=0, keepdims=True) + sinks
            o = _dot(vband[kv * HEAD:(kv + 1) * HEAD, a * WINDOW:a * WINDOW + nk], p.astype(BF16))
            o = o * (1.0 / l)
            for r in range(GQA_GROUP):
                blocks[kv * GQA_GROUP + r][a] = o[:, r * WINDOW:(r + 1) * WINDOW]
    y2 = jnp.concatenate([jnp.concatenate(b, axis=1) for b in blocks], axis=0).T
    y2_ref[...] = (y2 * g2_ref[...]).astype(BF16)
    ym = _mem_attention_t(mqz_ref, mk_ref, mvt_ref, bounded=True).T
    ym_ref[...] = (ym * gm_ref[...]).astype(BF16)


def _window_attention(sink, qz, k, vt, mqz, mk, mvt, gate, batch, seq, bounded):
    t = k.shape[0]
    tq = Q_TILE
    nq = seq // tq
    w_per_q = tq // WINDOW
    w_per_seq = seq // WINDOW
    q_per_chunk = ROW_TILE // tq
    w_per_chunk = ROW_TILE // WINDOW
    n_mem = mk.shape[1]

    def prev_w(b, q):
        return b * w_per_seq + jnp.maximum(q * w_per_q - 1, 0)

    def next_w(b, q):
        return b * w_per_seq + jnp.minimum((q + 1) * w_per_q, w_per_seq - 1)

    tok = lambda b, q: b * nq + q
    return pl.pallas_call(
        functools.partial(_window_stream_kernel if bounded else _window_kernel, seq=seq),
        grid=(batch, nq),
        in_specs=[
            pl.BlockSpec(memory_space=pltpu.SMEM),
            pl.BlockSpec((N_Q_HEADS, tq, LANES), lambda b, q: (0, tok(b, q), 0)),
            pl.BlockSpec((WINDOW, LANES), lambda b, q: (prev_w(b, q), 0)),
            pl.BlockSpec((tq, LANES), lambda b, q: (tok(b, q), 0)),
            pl.BlockSpec((WINDOW, LANES), lambda b, q: (next_w(b, q), 0)),
            pl.BlockSpec((1, LANES, WINDOW), lambda b, q: (prev_w(b, q) // w_per_chunk, 0, prev_w(b, q) % w_per_chunk)),
            pl.BlockSpec((1, LANES, tq), lambda b, q: (tok(b, q) // q_per_chunk, 0, tok(b, q) % q_per_chunk)),
            pl.BlockSpec((1, LANES, WINDOW), lambda b, q: (next_w(b, q) // w_per_chunk, 0, next_w(b, q) % w_per_chunk)),
            pl.BlockSpec((MEM_HEADS, tq, LANES), lambda b, q: (0, tok(b, q), 0)),
            pl.BlockSpec((1, n_mem, MEM_HEADS * HEAD), lambda b, q: (b, 0, 0)),
            pl.BlockSpec((1, MEM_HEADS * HEAD, n_mem), lambda b, q: (b, 0, 0)),
            pl.BlockSpec((tq, 512), lambda b, q: (tok(b, q), 0)),
            pl.BlockSpec((tq, 256), lambda b, q: (tok(b, q), 2)),
        ],
        out_specs=[
            pl.BlockSpec((tq, 512), lambda b, q: (tok(b, q), 0)),
            pl.BlockSpec((tq, 256), lambda b, q: (tok(b, q), 0)),
        ],
        out_shape=[
            jax.ShapeDtypeStruct((t, 512), BF16),
            jax.ShapeDtypeStruct((t, 256), BF16),
        ],
        compiler_params=_params(2),
        name="window_attention" if bounded else "window_attention_max",
    )(sink, qz, k, k, k, vt, vt, vt, mqz, mk, mvt, gate, gate)


def _proj_odd_kernel(x_ref, g_ref, w_ref, cos1_ref, sin1_ref, cosa_ref, sina_ref, gq_ref, gk_ref,
                     gdq_ref, gdk_ref, gmq_ref, bd_ref,
                     qz_ref, k_ref, vt_ref, dqz_ref, dk_ref, dvt_ref, mqz_ref, gate_ref):
    bd = bd_ref[...]
    cos1, sin1 = cos1_ref[...], sin1_ref[...]
    cosa, sina = cosa_ref[...], sina_ref[...]
    h = _rms_rows(x_ref[...], g_ref[...]).astype(BF16)
    q, kv, dq, dk, dv, mq, z = _project(
        h, w_ref, [(0, 512), (512, 768), (768, 1280), (1280, 1792), (1792, 2304), (MQ_OFF, GATE_OFF),
                   (GATE_OFF, IN_W)])
    gate_ref[...] = _silu(z)

    q = _rope(_head_rms(q, gq_ref[...], bd), cosa, sina, HEAD // 4) * Q_SCALE
    _place_heads(q, lambda h: h // GQA_GROUP, qz_ref)
    k = _rope(_head_rms(kv[:, 0:LANES], gk_ref[...], bd), cosa, sina, HEAD // 4)
    k_ref[...] = k.astype(BF16)
    vt_ref[0] = kv[:, LANES:2 * LANES].T.astype(BF16)

    dq = _rope(_head_rms(dq, gdq_ref[...], bd), cos1, sin1, HEAD // 2) * Q_SCALE
    _place_heads(dq, lambda h: h % 2, dqz_ref)
    dk = _rope(_head_rms(dk, gdk_ref[...], bd), cos1, sin1, HEAD // 2)
    dk_ref[...] = dk.astype(BF16)
    dvt_ref[0] = dv.T.astype(BF16)

    mq = _head_rms(mq, gmq_ref[...], bd) * Q_SCALE
    _place_heads(mq, lambda h: h % 2, mqz_ref)


def _proj_odd(x2, seq, g, w, cos1, sin1, cosa, sina, gq, gk, gdq, gdk, gmq, bd):
    t = x2.shape[0]
    n_tiles = t // ROW_TILE
    tiles_per_seq = seq // ROW_TILE
    row = lambda w: pl.BlockSpec((1, w), lambda i: (0, 0))
    tab = pl.BlockSpec((ROW_TILE, LANES), lambda i: (i % tiles_per_seq, 0))
    return pl.pallas_call(
        _proj_odd_kernel,
        grid=(n_tiles,),
        in_specs=[
            pl.BlockSpec((ROW_TILE, D_MODEL), lambda i: (i, 0)),
            row(D_MODEL),
            pl.BlockSpec((D_MODEL, IN_W), lambda i: (0, 0)),
            tab, tab, tab, tab, row(512), row(LANES), row(512), row(512), row(256),
            pl.BlockSpec((2 * LANES, 2 * LANES), lambda i: (0, 0)),
        ],
        out_specs=[
            pl.BlockSpec((N_Q_HEADS, ROW_TILE, LANES), lambda i: (0, i, 0)),
            pl.BlockSpec((ROW_TILE, LANES), lambda i: (i, 0)),
            pl.BlockSpec((1, LANES, ROW_TILE), lambda i: (i, 0, 0)),
            pl.BlockSpec((N_Q_HEADS, ROW_TILE, LANES), lambda i: (0, i, 0)),
            pl.BlockSpec((ROW_TILE, 512), lambda i: (i, 0)),
            pl.BlockSpec((1, 512, ROW_TILE), lambda i: (i, 0, 0)),
            pl.BlockSpec((MEM_HEADS, ROW_TILE, LANES), lambda i: (0, i, 0)),
            pl.BlockSpec((ROW_TILE, MIX_W), lambda i: (i, 0)),
        ],
        out_shape=[
            jax.ShapeDtypeStruct((N_Q_HEADS, t, LANES), BF16),
            jax.ShapeDtypeStruct((t, LANES), BF16),
            jax.ShapeDtypeStruct((n_tiles, LANES, ROW_TILE), BF16),
            jax.ShapeDtypeStruct((N_Q_HEADS, t, LANES), BF16),
            jax.ShapeDtypeStruct((t, 512), BF16),
            jax.ShapeDtypeStruct((n_tiles, 512, ROW_TILE), BF16),
            jax.ShapeDtypeStruct((MEM_HEADS, t, LANES), BF16),
            jax.ShapeDtypeStruct((t, MIX_W), F32),
        ],
        compiler_params=_params(1),
        name="proj_odd",
    )(x2, g, w, cos1, sin1, cosa, sina, gq, gk, gdq, gdk, gmq, bd)


def _flash_t(qz_ref, k_ref, vt_ref, v_rows, m_sc, l_sc, acc_sc):
    n_maps = qz_ref.shape[0]
    n_chunks, _, chunk = vt_ref.shape
    m_sc[...] = jnp.full(m_sc.shape, NEG, F32)
    l_sc[...] = jnp.zeros(l_sc.shape, F32)
    acc_sc[...] = jnp.zeros(acc_sc.shape, F32)

    def body(c, carry):
        kc = k_ref[pl.ds(pl.multiple_of(c * chunk, chunk), chunk), :]
        vc = vt_ref[c]
        for g in range(n_maps):
            r0, nr = v_rows(g)
            s = _dot_nt(kc, qz_ref[g])
            m_old = m_sc[g]
            m_new = jnp.maximum(m_old, jnp.max(s, axis=0, keepdims=True))
            alpha = jnp.exp2(m_old - m_new)
            p = jnp.exp2(s - m_new)
            l_sc[g] = alpha * l_sc[g] + jnp.sum(p, axis=0, keepdims=True)
            acc_sc[g] = alpha * acc_sc[g] + _dot(vc[r0:r0 + nr, :], p.astype(BF16))
            m_sc[g] = m_new
        return carry

    lax.fori_loop(0, n_chunks, body, 0)


def _stream_t(qz_ref, k_ref, vt_ref, v_rows, acc_sc):
    n_maps, tq, _ = qz_ref.shape
    n_tiles, _, tile = vt_ref.shape
    per_tile = tile // KV_CHUNK
    unroll = math.gcd(KV_UNROLL, n_tiles * per_tile)
    assert unroll % per_tile == 0
    outs = []
    for g in range(n_maps):
        r0, nr = v_rows(g)
        qg = qz_ref[g]
        acc_sc[g] = jnp.zeros((nr, tq), F32)

        def body(i, l8, g=g, r0=r0, nr=nr, qg=qg):
            where = [(i * (unroll // per_tile) + u // per_tile, (u % per_tile) * KV_CHUNK)
                     for u in range(unroll)]
            scores = []
            for t, off in where:
                kc = k_ref[pl.ds(pl.multiple_of(t * tile + off, KV_CHUNK), KV_CHUNK), :]
                scores.append(_dot_nt(kc, qg))
            pv = None
            for (t, off), s in zip(where, scores):
                p = jnp.exp2(s)
                l8 = l8 + jnp.sum(p.reshape(KV_CHUNK // SUBLANES, SUBLANES, tq), axis=0)
                d = _dot(vt_ref[t, r0:r0 + nr, off:off + KV_CHUNK], p.astype(BF16))
                pv = d if pv is None else pv + d
            acc_sc[g] += pv
            return l8

        l8 = lax.fori_loop(0, n_tiles * per_tile // unroll, body, jnp.zeros((SUBLANES, tq), F32))
        outs.append(acc_sc[g] * (1.0 / jnp.sum(l8, axis=0, keepdims=True)))
    return outs


def _axial_kernel(qz_ref, k_ref, vt_ref, mqz_ref, mk_ref, mvt_ref, g1_ref, gm_ref, y1_ref, ym_ref,
                  *scratch, bounded):
    v_rows = lambda g: ((g // GQA_GROUP) * HEAD, HEAD)
    if bounded:
        outs = _stream_t(qz_ref, k_ref, vt_ref, v_rows, *scratch)
    else:
        m_sc, l_sc, acc_sc = scratch
        _flash_t(qz_ref, k_ref, vt_ref, v_rows, m_sc, l_sc, acc_sc)
        outs = [acc_sc[g] * (1.0 / l_sc[g]) for g in range(N_Q_HEADS)]
    y1 = jnp.concatenate(outs, axis=0).T
    y1_ref[...] = (y1 * g1_ref[...]).astype(BF16)
    ym = _mem_attention_t(mqz_ref, mk_ref, mvt_ref, bounded=bounded).T
    ym_ref[...] = (ym * gm_ref[...]).astype(BF16)


def _axial_attention(qz, k, vt, mqz, mk, mvt, gate, batch, seq, bounded):
    t = k.shape[0]
    tq = STREAM_Q_TILE if bounded else Q_TILE
    nq = seq // tq
    n_chunks = seq // ROW_TILE
    n_mem = mk.shape[1]
    tok = lambda b, q: b * nq + q
    scratch = [pltpu.VMEM((N_Q_HEADS, HEAD, tq), F32)] if bounded else [
        pltpu.VMEM((N_Q_HEADS, 1, tq), F32),
        pltpu.VMEM((N_Q_HEADS, 1, tq), F32),
        pltpu.VMEM((N_Q_HEADS, HEAD, tq), F32),
    ]
    return pl.pallas_call(
        functools.partial(_axial_kernel, bounded=bounded),
        grid=(batch, nq),
        in_specs=[
            pl.BlockSpec((N_Q_HEADS, tq, LANES), lambda b, q: (0, tok(b, q), 0)),
            pl.BlockSpec((seq, LANES), lambda b, q: (b, 0)),
            pl.BlockSpec((n_chunks, LANES, ROW_TILE), lambda b, q: (b, 0, 0)),
            pl.BlockSpec((MEM_HEADS, tq, LANES), lambda b, q: (0, tok(b, q), 0)),
            pl.BlockSpec((1, n_mem, MEM_HEADS * HEAD), lambda b, q: (b, 0, 0)),
            pl.BlockSpec((1, MEM_HEADS * HEAD, n_mem), lambda b, q: (b, 0, 0)),
            pl.BlockSpec((tq, 512), lambda b, q: (tok(b, q), 0)),
            pl.BlockSpec((tq, 256), lambda b, q: (tok(b, q), 4)),
        ],
        out_specs=[
            pl.BlockSpec((tq, 512), lambda b, q: (tok(b, q), 0)),
            pl.BlockSpec((tq, 256), lambda b, q: (tok(b, q), 0)),
        ],
        out_shape=[
            jax.ShapeDtypeStruct((t, 512), BF16),
            jax.ShapeDtypeStruct((t, 256), BF16),
        ],
        scratch_shapes=scratch,
        compiler_params=_params(2),
        name="axial_attention" if bounded else "axial_attention_online",
    )(qz, k, vt, mqz, mk, mvt, gate, gate)


def _diff_kernel(qz_ref, k_ref, vt_ref, lam_ref, sg_ref, g2_ref, y2_ref, *scratch, lambda_init, bounded):
    v_rows = lambda g: (0, 2 * HEAD)
    if bounded:
        o0, o1 = _stream_t(qz_ref, k_ref, vt_ref, v_rows, *scratch)
    else:
        m_sc, l_sc, acc_sc = scratch
        _flash_t(qz_ref, k_ref, vt_ref, v_rows, m_sc, l_sc, acc_sc)
        o0, o1 = (acc_sc[g] * (1.0 / l_sc[g]) for g in range(2))
    lv = lam_ref[...]
    lam = (jnp.exp(jnp.sum(lv[0:1] * lv[1:2], axis=-1, keepdims=True))
           - jnp.exp(jnp.sum(lv[2:3] * lv[3:4], axis=-1, keepdims=True)) + lambda_init)
    o = o0 - lam * o1
    ms = jnp.mean(o * o, axis=0, keepdims=True)
    on = (o * lax.rsqrt(ms + EPS)).T
    y2_ref[...] = (on * sg_ref[...] * (1.0 - lambda_init) * g2_ref[...]).astype(BF16)


def _diff_attention(dqz, dk, dvt, lam, subln_g, gate, batch, seq, layer, bounded):
    t = dk.shape[0]
    tq = STREAM_Q_TILE if bounded else Q_TILE
    nq = seq // tq
    n_chunks = seq // ROW_TILE
    lambda_init = 0.8 - 0.6 * math.exp(-0.3 * layer)
    tok = lambda b, q: b * nq + q
    scratch = [pltpu.VMEM((2, 2 * HEAD, tq), F32)] if bounded else [
        pltpu.VMEM((2, 1, tq), F32),
        pltpu.VMEM((2, 1, tq), F32),
        pltpu.VMEM((2, 2 * HEAD, tq), F32),
    ]
    return pl.pallas_call(
        functools.partial(_diff_kernel, lambda_init=lambda_init, bounded=bounded),
        grid=(batch, DIFF_HEADS, nq),
        in_specs=[
            pl.BlockSpec((2, tq, LANES), lambda b, h, q: (h, tok(b, q), 0)),
            pl.BlockSpec((seq, LANES), lambda b, h, q: (b, h)),
            pl.BlockSpec((n_chunks, LANES, ROW_TILE), lambda b, h, q: (b, h, 0)),
            pl.BlockSpec((4, HEAD), lambda b, h, q: (0, 0)),
            pl.BlockSpec((1, 2 * HEAD), lambda b, h, q: (0, 0)),
            pl.BlockSpec((tq, LANES), lambda b, h, q: (tok(b, q), 4 + h)),
        ],
        out_specs=pl.BlockSpec((tq, LANES), lambda b, h, q: (tok(b, q), h)),
        out_shape=jax.ShapeDtypeStruct((t, 512), BF16),
        scratch_shapes=scratch,
        compiler_params=_params(3),
        name="diff_attention" if bounded else "diff_attention_online",
    )(dqz, dk, dvt, lam, subln_g.reshape(1, 2 * HEAD), gate)


def _out_proj_kernel(y1_ref, y2_ref, ym_ref, x_ref, w_ref, o_ref):
    acc = _dot(y1_ref[...], w_ref[0:512, :])
    acc = acc + _dot(y2_ref[...], w_ref[512:1024, :])
    acc = acc + _dot(ym_ref[...], w_ref[1024:MIX_W, :])
    o_ref[...] = x_ref[...] + acc


def _out_proj(y1, y2, ym, x2, w):
    t = x2.shape[0]
    return pl.pallas_call(
        _out_proj_kernel,
        grid=(t // ROW_TILE,),
        in_specs=[
            pl.BlockSpec((ROW_TILE, 512), lambda i: (i, 0)),
            pl.BlockSpec((ROW_TILE, 512), lambda i: (i, 0)),
            pl.BlockSpec((ROW_TILE, 256), lambda i: (i, 0)),
            pl.BlockSpec((ROW_TILE, D_MODEL), lambda i: (i, 0)),
            pl.BlockSpec((MIX_W, D_MODEL), lambda i: (0, 0)),
        ],
        out_specs=pl.BlockSpec((ROW_TILE, D_MODEL), lambda i: (i, 0)),
        out_shape=jax.ShapeDtypeStruct((t, D_MODEL), F32),
        compiler_params=_params(1),
        name="out_proj",
    )(y1, y2, ym, x2, w)


def _rope_angles(pos, dim):
    inv = ROPE_THETA ** (-jnp.arange(0, dim, 2, dtype=F32) / dim)
    return pos.astype(F32)[:, None] * inv[None, :]


def _rope_tables(seq):
    pos = jnp.arange(seq)
    a1 = _rope_angles(pos, HEAD)
    cos1 = jnp.concatenate([jnp.cos(a1), jnp.cos(a1)], axis=-1)
    sin1 = jnp.concatenate([-jnp.sin(a1), jnp.sin(a1)], axis=-1)
    ar = _rope_angles(pos // GRID_W, HEAD // 2)
    ac = _rope_angles(pos % GRID_W, HEAD // 2)
    cosa = jnp.concatenate([jnp.cos(ar), jnp.cos(ar), jnp.cos(ac), jnp.cos(ac)], axis=-1)
    sina = jnp.concatenate([-jnp.sin(ar), jnp.sin(ar), -jnp.sin(ac), jnp.sin(ac)], axis=-1)
    rep = lambda a: jnp.tile(a, (1, LANES // HEAD))
    return rep(cos1), rep(sin1), rep(cosa), rep(sina)


def _scores_bounded(qk_g):
    bound = HEAD * Q_SCALE * 1.01 * jnp.max(jnp.abs(qk_g[0])) * jnp.max(jnp.abs(qk_g[1]))
    return bound <= SAFE_SCORE_BOUND


def _tile_gain(g, width):
    return jnp.tile(g.astype(F32), width // HEAD).reshape(1, width)


def _trunk(x, mem, p):
    batch, seq, _ = x.shape
    x2 = x.reshape(batch * seq, D_MODEL)
    cos1, sin1, cosa, sina = _rope_tables(seq)
    bd = p["bd"]
    gk_mem = jnp.stack([_tile_gain(p["mem_qk_g"][l, 1], 256) for l in range(DEPTH)])
    mk_all, mvt_all = _memkv(mem, p["mem_norm_g"], p["w_mem_kv"], gk_mem, bd)
    for l in range(DEPTH):
        g_in, w_in = p["norm_g"][l].reshape(1, D_MODEL), p["w_in"][l]
        gmq = _tile_gain(p["mem_qk_g"][l, 0], 256)
        mk, mvt = mk_all[l], mvt_all[l]
        if l % 2 == 0:
            e = l // 2
            y1, qz, k, vt, mqz, gate = _proj_even(
                x2, seq, g_in, w_in, cos1, sin1, _tile_gain(p["swa_qk_g"][e, 0], 512),
                _tile_gain(p["swa_qk_g"][e, 1], LANES), gmq, p["conv_w"][e], bd)
            sink_ok = jnp.max(jnp.abs(p["swa_sink"][e])) * LOG2E <= SAFE_SCORE_BOUND
            y2, ym = lax.cond(
                _scores_bounded(p["swa_qk_g"][e]) & _scores_bounded(p["mem_qk_g"][l]) & sink_ok,
                functools.partial(_window_attention, batch=batch, seq=seq, bounded=True),
                functools.partial(_window_attention, batch=batch, seq=seq, bounded=False),
                p["swa_sink"][e], qz, k, vt, mqz, mk, mvt, gate)
        else:
            o = l // 2
            qz, k, vt, dqz, dk, dvt, mqz, gate = _proj_odd(
                x2, seq, g_in, w_in, cos1, sin1, cosa, sina, _tile_gain(p["ax_qk_g"][o, 0], 512),
                _tile_gain(p["ax_qk_g"][o, 1], LANES), _tile_gain(p["diff_qk_g"][o, 0], 512),
                _tile_gain(p["diff_qk_g"][o, 1], 512), gmq, bd)
            y1, ym = lax.cond(
                _scores_bounded(p["ax_qk_g"][o]) & _scores_bounded(p["mem_qk_g"][l]),
                functools.partial(_axial_attention, batch=batch, seq=seq, bounded=True),
                functools.partial(_axial_attention, batch=batch, seq=seq, bounded=False),
                qz, k, vt, mqz, mk, mvt, gate)
            y2 = lax.cond(
                _scores_bounded(p["diff_qk_g"][o]),
                functools.partial(_diff_attention, batch=batch, seq=seq, layer=l, bounded=True),
                functools.partial(_diff_attention, batch=batch, seq=seq, layer=l, bounded=False),
                dqz, dk, dvt, p["diff_lambda"][o], p["diff_subln_g"][o], gate)
        x2 = _out_proj(y1, y2, ym, x2, p["w_out"][l])
    return x2.reshape(batch, seq, D_MODEL)


def kernel(x_prompt, x_sample, mem_prompt, mem_sample, norm_g, w_in, w_out, mem_norm_g, w_mem_kv,
           mem_qk_g, conv_w, swa_qk_g, swa_sink, ax_qk_g, diff_qk_g, diff_lambda, diff_subln_g):
    grp = jnp.arange(2 * LANES) // HEAD
    p = dict(
        norm_g=norm_g, w_in=w_in.astype(BF16), w_out=w_out.astype(BF16), mem_norm_g=mem_norm_g,
        w_mem_kv=w_mem_kv.astype(BF16), mem_qk_g=mem_qk_g, conv_w=conv_w, swa_qk_g=swa_qk_g,
        swa_sink=swa_sink, ax_qk_g=ax_qk_g, diff_qk_g=diff_qk_g, diff_lambda=diff_lambda,
        diff_subln_g=diff_subln_g, bd=(grp[:, None] == grp[None, :]).astype(BF16))
    return (_trunk(x_prompt, mem_prompt, p), _trunk(x_sample, mem_sample, p))
```

```python
import functools
import math

import jax
import jax.numpy as jnp
from jax import lax
from jax.experimental import pallas as pl
from jax.experimental.pallas import tpu as pltpu

F32 = jnp.float32
BF16 = jnp.bfloat16

D_MODEL = 1024
DEPTH = 4
HEAD = 64
LANES = 128
SUBLANES = 8
N_Q_HEADS = 8
GQA_GROUP = 4
DIFF_HEADS = 4
MEM_HEADS = 4
CONV_W = 512
WINDOW = 128
GRID_W = 64
ROPE_THETA = 10000.0
EPS = 1e-6
NEG = -1e30
LOG2E = 1.4426950408889634
Q_SCALE = HEAD ** -0.5 * LOG2E
SAFE_SCORE_BOUND = 60.0

IN_W = 3840
MIX_W = 1280
GATE_OFF = 2560
MQ_OFF = 2304

ROW_TILE = 512
Q_TILE = 256
STREAM_Q_TILE = 512
KV_CHUNK = 256
KV_SKEW = 3
VMEM_LIMIT = 56 * 1024 * 1024

_NT = (((1,), (1,)), ((), ()))


def _params(n_axes):
    return pltpu.CompilerParams(dimension_semantics=("arbitrary",) * n_axes, vmem_limit_bytes=VMEM_LIMIT)


def _dot(a, b):
    return jnp.dot(a, b, preferred_element_type=F32)


def _dot_nt(a, b):
    return lax.dot_general(a, b, _NT, preferred_element_type=F32)


def _rms_rows(x, g):
    ms = jnp.mean(x * x, axis=-1, keepdims=True)
    return x * lax.rsqrt(ms + EPS) * g


def _head_sumsq(x, bd):
    w = x.shape[1]
    outs = []
    step = 2 * LANES if w % (2 * LANES) == 0 else LANES
    for c in range(0, w, step):
        sq = x[:, c:c + step]
        sq = sq * sq
        hi = sq.astype(BF16)
        lo = (sq - hi.astype(F32)).astype(BF16)
        b = bd[:step, :step]
        outs.append(_dot(hi, b) + _dot(lo, b))
    return outs[0] if len(outs) == 1 else jnp.concatenate(outs, axis=1)


def _head_rms(x, g, bd):
    return x * lax.rsqrt(_head_sumsq(x, bd) * (1.0 / HEAD) + EPS) * g


def _rope(x, cos, sin_signed, half):
    r = x.shape[0]
    lane = lax.broadcasted_iota(jnp.int32, (r, LANES), 1)
    first = (lane & (2 * half - 1)) < half
    outs = []
    for c in range(0, x.shape[1], LANES):
        xc = x[:, c:c + LANES]
        partner = jnp.where(first, pltpu.roll(xc, LANES - half, 1), pltpu.roll(xc, half, 1))
        outs.append(xc * cos + partner * sin_signed)
    return outs[0] if len(outs) == 1 else jnp.concatenate(outs, axis=1)


def _place_heads(x, target_half, out_ref):
    r = x.shape[0]
    low = lax.broadcasted_iota(jnp.int32, (r, LANES), 1) < HEAD
    for h in range(x.shape[1] // HEAD):
        xc = x[:, (h // 2) * LANES:(h // 2 + 1) * LANES]
        th = target_half(h)
        if th != h % 2:
            xc = pltpu.roll(xc, HEAD, 1)
        keep = low if th == 0 else jnp.logical_not(low)
        out_ref[h] = jnp.where(keep, xc, 0.0).astype(BF16)


def _silu(z):
    return z * (1.0 / (1.0 + jnp.exp(-z)))


def _project(h, w_ref, splits):
    return [_dot(h, w_ref[:, a:b]) for a, b in splits]


def _memkv_kernel(mem_ref, g_ref, w_ref, gk_ref, bd_ref, mk_ref, mvt_ref):
    h = _rms_rows(mem_ref[0], g_ref[0]).astype(BF16)
    mkv = _dot(h, w_ref[0])
    half = mkv.shape[1] // 2
    mk = _head_rms(mkv[:, :half], gk_ref[0], bd_ref[...])
    mk_ref[0, 0] = mk.astype(BF16)
    mvt_ref[0, 0] = mkv[:, half:].T.astype(BF16)


def _memkv(mem, mem_norm_g, w_mem_kv, gk_tiled, bd):
    b, n_mem, _ = mem.shape
    width = MEM_HEADS * HEAD
    return pl.pallas_call(
        _memkv_kernel,
        grid=(DEPTH, b),
        in_specs=[
            pl.BlockSpec((1, n_mem, D_MODEL), lambda l, i: (i, 0, 0)),
            pl.BlockSpec((1, 1, D_MODEL), lambda l, i: (l, 0, 0)),
            pl.BlockSpec((1, D_MODEL, 2 * width), lambda l, i: (l, 0, 0)),
            pl.BlockSpec((1, 1, width), lambda l, i: (l, 0, 0)),
            pl.BlockSpec((2 * LANES, 2 * LANES), lambda l, i: (0, 0)),
        ],
        out_specs=[
            pl.BlockSpec((1, 1, n_mem, width), lambda l, i: (l, i, 0, 0)),
            pl.BlockSpec((1, 1, width, n_mem), lambda l, i: (l, i, 0, 0)),
        ],
        out_shape=[
            jax.ShapeDtypeStruct((DEPTH, b, n_mem, width), BF16),
            jax.ShapeDtypeStruct((DEPTH, b, width, n_mem), BF16),
        ],
        compiler_params=_params(2),
        name="mem_kv",
    )(mem, mem_norm_g.reshape(DEPTH, 1, D_MODEL), w_mem_kv, gk_tiled, bd)


def _proj_even_kernel(x_ref, xp_ref, xn_ref, g_ref, w_ref, cos_ref, sin_ref, gq_ref, gk_ref,
                      gmq_ref, cw_ref, bd_ref,
                      y1_ref, qz_ref, k_ref, vt_ref, mqz_ref, gate_ref, conv_sc, *, tiles_per_seq):
    i = pl.program_id(0)
    pos_tile = i % tiles_per_seq
    bd = bd_ref[...]
    cos = cos_ref[...]
    sin = sin_ref[...]
    rows = x_ref.shape[0]

    x_ext = jnp.concatenate([x_ref[...], xp_ref[...], xn_ref[...]], axis=0)
    h_ext = _rms_rows(x_ext, g_ref[...]).astype(BF16)
    h = h_ext[:rows]
    gb, qkv, mq, z = _project(h, w_ref, [(0, 512), (1536, 2304), (MQ_OFF, GATE_OFF), (GATE_OFF, IN_W)])
    gchc = _dot(h_ext, w_ref[:, 512:1536])

    inner_ext = gchc[:, 0:512] * gchc[:, 512:1024]
    inner = inner_ext[:rows]
    prev_row = inner_ext[rows + SUBLANES - 1:rows + SUBLANES]
    next_row = inner_ext[rows + SUBLANES:rows + SUBLANES + 1]
    prev_row = jnp.where(pos_tile == 0, 0.0, prev_row)
    next_row = jnp.where(pos_tile == tiles_per_seq - 1, 0.0, next_row)
    conv_sc[pl.ds(SUBLANES, rows), :] = inner
    conv_sc[pl.ds(SUBLANES - 1, 1), :] = prev_row
    conv_sc[pl.ds(SUBLANES + rows, 1), :] = next_row
    cw = cw_ref[...]
    conv = (conv_sc[pl.ds(SUBLANES - 1, rows), :] * cw[0:1, :] + inner * cw[1:2, :]
            + conv_sc[pl.ds(SUBLANES + 1, rows), :] * cw[2:3, :])
    y1_ref[...] = (gb * conv * _silu(z[:, 0:512])).astype(BF16)
    gate_ref[...] = _silu(z[:, 512:MIX_W])

    q = _rope(_head_rms(qkv[:, 0:512], gq_ref[...], bd), cos, sin, HEAD // 2) * Q_SCALE
    _place_heads(q, lambda h: h // GQA_GROUP, qz_ref)
    k = _rope(_head_rms(qkv[:, 512:640], gk_ref[...], bd), cos, sin, HEAD // 2)
    k_ref[...] = k.astype(BF16)
    vt_ref[0] = qkv[:, 640:768].T.astype(BF16)

    mq = _head_rms(mq, gmq_ref[...], bd) * Q_SCALE
    _place_heads(mq, lambda h: h % 2, mqz_ref)


def _proj_even(x2, seq, g, w, cos, sin, gq, gk, gmq, conv_w, bd):
    t = x2.shape[0]
    n_tiles = t // ROW_TILE
    tiles_per_seq = seq // ROW_TILE
    sub_per_tile = ROW_TILE // SUBLANES
    last_sub = t // SUBLANES - 1
    row = lambda w: pl.BlockSpec((1, w), lambda i: (0, 0))
    tab = pl.BlockSpec((ROW_TILE, LANES), lambda i: (i % tiles_per_seq, 0))
    return pl.pallas_call(
        functools.partial(_proj_even_kernel, tiles_per_seq=tiles_per_seq),
        grid=(n_tiles,),
        in_specs=[
            pl.BlockSpec((ROW_TILE, D_MODEL), lambda i: (i, 0)),
            pl.BlockSpec((SUBLANES, D_MODEL), lambda i: (jnp.maximum(i * sub_per_tile - 1, 0), 0)),
            pl.BlockSpec((SUBLANES, D_MODEL), lambda i: (jnp.minimum((i + 1) * sub_per_tile, last_sub), 0)),
            row(D_MODEL),
            pl.BlockSpec((D_MODEL, IN_W), lambda i: (0, 0)),
            tab, tab, row(512), row(LANES), row(256),
            pl.BlockSpec((3, CONV_W), lambda i: (0, 0)),
            pl.BlockSpec((2 * LANES, 2 * LANES), lambda i: (0, 0)),
        ],
        out_specs=[
            pl.BlockSpec((ROW_TILE, CONV_W), lambda i: (i, 0)),
            pl.BlockSpec((N_Q_HEADS, ROW_TILE, LANES), lambda i: (0, i, 0)),
            pl.BlockSpec((ROW_TILE, LANES), lambda i: (i, 0)),
            pl.BlockSpec((1, LANES, ROW_TILE), lambda i: (i, 0, 0)),
            pl.BlockSpec((MEM_HEADS, ROW_TILE, LANES), lambda i: (0, i, 0)),
            pl.BlockSpec((ROW_TILE, 768), lambda i: (i, 0)),
        ],
        out_shape=[
            jax.ShapeDtypeStruct((t, CONV_W), BF16),
            jax.ShapeDtypeStruct((N_Q_HEADS, t, LANES), BF16),
            jax.ShapeDtypeStruct((t, LANES), BF16),
            jax.ShapeDtypeStruct((n_tiles, LANES, ROW_TILE), BF16),
            jax.ShapeDtypeStruct((MEM_HEADS, t, LANES), BF16),
            jax.ShapeDtypeStruct((t, 768), F32),
        ],
        scratch_shapes=[pltpu.VMEM((ROW_TILE + 2 * SUBLANES, CONV_W), F32)],
        compiler_params=_params(1),
        name="proj_even",
    )(x2, x2, x2, g, w, cos, sin, gq, gk, gmq, conv_w, bd)


def _mem_attention_t(mqz_ref, mk_ref, mvt_ref, bounded=False):
    outs = []
    if bounded:
        tq = mqz_ref.shape[1]
        pairs = []
        for grp in range(MEM_HEADS // 2):
            q2 = jnp.concatenate([mqz_ref[2 * grp], mqz_ref[2 * grp + 1]], axis=0)
            pairs.append(jnp.exp2(_dot_nt(mk_ref[0, :, grp * LANES:(grp + 1) * LANES], q2)))
        for h in range(MEM_HEADS):
            p = pairs[h // 2][:, (h % 2) * tq:(h % 2 + 1) * tq]
            l = jnp.sum(p, axis=0, keepdims=True)
            o = _dot(mvt_ref[0, h * HEAD:(h + 1) * HEAD, :], p.astype(BF16))
            outs.append(o * (1.0 / l))
        return jnp.concatenate(outs, axis=0)
    for h in range(MEM_HEADS):
        grp = h // 2
        s = _dot_nt(mk_ref[0, :, grp * LANES:(grp + 1) * LANES], mqz_ref[h])
        m = jnp.max(s, axis=0, keepdims=True)
        p = jnp.exp2(s - m)
        l = jnp.sum(p, axis=0, keepdims=True)
        o = _dot(mvt_ref[0, h * HEAD:(h + 1) * HEAD, :], p.astype(BF16))
        outs.append(o * (1.0 / l))
    return jnp.concatenate(outs, axis=0)


def _window_kernel(sink_ref, qz_ref, kp_ref, kc_ref, kn_ref, vp_ref, vc_ref, vn_ref, mqz_ref, mk_ref,
                   mvt_ref, g2_ref, gm_ref, y2_ref, ym_ref, *, seq):
    tq = qz_ref.shape[1]
    t0 = pl.program_id(1) * tq
    kband = jnp.concatenate([kp_ref[...], kc_ref[...], kn_ref[...]], axis=0)
    vband = jnp.concatenate([vp_ref[0], vc_ref[0], vn_ref[0]], axis=1)
    nk = tq + 2 * WINDOW
    j = lax.broadcasted_iota(jnp.int32, (nk, tq), 0)
    i = lax.broadcasted_iota(jnp.int32, (nk, tq), 1)
    kpos = j + (t0 - WINDOW)
    mask = (j >= i) & (j <= i + 2 * WINDOW) & (kpos >= 0) & (kpos < seq)
    outs = []
    for g in range(N_Q_HEADS):
        kv = g // GQA_GROUP
        s = jnp.where(mask, _dot_nt(kband, qz_ref[g]), NEG)
        sink = sink_ref[g] * LOG2E
        m = jnp.maximum(jnp.max(s, axis=0, keepdims=True), sink)
        p = jnp.exp2(s - m)
        l = jnp.sum(p, axis=0, keepdims=True) + jnp.exp2(sink - m)
        o = _dot(vband[kv * HEAD:(kv + 1) * HEAD, :], p.astype(BF16))
        outs.append(o * (1.0 / l))
    y2 = jnp.concatenate(outs, axis=0).T
    y2_ref[...] = (y2 * g2_ref[...]).astype(BF16)
    ym = _mem_attention_t(mqz_ref, mk_ref, mvt_ref).T
    ym_ref[...] = (ym * gm_ref[...]).astype(BF16)


def _window_stream_kernel(sink_ref, qz_ref, kp_ref, kc_ref, kn_ref, vp_ref, vc_ref, vn_ref, mqz_ref,
                          mk_ref, mvt_ref, g2_ref, gm_ref, y2_ref, ym_ref, *, seq):
    tq = qz_ref.shape[1]
    t0 = pl.program_id(1) * tq
    kband = jnp.concatenate([kp_ref[...], kc_ref[...], kn_ref[...]], axis=0)
    vband = jnp.concatenate([vp_ref[0], vc_ref[0], vn_ref[0]], axis=1)
    nk = 3 * WINDOW
    j = lax.broadcasted_iota(jnp.int32, (nk, WINDOW), 0)
    i = lax.broadcasted_iota(jnp.int32, (nk, WINDOW), 1)
    in_band = (j >= i) & (j <= i + 2 * WINDOW)
    sink_p = [jnp.exp2(jnp.full((1, WINDOW), sink_ref[g] * LOG2E, F32)) for g in range(N_Q_HEADS)]
    scores = []
    for a in range(tq // WINDOW):
        for kv in range(N_Q_HEADS // GQA_GROUP):
            q4 = jnp.concatenate([qz_ref[kv * GQA_GROUP + r, a * WINDOW:(a + 1) * WINDOW, :]
                                  for r in range(GQA_GROUP)], axis=0)
            scores.append(_dot_nt(kband[a * WINDOW:a * WINDOW + nk], q4))
    blocks = [[None] * (tq // WINDOW) for _ in range(N_Q_HEADS)]
    for a in range(tq // WINDOW):
        kpos = j + (t0 + (a - 1) * WINDOW)
        bias1 = jnp.where(in_band & (kpos >= 0) & (kpos < seq), 0.0, NEG)
        bias = jnp.concatenate([bias1] * GQA_GROUP, axis=1)
        for kv in range(N_Q_HEADS // GQA_GROUP):
            p = jnp.exp2(scores[a * (N_Q_HEADS // GQA_GROUP) + kv] + bias)
            sinks = jnp.concatenate(sink_p[kv * GQA_GROUP:(kv + 1) * GQA_GROUP], axis=1)
            l = jnp.sum(p, axis=0, keepdims=True) + sinks
            o = _dot(vband[kv * HEAD:(kv + 1) * HEAD, a * WINDOW:a * WINDOW + nk], p.astype(BF16))
            o = o * (1.0 / l)
            for r in range(GQA_GROUP):
                blocks[kv * GQA_GROUP + r][a] = o[:, r * WINDOW:(r + 1) * WINDOW]
    y2 = jnp.concatenate([jnp.concatenate(b, axis=1) for b in blocks], axis=0).T
    y2_ref[...] = (y2 * g2_ref[...]).astype(BF16)
    ym = _mem_attention_t(mqz_ref, mk_ref, mvt_ref, bounded=True).T
    ym_ref[...] = (ym * gm_ref[...]).astype(BF16)


def _window_attention(sink, qz, k, vt, mqz, mk, mvt, gate, batch, seq, bounded):
    t = k.shape[0]
    tq = Q_TILE
    nq = seq // tq
    w_per_q = tq // WINDOW
    w_per_seq = seq // WINDOW
    q_per_chunk = ROW_TILE // tq
    w_per_chunk = ROW_TILE // WINDOW
    n_mem = mk.shape[1]

    def prev_w(b, q):
        return b * w_per_seq + jnp.maximum(q * w_per_q - 1, 0)

    def next_w(b, q):
        return b * w_per_seq + jnp.minimum((q + 1) * w_per_q, w_per_seq - 1)

    tok = lambda b, q: b * nq + q
    return pl.pallas_call(
        functools.partial(_window_stream_kernel if bounded else _window_kernel, seq=seq),
        grid=(batch, nq),
        in_specs=[
            pl.BlockSpec(memory_space=pltpu.SMEM),
            pl.BlockSpec((N_Q_HEADS, tq, LANES), lambda b, q: (0, tok(b, q), 0)),
            pl.BlockSpec((WINDOW, LANES), lambda b, q: (prev_w(b, q), 0)),
            pl.BlockSpec((tq, LANES), lambda b, q: (tok(b, q), 0)),
            pl.BlockSpec((WINDOW, LANES), lambda b, q: (next_w(b, q), 0)),
            pl.BlockSpec((1, LANES, WINDOW), lambda b, q: (prev_w(b, q) // w_per_chunk, 0, prev_w(b, q) % w_per_chunk)),
            pl.BlockSpec((1, LANES, tq), lambda b, q: (tok(b, q) // q_per_chunk, 0, tok(b, q) % q_per_chunk)),
            pl.BlockSpec((1, LANES, WINDOW), lambda b, q: (next_w(b, q) // w_per_chunk, 0, next_w(b, q) % w_per_chunk)),
            pl.BlockSpec((MEM_HEADS, tq, LANES), lambda b, q: (0, tok(b, q), 0)),
            pl.BlockSpec((1, n_mem, MEM_HEADS * HEAD), lambda b, q: (b, 0, 0)),
            pl.BlockSpec((1, MEM_HEADS * HEAD, n_mem), lambda b, q: (b, 0, 0)),
            pl.BlockSpec((tq, 512), lambda b, q: (tok(b, q), 0)),
            pl.BlockSpec((tq, 256), lambda b, q: (tok(b, q), 2)),
        ],
        out_specs=[
            pl.BlockSpec((tq, 512), lambda b, q: (tok(b, q), 0)),
            pl.BlockSpec((tq, 256), lambda b, q: (tok(b, q), 0)),
        ],
        out_shape=[
            jax.ShapeDtypeStruct((t, 512), BF16),
            jax.ShapeDtypeStruct((t, 256), BF16),
        ],
        compiler_params=_params(2),
        name="window_attention" if bounded else "window_attention_max",
    )(sink, qz, k, k, k, vt, vt, vt, mqz, mk, mvt, gate, gate)


def _proj_odd_kernel(x_ref, g_ref, w_ref, cos1_ref, sin1_ref, cosa_ref, sina_ref, gq_ref, gk_ref,
                     gdq_ref, gdk_ref, gmq_ref, bd_ref,
                     qz_ref, k_ref, vt_ref, dqz_ref, dk_ref, dvt_ref, mqz_ref, gate_ref):
    bd = bd_ref[...]
    cos1, sin1 = cos1_ref[...], sin1_ref[...]
    cosa, sina = cosa_ref[...], sina_ref[...]
    h = _rms_rows(x_ref[...], g_ref[...]).astype(BF16)
    q, kv, dq, dk, dv, mq, z = _project(
        h, w_ref, [(0, 512), (512, 768), (768, 1280), (1280, 1792), (1792, 2304), (MQ_OFF, GATE_OFF),
                   (GATE_OFF, IN_W)])
    gate_ref[...] = _silu(z)

    q = _rope(_head_rms(q, gq_ref[...], bd), cosa, sina, HEAD // 4) * Q_SCALE
    _place_heads(q, lambda h: h // GQA_GROUP, qz_ref)
    k = _rope(_head_rms(kv[:, 0:LANES], gk_ref[...], bd), cosa, sina, HEAD // 4)
    k_ref[...] = k.astype(BF16)
    vt_ref[0] = kv[:, LANES:2 * LANES].T.astype(BF16)

    dq = _rope(_head_rms(dq, gdq_ref[...], bd), cos1, sin1, HEAD // 2) * Q_SCALE
    _place_heads(dq, lambda h: h % 2, dqz_ref)
    dk = _rope(_head_rms(dk, gdk_ref[...], bd), cos1, sin1, HEAD // 2)
    dk_ref[...] = dk.astype(BF16)
    dvt_ref[0] = dv.T.astype(BF16)

    mq = _head_rms(mq, gmq_ref[...], bd) * Q_SCALE
    _place_heads(mq, lambda h: h % 2, mqz_ref)


def _proj_odd(x2, seq, g, w, cos1, sin1, cosa, sina, gq, gk, gdq, gdk, gmq, bd):
    t = x2.shape[0]
    n_tiles = t // ROW_TILE
    tiles_per_seq = seq // ROW_TILE
    row = lambda w: pl.BlockSpec((1, w), lambda i: (0, 0))
    tab = pl.BlockSpec((ROW_TILE, LANES), lambda i: (i % tiles_per_seq, 0))
    return pl.pallas_call(
        _proj_odd_kernel,
        grid=(n_tiles,),
        in_specs=[
            pl.BlockSpec((ROW_TILE, D_MODEL), lambda i: (i, 0)),
            row(D_MODEL),
            pl.BlockSpec((D_MODEL, IN_W), lambda i: (0, 0)),
            tab, tab, tab, tab, row(512), row(LANES), row(512), row(512), row(256),
            pl.BlockSpec((2 * LANES, 2 * LANES), lambda i: (0, 0)),
        ],
        out_specs=[
            pl.BlockSpec((N_Q_HEADS, ROW_TILE, LANES), lambda i: (0, i, 0)),
            pl.BlockSpec((ROW_TILE, LANES), lambda i: (i, 0)),
            pl.BlockSpec((1, LANES, ROW_TILE), lambda i: (i, 0, 0)),
            pl.BlockSpec((N_Q_HEADS, ROW_TILE, LANES), lambda i: (0, i, 0)),
            pl.BlockSpec((ROW_TILE, 512), lambda i: (i, 0)),
            pl.BlockSpec((1, 512, ROW_TILE), lambda i: (i, 0, 0)),
            pl.BlockSpec((MEM_HEADS, ROW_TILE, LANES), lambda i: (0, i, 0)),
            pl.BlockSpec((ROW_TILE, MIX_W), lambda i: (i, 0)),
        ],
        out_shape=[
            jax.ShapeDtypeStruct((N_Q_HEADS, t, LANES), BF16),
            jax.ShapeDtypeStruct((t, LANES), BF16),
            jax.ShapeDtypeStruct((n_tiles, LANES, ROW_TILE), BF16),
            jax.ShapeDtypeStruct((N_Q_HEADS, t, LANES), BF16),
            jax.ShapeDtypeStruct((t, 512), BF16),
            jax.ShapeDtypeStruct((n_tiles, 512, ROW_TILE), BF16),
            jax.ShapeDtypeStruct((MEM_HEADS, t, LANES), BF16),
            jax.ShapeDtypeStruct((t, MIX_W), F32),
        ],
        compiler_params=_params(1),
        name="proj_odd",
    )(x2, g, w, cos1, sin1, cosa, sina, gq, gk, gdq, gdk, gmq, bd)


def _flash_t(qz_ref, k_ref, vt_ref, v_rows, m_sc, l_sc, acc_sc):
    n_maps = qz_ref.shape[0]
    n_chunks, _, chunk = vt_ref.shape
    m_sc[...] = jnp.full(m_sc.shape, NEG, F32)
    l_sc[...] = jnp.zeros(l_sc.shape, F32)
    acc_sc[...] = jnp.zeros(acc_sc.shape, F32)

    def body(c, carry):
        kc = k_ref[pl.ds(pl.multiple_of(c * chunk, chunk), chunk), :]
        vc = vt_ref[c]
        for g in range(n_maps):
            r0, nr = v_rows(g)
            s = _dot_nt(kc, qz_ref[g])
            m_old = m_sc[g]
            m_new = jnp.maximum(m_old, jnp.max(s, axis=0, keepdims=True))
            alpha = jnp.exp2(m_old - m_new)
            p = jnp.exp2(s - m_new)
            l_sc[g] = alpha * l_sc[g] + jnp.sum(p, axis=0, keepdims=True)
            acc_sc[g] = alpha * acc_sc[g] + _dot(vc[r0:r0 + nr, :], p.astype(BF16))
            m_sc[g] = m_new
        return carry

    lax.fori_loop(0, n_chunks, body, 0)


def _stream_t(qz_ref, k_ref, vt_ref, v_rows, acc_sc):
    n_maps, tq, _ = qz_ref.shape
    n_tiles, _, tile = vt_ref.shape
    per_tile = tile // KV_CHUNK
    n_chunks = n_tiles * per_tile

    def one_map(g, carry):
        qt = qz_ref[g].astype(F32).T.astype(BF16)
        r0, nr = v_rows(g)

        def score(c):
            return _dot(k_ref[c * KV_CHUNK:(c + 1) * KV_CHUNK, :], qt)

        pending = [score(c) for c in range(min(KV_SKEW, n_chunks))]
        l8 = jnp.zeros((SUBLANES, tq), F32)
        pv = None
        for c in range(n_chunks):
            if c + KV_SKEW < n_chunks:
                pending.append(score(c + KV_SKEW))
            off = (c % per_tile) * KV_CHUNK
            p = jnp.exp2(pending.pop(0))
            l8 = l8 + jnp.sum(p.reshape(KV_CHUNK // SUBLANES, SUBLANES, tq), axis=0)
            d = _dot(vt_ref[c // per_tile, pl.ds(r0, nr), off:off + KV_CHUNK], p.astype(BF16))
            pv = d if pv is None else pv + d
        acc_sc[g] = pv * (1.0 / jnp.sum(l8, axis=0, keepdims=True))
        return carry

    lax.fori_loop(0, n_maps, one_map, 0)
    return [acc_sc[g] for g in range(n_maps)]


def _axial_kernel(qz_ref, k_ref, vt_ref, mqz_ref, mk_ref, mvt_ref, g1_ref, gm_ref, y1_ref, ym_ref,
                  *scratch, bounded):
    v_rows = lambda g: ((g // GQA_GROUP) * HEAD, HEAD)
    if bounded:
        kv_rows = lambda g: (pl.multiple_of((g // GQA_GROUP) * HEAD, HEAD), HEAD)
        outs = _stream_t(qz_ref, k_ref, vt_ref, kv_rows, *scratch)
    else:
        m_sc, l_sc, acc_sc = scratch
        _flash_t(qz_ref, k_ref, vt_ref, v_rows, m_sc, l_sc, acc_sc)
        outs = [acc_sc[g] * (1.0 / l_sc[g]) for g in range(N_Q_HEADS)]
    y1 = jnp.concatenate(outs, axis=0).T
    y1_ref[...] = (y1 * g1_ref[...]).astype(BF16)
    ym = _mem_attention_t(mqz_ref, mk_ref, mvt_ref, bounded=bounded).T
    ym_ref[...] = (ym * gm_ref[...]).astype(BF16)


def _axial_attention(qz, k, vt, mqz, mk, mvt, gate, batch, seq, bounded):
    t = k.shape[0]
    tq = STREAM_Q_TILE if bounded else Q_TILE
    nq = seq // tq
    n_chunks = seq // ROW_TILE
    n_mem = mk.shape[1]
    tok = lambda b, q: b * nq + q
    scratch = [pltpu.VMEM((N_Q_HEADS, HEAD, tq), F32)] if bounded else [
        pltpu.VMEM((N_Q_HEADS, 1, tq), F32),
        pltpu.VMEM((N_Q_HEADS, 1, tq), F32),
        pltpu.VMEM((N_Q_HEADS, HEAD, tq), F32),
    ]
    return pl.pallas_call(
        functools.partial(_axial_kernel, bounded=bounded),
        grid=(batch, nq),
        in_specs=[
            pl.BlockSpec((N_Q_HEADS, tq, LANES), lambda b, q: (0, tok(b, q), 0)),
            pl.BlockSpec((seq, LANES), lambda b, q: (b, 0)),
            pl.BlockSpec((n_chunks, LANES, ROW_TILE), lambda b, q: (b, 0, 0)),
            pl.BlockSpec((MEM_HEADS, tq, LANES), lambda b, q: (0, tok(b, q), 0)),
            pl.BlockSpec((1, n_mem, MEM_HEADS * HEAD), lambda b, q: (b, 0, 0)),
            pl.BlockSpec((1, MEM_HEADS * HEAD, n_mem), lambda b, q: (b, 0, 0)),
            pl.BlockSpec((tq, 512), lambda b, q: (tok(b, q), 0)),
            pl.BlockSpec((tq, 256), lambda b, q: (tok(b, q), 4)),
        ],
        out_specs=[
            pl.BlockSpec((tq, 512), lambda b, q: (tok(b, q), 0)),
            pl.BlockSpec((tq, 256), lambda b, q: (tok(b, q), 0)),
        ],
        out_shape=[
            jax.ShapeDtypeStruct((t, 512), BF16),
            jax.ShapeDtypeStruct((t, 256), BF16),
        ],
        scratch_shapes=scratch,
        compiler_params=_params(2),
        name="axial_attention" if bounded else "axial_attention_online",
    )(qz, k, vt, mqz, mk, mvt, gate, gate)


def _diff_kernel(qz_ref, k_ref, vt_ref, lam_ref, sg_ref, g2_ref, y2_ref, *scratch, lambda_init, bounded):
    v_rows = lambda g: (0, 2 * HEAD)
    if bounded:
        o0, o1 = _stream_t(qz_ref, k_ref, vt_ref, v_rows, *scratch)
    else:
        m_sc, l_sc, acc_sc = scratch
        _flash_t(qz_ref, k_ref, vt_ref, v_rows, m_sc, l_sc, acc_sc)
        o0, o1 = (acc_sc[g] * (1.0 / l_sc[g]) for g in range(2))
    lv = lam_ref[...]
    lam = (jnp.exp(jnp.sum(lv[0:1] * lv[1:2], axis=-1, keepdims=True))
           - jnp.exp(jnp.sum(lv[2:3] * lv[3:4], axis=-1, keepdims=True)) + lambda_init)
    o = o0 - lam * o1
    ms = jnp.mean(o * o, axis=0, keepdims=True)
    on = (o * lax.rsqrt(ms + EPS)).T
    y2_ref[...] = (on * sg_ref[...] * (1.0 - lambda_init) * g2_ref[...]).astype(BF16)


def _diff_attention(dqz, dk, dvt, lam, subln_g, gate, batch, seq, layer, bounded):
    t = dk.shape[0]
    tq = STREAM_Q_TILE if bounded else Q_TILE
    nq = seq // tq
    n_chunks = seq // ROW_TILE
    lambda_init = 0.8 - 0.6 * math.exp(-0.3 * layer)
    tok = lambda b, q: b * nq + q
    scratch = [pltpu.VMEM((2, 2 * HEAD, tq), F32)] if bounded else [
        pltpu.VMEM((2, 1, tq), F32),
        pltpu.VMEM((2, 1, tq), F32),
        pltpu.VMEM((2, 2 * HEAD, tq), F32),
    ]
    return pl.pallas_call(
        functools.partial(_diff_kernel, lambda_init=lambda_init, bounded=bounded),
        grid=(batch, DIFF_HEADS, nq),
        in_specs=[
            pl.BlockSpec((2, tq, LANES), lambda b, h, q: (h, tok(b, q), 0)),
            pl.BlockSpec((seq, LANES), lambda b, h, q: (b, h)),
            pl.BlockSpec((n_chunks, LANES, ROW_TILE), lambda b, h, q: (b, h, 0)),
            pl.BlockSpec((4, HEAD), lambda b, h, q: (0, 0)),
            pl.BlockSpec((1, 2 * HEAD), lambda b, h, q: (0, 0)),
            pl.BlockSpec((tq, LANES), lambda b, h, q: (tok(b, q), 4 + h)),
        ],
        out_specs=pl.BlockSpec((tq, LANES), lambda b, h, q: (tok(b, q), h)),
        out_shape=jax.ShapeDtypeStruct((t, 512), BF16),
        scratch_shapes=scratch,
        compiler_params=_params(3),
        name="diff_attention" if bounded else "diff_attention_online",
    )(dqz, dk, dvt, lam, subln_g.reshape(1, 2 * HEAD), gate)


def _out_proj_kernel(y1_ref, y2_ref, ym_ref, x_ref, w_ref, o_ref):
    acc = _dot(y1_ref[...], w_ref[0:512, :])
    acc = acc + _dot(y2_ref[...], w_ref[512:1024, :])
    acc = acc + _dot(ym_ref[...], w_ref[1024:MIX_W, :])
    o_ref[...] = x_ref[...] + acc


def _out_proj(y1, y2, ym, x2, w):
    t = x2.shape[0]
    return pl.pallas_call(
        _out_proj_kernel,
        grid=(t // ROW_TILE,),
        in_specs=[
            pl.BlockSpec((ROW_TILE, 512), lambda i: (i, 0)),
            pl.BlockSpec((ROW_TILE, 512), lambda i: (i, 0)),
            pl.BlockSpec((ROW_TILE, 256), lambda i: (i, 0)),
            pl.BlockSpec((ROW_TILE, D_MODEL), lambda i: (i, 0)),
            pl.BlockSpec((MIX_W, D_MODEL), lambda i: (0, 0)),
        ],
        out_specs=pl.BlockSpec((ROW_TILE, D_MODEL), lambda i: (i, 0)),
        out_shape=jax.ShapeDtypeStruct((t, D_MODEL), F32),
        compiler_params=_params(1),
        name="out_proj",
    )(y1, y2, ym, x2, w)


def _rope_angles(pos, dim):
    inv = ROPE_THETA ** (-jnp.arange(0, dim, 2, dtype=F32) / dim)
    return pos.astype(F32)[:, None] * inv[None, :]


def _rope_tables(seq):
    pos = jnp.arange(seq)
    a1 = _rope_angles(pos, HEAD)
    cos1 = jnp.concatenate([jnp.cos(a1), jnp.cos(a1)], axis=-1)
    sin1 = jnp.concatenate([-jnp.sin(a1), jnp.sin(a1)], axis=-1)
    ar = _rope_angles(pos // GRID_W, HEAD // 2)
    ac = _rope_angles(pos % GRID_W, HEAD // 2)
    cosa = jnp.concatenate([jnp.cos(ar), jnp.cos(ar), jnp.cos(ac), jnp.cos(ac)], axis=-1)
    sina = jnp.concatenate([-jnp.sin(ar), jnp.sin(ar), -jnp.sin(ac), jnp.sin(ac)], axis=-1)
    rep = lambda a: jnp.tile(a, (1, LANES // HEAD))
    return rep(cos1), rep(sin1), rep(cosa), rep(sina)


def _scores_bounded(qk_g):
    bound = HEAD * Q_SCALE * 1.01 * jnp.max(jnp.abs(qk_g[0])) * jnp.max(jnp.abs(qk_g[1]))
    return bound <= SAFE_SCORE_BOUND


def _tile_gain(g, width):
    return jnp.tile(g.astype(F32), width // HEAD).reshape(1, width)


def _trunk(x, mem, p):
    batch, seq, _ = x.shape
    x2 = x.reshape(batch * seq, D_MODEL)
    cos1, sin1, cosa, sina = _rope_tables(seq)
    bd = p["bd"]
    gk_mem = jnp.stack([_tile_gain(p["mem_qk_g"][l, 1], 256) for l in range(DEPTH)])
    mk_all, mvt_all = _memkv(mem, p["mem_norm_g"], p["w_mem_kv"], gk_mem, bd)
    for l in range(DEPTH):
        g_in, w_in = p["norm_g"][l].reshape(1, D_MODEL), p["w_in"][l]
        gmq = _tile_gain(p["mem_qk_g"][l, 0], 256)
        mk, mvt = mk_all[l], mvt_all[l]
        if l % 2 == 0:
            e = l // 2
            y1, qz, k, vt, mqz, gate = _proj_even(
                x2, seq, g_in, w_in, cos1, sin1, _tile_gain(p["swa_qk_g"][e, 0], 512),
                _tile_gain(p["swa_qk_g"][e, 1], LANES), gmq, p["conv_w"][e], bd)
            sink_ok = jnp.max(jnp.abs(p["swa_sink"][e])) * LOG2E <= SAFE_SCORE_BOUND
            y2, ym = lax.cond(
                _scores_bounded(p["swa_qk_g"][e]) & _scores_bounded(p["mem_qk_g"][l]) & sink_ok,
                functools.partial(_window_attention, batch=batch, seq=seq, bounded=True),
                functools.partial(_window_attention, batch=batch, seq=seq, bounded=False),
                p["swa_sink"][e], qz, k, vt, mqz, mk, mvt, gate)
        else:
            o = l // 2
            qz, k, vt, dqz, dk, dvt, mqz, gate = _proj_odd(
                x2, seq, g_in, w_in, cos1, sin1, cosa, sina, _tile_gain(p["ax_qk_g"][o, 0], 512),
                _tile_gain(p["ax_qk_g"][o, 1], LANES), _tile_gain(p["diff_qk_g"][o, 0], 512),
                _tile_gain(p["diff_qk_g"][o, 1], 512), gmq, bd)
            y1, ym = lax.cond(
                _scores_bounded(p["ax_qk_g"][o]) & _scores_bounded(p["mem_qk_g"][l]),
                functools.partial(_axial_attention, batch=batch, seq=seq, bounded=True),
                functools.partial(_axial_attention, batch=batch, seq=seq, bounded=False),
                qz, k, vt, mqz, mk, mvt, gate)
            y2 = lax.cond(
                _scores_bounded(p["diff_qk_g"][o]),
                functools.partial(_diff_attention, batch=batch, seq=seq, layer=l, bounded=True),
                functools.partial(_diff_attention, batch=batch, seq=seq, layer=l, bounded=False),
                dqz, dk, dvt, p["diff_lambda"][o], p["diff_subln_g"][o], gate)
        x2 = _out_proj(y1, y2, ym, x2, p["w_out"][l])
    return x2.reshape(batch, seq, D_MODEL)


def kernel(x_prompt, x_sample, mem_prompt, mem_sample, norm_g, w_in, w_out, mem_norm_g, w_mem_kv,
           mem_qk_g, conv_w, swa_qk_g, swa_sink, ax_qk_g, diff_qk_g, diff_lambda, diff_subln_g):
    grp = jnp.arange(2 * LANES) // HEAD
    p = dict(
        norm_g=norm_g, w_in=w_in.astype(BF16), w_out=w_out.astype(BF16), mem_norm_g=mem_norm_g,
        w_mem_kv=w_mem_kv.astype(BF16), mem_qk_g=mem_qk_g, conv_w=conv_w, swa_qk_g=swa_qk_g,
        swa_sink=swa_sink, ax_qk_g=ax_qk_g, diff_qk_g=diff_qk_g, diff_lambda=diff_lambda,
        diff_subln_g=diff_subln_g, bd=(grp[:, None] == grp[None, :]).astype(BF16))
    return (_trunk(x_prompt, mem_prompt, p), _trunk(x_sample, mem_sample, p))
```

```python
import functools
import math

import jax
import jax.numpy as jnp
from jax import lax
from jax.experimental import pallas as pl
from jax.experimental.pallas import tpu as pltpu

F32 = jnp.float32
BF16 = jnp.bfloat16

D_MODEL = 1024
DEPTH = 4
HEAD = 64
LANES = 128
SUBLANES = 8
N_Q_HEADS = 8
GQA_GROUP = 4
DIFF_HEADS = 4
MEM_HEADS = 4
CONV_W = 512
WINDOW = 128
GRID_W = 64
ROPE_THETA = 10000.0
EPS = 1e-6
NEG = -1e30
LOG2E = 1.4426950408889634
Q_SCALE = HEAD ** -0.5 * LOG2E
SAFE_SCORE_BOUND = 60.0

IN_W = 3840
MIX_W = 1280
GATE_OFF = 2560
MQ_OFF = 2304

ROW_TILE = 512
Q_TILE = 256
STREAM_Q_TILE = 512
KV_CHUNK = 256
LOOP_GROUP = 16
KV_SKEW = 3
VMEM_LIMIT = 56 * 1024 * 1024

_NT = (((1,), (1,)), ((), ()))


def _params(n_axes):
    return pltpu.CompilerParams(dimension_semantics=("arbitrary",) * n_axes, vmem_limit_bytes=VMEM_LIMIT)


def _dot(a, b):
    return jnp.dot(a, b, preferred_element_type=F32)


def _dot_nt(a, b):
    return lax.dot_general(a, b, _NT, preferred_element_type=F32)


def _rms_rows(x, g):
    ms = jnp.mean(x * x, axis=-1, keepdims=True)
    return x * lax.rsqrt(ms + EPS) * g


def _head_sumsq(x, bd):
    w = x.shape[1]
    outs = []
    step = 2 * LANES if w % (2 * LANES) == 0 else LANES
    for c in range(0, w, step):
        sq = x[:, c:c + step]
        sq = sq * sq
        hi = sq.astype(BF16)
        lo = (sq - hi.astype(F32)).astype(BF16)
        b = bd[:step, :step]
        outs.append(_dot(hi, b) + _dot(lo, b))
    return outs[0] if len(outs) == 1 else jnp.concatenate(outs, axis=1)


def _head_rms(x, g, bd):
    return x * lax.rsqrt(_head_sumsq(x, bd) * (1.0 / HEAD) + EPS) * g


def _rope(x, cos, sin_signed, half):
    r = x.shape[0]
    lane = lax.broadcasted_iota(jnp.int32, (r, LANES), 1)
    first = (lane & (2 * half - 1)) < half
    outs = []
    for c in range(0, x.shape[1], LANES):
        xc = x[:, c:c + LANES]
        partner = jnp.where(first, pltpu.roll(xc, LANES - half, 1), pltpu.roll(xc, half, 1))
        outs.append(xc * cos + partner * sin_signed)
    return outs[0] if len(outs) == 1 else jnp.concatenate(outs, axis=1)


def _place_heads(x, target_half, out_ref):
    r = x.shape[0]
    low = lax.broadcasted_iota(jnp.int32, (r, LANES), 1) < HEAD
    for h in range(x.shape[1] // HEAD):
        xc = x[:, (h // 2) * LANES:(h // 2 + 1) * LANES]
        th = target_half(h)
        if th != h % 2:
            xc = pltpu.roll(xc, HEAD, 1)
        keep = low if th == 0 else jnp.logical_not(low)
        out_ref[h] = jnp.where(keep, xc, 0.0).astype(BF16)


def _silu(z):
    return z * (1.0 / (1.0 + jnp.exp(-z)))


def _project(h, w_ref, splits):
    return [_dot(h, w_ref[:, a:b]) for a, b in splits]


def _memkv_kernel(mem_ref, g_ref, w_ref, gk_ref, bd_ref, mk_ref, mvt_ref):
    h = _rms_rows(mem_ref[0], g_ref[0]).astype(BF16)
    mkv = _dot(h, w_ref[0])
    half = mkv.shape[1] // 2
    mk = _head_rms(mkv[:, :half], gk_ref[0], bd_ref[...])
    mk_ref[0, 0] = mk.astype(BF16)
    mvt_ref[0, 0] = mkv[:, half:].T.astype(BF16)


def _memkv(mem, mem_norm_g, w_mem_kv, gk_tiled, bd):
    b, n_mem, _ = mem.shape
    width = MEM_HEADS * HEAD
    return pl.pallas_call(
        _memkv_kernel,
        grid=(DEPTH, b),
        in_specs=[
            pl.BlockSpec((1, n_mem, D_MODEL), lambda l, i: (i, 0, 0)),
            pl.BlockSpec((1, 1, D_MODEL), lambda l, i: (l, 0, 0)),
            pl.BlockSpec((1, D_MODEL, 2 * width), lambda l, i: (l, 0, 0)),
            pl.BlockSpec((1, 1, width), lambda l, i: (l, 0, 0)),
            pl.BlockSpec((2 * LANES, 2 * LANES), lambda l, i: (0, 0)),
        ],
        out_specs=[
            pl.BlockSpec((1, 1, n_mem, width), lambda l, i: (l, i, 0, 0)),
            pl.BlockSpec((1, 1, width, n_mem), lambda l, i: (l, i, 0, 0)),
        ],
        out_shape=[
            jax.ShapeDtypeStruct((DEPTH, b, n_mem, width), BF16),
            jax.ShapeDtypeStruct((DEPTH, b, width, n_mem), BF16),
        ],
        compiler_params=_params(2),
        name="mem_kv",
    )(mem, mem_norm_g.reshape(DEPTH, 1, D_MODEL), w_mem_kv, gk_tiled, bd)


def _proj_even_kernel(x_ref, xp_ref, xn_ref, g_ref, w_ref, cos_ref, sin_ref, gq_ref, gk_ref,
                      gmq_ref, cw_ref, bd_ref,
                      y1_ref, qz_ref, k_ref, vt_ref, mqz_ref, gate_ref, conv_sc, *, tiles_per_seq):
    i = pl.program_id(0)
    pos_tile = i % tiles_per_seq
    bd = bd_ref[...]
    cos = cos_ref[...]
    sin = sin_ref[...]
    rows = x_ref.shape[0]

    x_ext = jnp.concatenate([x_ref[...], xp_ref[...], xn_ref[...]], axis=0)
    h_ext = _rms_rows(x_ext, g_ref[...]).astype(BF16)
    h = h_ext[:rows]
    gb, qkv, mq, z = _project(h, w_ref, [(0, 512), (1536, 2304), (MQ_OFF, GATE_OFF), (GATE_OFF, IN_W)])
    gchc = _dot(h_ext, w_ref[:, 512:1536])

    inner_ext = gchc[:, 0:512] * gchc[:, 512:1024]
    inner = inner_ext[:rows]
    prev_row = inner_ext[rows + SUBLANES - 1:rows + SUBLANES]
    next_row = inner_ext[rows + SUBLANES:rows + SUBLANES + 1]
    prev_row = jnp.where(pos_tile == 0, 0.0, prev_row)
    next_row = jnp.where(pos_tile == tiles_per_seq - 1, 0.0, next_row)
    conv_sc[pl.ds(SUBLANES, rows), :] = inner
    conv_sc[pl.ds(SUBLANES - 1, 1), :] = prev_row
    conv_sc[pl.ds(SUBLANES + rows, 1), :] = next_row
    cw = cw_ref[...]
    conv = (conv_sc[pl.ds(SUBLANES - 1, rows), :] * cw[0:1, :] + inner * cw[1:2, :]
            + conv_sc[pl.ds(SUBLANES + 1, rows), :] * cw[2:3, :])
    y1_ref[...] = (gb * conv * _silu(z[:, 0:512])).astype(BF16)
    gate_ref[...] = _silu(z[:, 512:MIX_W])

    q = _rope(_head_rms(qkv[:, 0:512], gq_ref[...], bd), cos, sin, HEAD // 2) * Q_SCALE
    _place_heads(q, lambda h: h // GQA_GROUP, qz_ref)
    k = _rope(_head_rms(qkv[:, 512:640], gk_ref[...], bd), cos, sin, HEAD // 2)
    k_ref[...] = k.astype(BF16)
    vt_ref[0] = qkv[:, 640:768].T.astype(BF16)

    mq = _head_rms(mq, gmq_ref[...], bd) * Q_SCALE
    _place_heads(mq, lambda h: h % 2, mqz_ref)


def _proj_even(x2, seq, g, w, cos, sin, gq, gk, gmq, conv_w, bd):
    t = x2.shape[0]
    n_tiles = t // ROW_TILE
    tiles_per_seq = seq // ROW_TILE
    sub_per_tile = ROW_TILE // SUBLANES
    last_sub = t // SUBLANES - 1
    row = lambda w: pl.BlockSpec((1, w), lambda i: (0, 0))
    tab = pl.BlockSpec((ROW_TILE, LANES), lambda i: (i % tiles_per_seq, 0))
    return pl.pallas_call(
        functools.partial(_proj_even_kernel, tiles_per_seq=tiles_per_seq),
        grid=(n_tiles,),
        in_specs=[
            pl.BlockSpec((ROW_TILE, D_MODEL), lambda i: (i, 0)),
            pl.BlockSpec((SUBLANES, D_MODEL), lambda i: (jnp.maximum(i * sub_per_tile - 1, 0), 0)),
            pl.BlockSpec((SUBLANES, D_MODEL), lambda i: (jnp.minimum((i + 1) * sub_per_tile, last_sub), 0)),
            row(D_MODEL),
            pl.BlockSpec((D_MODEL, IN_W), lambda i: (0, 0)),
            tab, tab, row(512), row(LANES), row(256),
            pl.BlockSpec((3, CONV_W), lambda i: (0, 0)),
            pl.BlockSpec((2 * LANES, 2 * LANES), lambda i: (0, 0)),
        ],
        out_specs=[
            pl.BlockSpec((ROW_TILE, CONV_W), lambda i: (i, 0)),
            pl.BlockSpec((N_Q_HEADS, ROW_TILE, LANES), lambda i: (0, i, 0)),
            pl.BlockSpec((ROW_TILE, LANES), lambda i: (i, 0)),
            pl.BlockSpec((1, LANES, ROW_TILE), lambda i: (i, 0, 0)),
            pl.BlockSpec((MEM_HEADS, ROW_TILE, LANES), lambda i: (0, i, 0)),
            pl.BlockSpec((ROW_TILE, 768), lambda i: (i, 0)),
        ],
        out_shape=[
            jax.ShapeDtypeStruct((t, CONV_W), BF16),
            jax.ShapeDtypeStruct((N_Q_HEADS, t, LANES), BF16),
            jax.ShapeDtypeStruct((t, LANES), BF16),
            jax.ShapeDtypeStruct((n_tiles, LANES, ROW_TILE), BF16),
            jax.ShapeDtypeStruct((MEM_HEADS, t, LANES), BF16),
            jax.ShapeDtypeStruct((t, 768), F32),
        ],
        scratch_shapes=[pltpu.VMEM((ROW_TILE + 2 * SUBLANES, CONV_W), F32)],
        compiler_params=_params(1),
        name="proj_even",
    )(x2, x2, x2, g, w, cos, sin, gq, gk, gmq, conv_w, bd)


def _mem_attention_t(mqz_ref, mk_ref, mvt_ref, bounded=False):
    outs = []
    if bounded:
        tq = mqz_ref.shape[1]
        pairs = []
        for grp in range(MEM_HEADS // 2):
            q2 = jnp.concatenate([mqz_ref[2 * grp], mqz_ref[2 * grp + 1]], axis=0)
            pairs.append(jnp.exp2(_dot_nt(mk_ref[0, :, grp * LANES:(grp + 1) * LANES], q2)))
        for h in range(MEM_HEADS):
            p = pairs[h // 2][:, (h % 2) * tq:(h % 2 + 1) * tq]
            l = jnp.sum(p, axis=0, keepdims=True)
            o = _dot(mvt_ref[0, h * HEAD:(h + 1) * HEAD, :], p.astype(BF16))
            outs.append(o * (1.0 / l))
        return jnp.concatenate(outs, axis=0)
    for h in range(MEM_HEADS):
        grp = h // 2
        s = _dot_nt(mk_ref[0, :, grp * LANES:(grp + 1) * LANES], mqz_ref[h])
        m = jnp.max(s, axis=0, keepdims=True)
        p = jnp.exp2(s - m)
        l = jnp.sum(p, axis=0, keepdims=True)
        o = _dot(mvt_ref[0, h * HEAD:(h + 1) * HEAD, :], p.astype(BF16))
        outs.append(o * (1.0 / l))
    return jnp.concatenate(outs, axis=0)


def _window_kernel(sink_ref, qz_ref, kp_ref, kc_ref, kn_ref, vp_ref, vc_ref, vn_ref, mqz_ref, mk_ref,
                   mvt_ref, g2_ref, gm_ref, y2_ref, ym_ref, *, seq):
    tq = qz_ref.shape[1]
    t0 = pl.program_id(1) * tq
    kband = jnp.concatenate([kp_ref[...], kc_ref[...], kn_ref[...]], axis=0)
    vband = jnp.concatenate([vp_ref[0], vc_ref[0], vn_ref[0]], axis=1)
    nk = tq + 2 * WINDOW
    j = lax.broadcasted_iota(jnp.int32, (nk, tq), 0)
    i = lax.broadcasted_iota(jnp.int32, (nk, tq), 1)
    kpos = j + (t0 - WINDOW)
    mask = (j >= i) & (j <= i + 2 * WINDOW) & (kpos >= 0) & (kpos < seq)
    outs = []
    for g in range(N_Q_HEADS):
        kv = g // GQA_GROUP
        s = jnp.where(mask, _dot_nt(kband, qz_ref[g]), NEG)
        sink = sink_ref[g] * LOG2E
        m = jnp.maximum(jnp.max(s, axis=0, keepdims=True), sink)
        p = jnp.exp2(s - m)
        l = jnp.sum(p, axis=0, keepdims=True) + jnp.exp2(sink - m)
        o = _dot(vband[kv * HEAD:(kv + 1) * HEAD, :], p.astype(BF16))
        outs.append(o * (1.0 / l))
    y2 = jnp.concatenate(outs, axis=0).T
    y2_ref[...] = (y2 * g2_ref[...]).astype(BF16)
    ym = _mem_attention_t(mqz_ref, mk_ref, mvt_ref).T
    ym_ref[...] = (ym * gm_ref[...]).astype(BF16)


def _window_stream_kernel(sink_ref, qz_ref, kp_ref, kc_ref, kn_ref, vp_ref, vc_ref, vn_ref, mqz_ref,
                          mk_ref, mvt_ref, g2_ref, gm_ref, y2_ref, ym_ref, *, seq):
    tq = qz_ref.shape[1]
    t0 = pl.program_id(1) * tq
    kband = jnp.concatenate([kp_ref[...], kc_ref[...], kn_ref[...]], axis=0)
    vband = jnp.concatenate([vp_ref[0], vc_ref[0], vn_ref[0]], axis=1)
    nk = 3 * WINDOW
    j = lax.broadcasted_iota(jnp.int32, (nk, WINDOW), 0)
    i = lax.broadcasted_iota(jnp.int32, (nk, WINDOW), 1)
    in_band = (j >= i) & (j <= i + 2 * WINDOW)
    sink_p = [jnp.exp2(jnp.full((1, WINDOW), sink_ref[g] * LOG2E, F32)) for g in range(N_Q_HEADS)]
    scores = []
    for a in range(tq // WINDOW):
        for kv in range(N_Q_HEADS // GQA_GROUP):
            q4 = jnp.concatenate([qz_ref[kv * GQA_GROUP + r, a * WINDOW:(a + 1) * WINDOW, :]
                                  for r in range(GQA_GROUP)], axis=0)
            scores.append(_dot_nt(kband[a * WINDOW:a * WINDOW + nk], q4))
    blocks = [[None] * (tq // WINDOW) for _ in range(N_Q_HEADS)]
    for a in range(tq // WINDOW):
        kpos = j + (t0 + (a - 1) * WINDOW)
        bias1 = jnp.where(in_band & (kpos >= 0) & (kpos < seq), 0.0, NEG)
        bias = jnp.concatenate([bias1] * GQA_GROUP, axis=1)
        for kv in range(N_Q_HEADS // GQA_GROUP):
            p = jnp.exp2(scores[a * (N_Q_HEADS // GQA_GROUP) + kv] + bias)
            sinks = jnp.concatenate(sink_p[kv * GQA_GROUP:(kv + 1) * GQA_GROUP], axis=1)
            l = jnp.sum(p, axis=0, keepdims=True) + sinks
            o = _dot(vband[kv * HEAD:(kv + 1) * HEAD, a * WINDOW:a * WINDOW + nk], p.astype(BF16))
            o = o * (1.0 / l)
            for r in range(GQA_GROUP):
                blocks[kv * GQA_GROUP + r][a] = o[:, r * WINDOW:(r + 1) * WINDOW]
    y2 = jnp.concatenate([jnp.concatenate(b, axis=1) for b in blocks], axis=0).T
    y2_ref[...] = (y2 * g2_ref[...]).astype(BF16)
    ym = _mem_attention_t(mqz_ref, mk_ref, mvt_ref, bounded=True).T
    ym_ref[...] = (ym * gm_ref[...]).astype(BF16)


def _window_attention(sink, qz, k, vt, mqz, mk, mvt, gate, batch, seq, bounded):
    t = k.shape[0]
    tq = STREAM_Q_TILE if bounded else Q_TILE
    nq = seq // tq
    w_per_q = tq // WINDOW
    w_per_seq = seq // WINDOW
    q_per_chunk = ROW_TILE // tq
    w_per_chunk = ROW_TILE // WINDOW
    n_mem = mk.shape[1]

    def prev_w(b, q):
        return b * w_per_seq + jnp.maximum(q * w_per_q - 1, 0)

    def next_w(b, q):
        return b * w_per_seq + jnp.minimum((q + 1) * w_per_q, w_per_seq - 1)

    tok = lambda b, q: b * nq + q
    return pl.pallas_call(
        functools.partial(_window_stream_kernel if bounded else _window_kernel, seq=seq),
        grid=(batch, nq),
        in_specs=[
            pl.BlockSpec(memory_space=pltpu.SMEM),
            pl.BlockSpec((N_Q_HEADS, tq, LANES), lambda b, q: (0, tok(b, q), 0)),
            pl.BlockSpec((WINDOW, LANES), lambda b, q: (prev_w(b, q), 0)),
            pl.BlockSpec((tq, LANES), lambda b, q: (tok(b, q), 0)),
            pl.BlockSpec((WINDOW, LANES), lambda b, q: (next_w(b, q), 0)),
            pl.BlockSpec((1, LANES, WINDOW), lambda b, q: (prev_w(b, q) // w_per_chunk, 0, prev_w(b, q) % w_per_chunk)),
            pl.BlockSpec((1, LANES, tq), lambda b, q: (tok(b, q) // q_per_chunk, 0, tok(b, q) % q_per_chunk)),
            pl.BlockSpec((1, LANES, WINDOW), lambda b, q: (next_w(b, q) // w_per_chunk, 0, next_w(b, q) % w_per_chunk)),
            pl.BlockSpec((MEM_HEADS, tq, LANES), lambda b, q: (0, tok(b, q), 0)),
            pl.BlockSpec((1, n_mem, MEM_HEADS * HEAD), lambda b, q: (b, 0, 0)),
            pl.BlockSpec((1, MEM_HEADS * HEAD, n_mem), lambda b, q: (b, 0, 0)),
            pl.BlockSpec((tq, 512), lambda b, q: (tok(b, q), 0)),
            pl.BlockSpec((tq, 256), lambda b, q: (tok(b, q), 2)),
        ],
        out_specs=[
            pl.BlockSpec((tq, 512), lambda b, q: (tok(b, q), 0)),
            pl.BlockSpec((tq, 256), lambda b, q: (tok(b, q), 0)),
        ],
        out_shape=[
            jax.ShapeDtypeStruct((t, 512), BF16),
            jax.ShapeDtypeStruct((t, 256), BF16),
        ],
        compiler_params=_params(2),
        name="window_attention" if bounded else "window_attention_max",
    )(sink, qz, k, k, k, vt, vt, vt, mqz, mk, mvt, gate, gate)


def _proj_odd_kernel(x_ref, g_ref, w_ref, cos1_ref, sin1_ref, cosa_ref, sina_ref, gq_ref, gk_ref,
                     gdq_ref, gdk_ref, gmq_ref, bd_ref,
                     qz_ref, k_ref, vt_ref, dqz_ref, dk_ref, dvt_ref, mqz_ref, gate_ref):
    bd = bd_ref[...]
    cos1, sin1 = cos1_ref[...], sin1_ref[...]
    cosa, sina = cosa_ref[...], sina_ref[...]
    h = _rms_rows(x_ref[...], g_ref[...]).astype(BF16)
    q, kv, dq, dk, dv, mq, z = _project(
        h, w_ref, [(0, 512), (512, 768), (768, 1280), (1280, 1792), (1792, 2304), (MQ_OFF, GATE_OFF),
                   (GATE_OFF, IN_W)])
    gate_ref[...] = _silu(z)

    q = _rope(_head_rms(q, gq_ref[...], bd), cosa, sina, HEAD // 4) * Q_SCALE
    _place_heads(q, lambda h: h // GQA_GROUP, qz_ref)
    k = _rope(_head_rms(kv[:, 0:LANES], gk_ref[...], bd), cosa, sina, HEAD // 4)
    k_ref[...] = k.astype(BF16)
    vt_ref[0] = kv[:, LANES:2 * LANES].T.astype(BF16)

    dq = _rope(_head_rms(dq, gdq_ref[...], bd), cos1, sin1, HEAD // 2) * Q_SCALE
    _place_heads(dq, lambda h: h % 2, dqz_ref)
    dk = _rope(_head_rms(dk, gdk_ref[...], bd), cos1, sin1, HEAD // 2)
    dk_ref[...] = dk.astype(BF16)
    dvt_ref[0] = dv.T.astype(BF16)

    mq = _head_rms(mq, gmq_ref[...], bd) * Q_SCALE
    _place_heads(mq, lambda h: h % 2, mqz_ref)


def _proj_odd(x2, seq, g, w, cos1, sin1, cosa, sina, gq, gk, gdq, gdk, gmq, bd):
    t = x2.shape[0]
    n_tiles = t // ROW_TILE
    tiles_per_seq = seq // ROW_TILE
    row = lambda w: pl.BlockSpec((1, w), lambda i: (0, 0))
    tab = pl.BlockSpec((ROW_TILE, LANES), lambda i: (i % tiles_per_seq, 0))
    return pl.pallas_call(
        _proj_odd_kernel,
        grid=(n_tiles,),
        in_specs=[
            pl.BlockSpec((ROW_TILE, D_MODEL), lambda i: (i, 0)),
            row(D_MODEL),
            pl.BlockSpec((D_MODEL, IN_W), lambda i: (0, 0)),
            tab, tab, tab, tab, row(512), row(LANES), row(512), row(512), row(256),
            pl.BlockSpec((2 * LANES, 2 * LANES), lambda i: (0, 0)),
        ],
        out_specs=[
            pl.BlockSpec((N_Q_HEADS, ROW_TILE, LANES), lambda i: (0, i, 0)),
            pl.BlockSpec((ROW_TILE, LANES), lambda i: (i, 0)),
            pl.BlockSpec((1, LANES, ROW_TILE), lambda i: (i, 0, 0)),
            pl.BlockSpec((N_Q_HEADS, ROW_TILE, LANES), lambda i: (0, i, 0)),
            pl.BlockSpec((ROW_TILE, 512), lambda i: (i, 0)),
            pl.BlockSpec((1, 512, ROW_TILE), lambda i: (i, 0, 0)),
            pl.BlockSpec((MEM_HEADS, ROW_TILE, LANES), lambda i: (0, i, 0)),
            pl.BlockSpec((ROW_TILE, MIX_W), lambda i: (i, 0)),
        ],
        out_shape=[
            jax.ShapeDtypeStruct((N_Q_HEADS, t, LANES), BF16),
            jax.ShapeDtypeStruct((t, LANES), BF16),
            jax.ShapeDtypeStruct((n_tiles, LANES, ROW_TILE), BF16),
            jax.ShapeDtypeStruct((N_Q_HEADS, t, LANES), BF16),
            jax.ShapeDtypeStruct((t, 512), BF16),
            jax.ShapeDtypeStruct((n_tiles, 512, ROW_TILE), BF16),
            jax.ShapeDtypeStruct((MEM_HEADS, t, LANES), BF16),
            jax.ShapeDtypeStruct((t, MIX_W), F32),
        ],
        compiler_params=_params(1),
        name="proj_odd",
    )(x2, g, w, cos1, sin1, cosa, sina, gq, gk, gdq, gdk, gmq, bd)


def _flash_t(qz_ref, k_ref, vt_ref, v_rows, m_sc, l_sc, acc_sc):
    n_maps = qz_ref.shape[0]
    n_chunks, _, chunk = vt_ref.shape
    m_sc[...] = jnp.full(m_sc.shape, NEG, F32)
    l_sc[...] = jnp.zeros(l_sc.shape, F32)
    acc_sc[...] = jnp.zeros(acc_sc.shape, F32)

    def body(c, carry):
        kc = k_ref[pl.ds(pl.multiple_of(c * chunk, chunk), chunk), :]
        vc = vt_ref[c]
        for g in range(n_maps):
            r0, nr = v_rows(g)
            s = _dot_nt(kc, qz_ref[g])
            m_old = m_sc[g]
            m_new = jnp.maximum(m_old, jnp.max(s, axis=0, keepdims=True))
            alpha = jnp.exp2(m_old - m_new)
            p = jnp.exp2(s - m_new)
            l_sc[g] = alpha * l_sc[g] + jnp.sum(p, axis=0, keepdims=True)
            acc_sc[g] = alpha * acc_sc[g] + _dot(vc[r0:r0 + nr, :], p.astype(BF16))
            m_sc[g] = m_new
        return carry

    lax.fori_loop(0, n_chunks, body, 0)


def _stream_t(qz_ref, k_ref, vt_ref, v_rows, acc_sc, *, group, ahead):
    n_maps, tq, _ = qz_ref.shape
    n_tiles, _, tile = vt_ref.shape
    per_tile = tile // KV_CHUNK
    n_chunks = n_tiles * per_tile
    n_groups = n_chunks // group
    assert n_chunks % group == 0

    def one_map(g, carry):
        qt = qz_ref[g].astype(F32).T.astype(BF16)
        r0, nr = v_rows(g)

        def scores(j):
            return [_dot(k_ref[c * KV_CHUNK:(c + 1) * KV_CHUNK, :], qt)
                    for c in range(j * group, (j + 1) * group)]

        pending = [scores(j) for j in range(min(ahead, n_groups))]
        l8 = jnp.zeros((SUBLANES, tq), F32)
        pv = None
        for j in range(n_groups):
            if j + ahead < n_groups:
                pending.append(scores(j + ahead))
            for u, s in enumerate(pending.pop(0)):
                c = j * group + u
                off = (c % per_tile) * KV_CHUNK
                p = jnp.exp2(s)
                l8 = l8 + jnp.sum(p.reshape(KV_CHUNK // SUBLANES, SUBLANES, tq), axis=0)
                d = _dot(vt_ref[c // per_tile, pl.ds(r0, nr), off:off + KV_CHUNK], p.astype(BF16))
                pv = d if pv is None else pv + d
        acc_sc[g] = pv * (1.0 / jnp.sum(l8, axis=0, keepdims=True))
        return carry

    lax.fori_loop(0, n_maps, one_map, 0)
    return [acc_sc[g] for g in range(n_maps)]


def _stream_loop_t(qz_ref, k_ref, vt_ref, nr, acc_sc):
    n_maps, tq, _ = qz_ref.shape
    n_tiles, _, tile = vt_ref.shape
    per_tile = tile // KV_CHUNK
    group = math.gcd(LOOP_GROUP, n_tiles * per_tile)
    assert group % per_tile == 0
    outs = []
    for g in range(n_maps):
        qg = qz_ref[g]
        acc_sc[g] = jnp.zeros((nr, tq), F32)

        def body(i, l8, g=g, qg=qg):
            where = [(i * (group // per_tile) + u // per_tile, (u % per_tile) * KV_CHUNK)
                     for u in range(group)]
            scores = []
            for t, off in where:
                kc = k_ref[pl.ds(pl.multiple_of(t * tile + off, KV_CHUNK), KV_CHUNK), :]
                scores.append(_dot_nt(kc, qg))
            pv = None
            for (t, off), s in zip(where, scores):
                p = jnp.exp2(s)
                l8 = l8 + jnp.sum(p.reshape(KV_CHUNK // SUBLANES, SUBLANES, tq), axis=0)
                d = _dot(vt_ref[t, 0:nr, off:off + KV_CHUNK], p.astype(BF16))
                pv = d if pv is None else pv + d
            acc_sc[g] += pv
            return l8

        l8 = lax.fori_loop(0, n_tiles * per_tile // group, body, jnp.zeros((SUBLANES, tq), F32))
        outs.append(acc_sc[g] * (1.0 / jnp.sum(l8, axis=0, keepdims=True)))
    return outs


def _axial_kernel(qz_ref, k_ref, vt_ref, mqz_ref, mk_ref, mvt_ref, g1_ref, gm_ref, y1_ref, ym_ref,
                  *scratch, bounded):
    v_rows = lambda g: ((g // GQA_GROUP) * HEAD, HEAD)
    if bounded:
        kv_rows = lambda g: (pl.multiple_of((g // GQA_GROUP) * HEAD, HEAD), HEAD)
        outs = _stream_t(qz_ref, k_ref, vt_ref, kv_rows, *scratch, group=1, ahead=KV_SKEW)
    else:
        m_sc, l_sc, acc_sc = scratch
        _flash_t(qz_ref, k_ref, vt_ref, v_rows, m_sc, l_sc, acc_sc)
        outs = [acc_sc[g] * (1.0 / l_sc[g]) for g in range(N_Q_HEADS)]
    y1 = jnp.concatenate(outs, axis=0).T
    y1_ref[...] = (y1 * g1_ref[...]).astype(BF16)
    ym = _mem_attention_t(mqz_ref, mk_ref, mvt_ref, bounded=bounded).T
    ym_ref[...] = (ym * gm_ref[...]).astype(BF16)


def _axial_attention(qz, k, vt, mqz, mk, mvt, gate, batch, seq, bounded):
    t = k.shape[0]
    tq = STREAM_Q_TILE if bounded else Q_TILE
    nq = seq // tq
    n_chunks = seq // ROW_TILE
    n_mem = mk.shape[1]
    tok = lambda b, q: b * nq + q
    scratch = [pltpu.VMEM((N_Q_HEADS, HEAD, tq), F32)] if bounded else [
        pltpu.VMEM((N_Q_HEADS, 1, tq), F32),
        pltpu.VMEM((N_Q_HEADS, 1, tq), F32),
        pltpu.VMEM((N_Q_HEADS, HEAD, tq), F32),
    ]
    return pl.pallas_call(
        functools.partial(_axial_kernel, bounded=bounded),
        grid=(batch, nq),
        in_specs=[
            pl.BlockSpec((N_Q_HEADS, tq, LANES), lambda b, q: (0, tok(b, q), 0)),
            pl.BlockSpec((seq, LANES), lambda b, q: (b, 0)),
            pl.BlockSpec((n_chunks, LANES, ROW_TILE), lambda b, q: (b, 0, 0)),
            pl.BlockSpec((MEM_HEADS, tq, LANES), lambda b, q: (0, tok(b, q), 0)),
            pl.BlockSpec((1, n_mem, MEM_HEADS * HEAD), lambda b, q: (b, 0, 0)),
            pl.BlockSpec((1, MEM_HEADS * HEAD, n_mem), lambda b, q: (b, 0, 0)),
            pl.BlockSpec((tq, 512), lambda b, q: (tok(b, q), 0)),
            pl.BlockSpec((tq, 256), lambda b, q: (tok(b, q), 4)),
        ],
        out_specs=[
            pl.BlockSpec((tq, 512), lambda b, q: (tok(b, q), 0)),
            pl.BlockSpec((tq, 256), lambda b, q: (tok(b, q), 0)),
        ],
        out_shape=[
            jax.ShapeDtypeStruct((t, 512), BF16),
            jax.ShapeDtypeStruct((t, 256), BF16),
        ],
        scratch_shapes=scratch,
        compiler_params=_params(2),
        name="axial_attention" if bounded else "axial_attention_online",
    )(qz, k, vt, mqz, mk, mvt, gate, gate)


def _diff_kernel(qz_ref, k_ref, vt_ref, lam_ref, sg_ref, g2_ref, y2_ref, *scratch, lambda_init, bounded):
    v_rows = lambda g: (0, 2 * HEAD)
    if bounded:
        o0, o1 = _stream_loop_t(qz_ref, k_ref, vt_ref, 2 * HEAD, *scratch)
    else:
        m_sc, l_sc, acc_sc = scratch
        _flash_t(qz_ref, k_ref, vt_ref, v_rows, m_sc, l_sc, acc_sc)
        o0, o1 = (acc_sc[g] * (1.0 / l_sc[g]) for g in range(2))
    lv = lam_ref[...]
    lam = (jnp.exp(jnp.sum(lv[0:1] * lv[1:2], axis=-1, keepdims=True))
           - jnp.exp(jnp.sum(lv[2:3] * lv[3:4], axis=-1, keepdims=True)) + lambda_init)
    o = o0 - lam * o1
    ms = jnp.mean(o * o, axis=0, keepdims=True)
    on = (o * lax.rsqrt(ms + EPS)).T
    y2_ref[...] = (on * sg_ref[...] * (1.0 - lambda_init) * g2_ref[...]).astype(BF16)


def _diff_attention(dqz, dk, dvt, lam, subln_g, gate, batch, seq, layer, bounded):
    t = dk.shape[0]
    tq = STREAM_Q_TILE if bounded else Q_TILE
    nq = seq // tq
    n_chunks = seq // ROW_TILE
    lambda_init = 0.8 - 0.6 * math.exp(-0.3 * layer)
    tok = lambda b, q: b * nq + q
    scratch = [pltpu.VMEM((2, 2 * HEAD, tq), F32)] if bounded else [
        pltpu.VMEM((2, 1, tq), F32),
        pltpu.VMEM((2, 1, tq), F32),
        pltpu.VMEM((2, 2 * HEAD, tq), F32),
    ]
    return pl.pallas_call(
        functools.partial(_diff_kernel, lambda_init=lambda_init, bounded=bounded),
        grid=(batch, DIFF_HEADS, nq),
        in_specs=[
            pl.BlockSpec((2, tq, LANES), lambda b, h, q: (h, tok(b, q), 0)),
            pl.BlockSpec((seq, LANES), lambda b, h, q: (b, h)),
            pl.BlockSpec((n_chunks, LANES, ROW_TILE), lambda b, h, q: (b, h, 0)),
            pl.BlockSpec((4, HEAD), lambda b, h, q: (0, 0)),
            pl.BlockSpec((1, 2 * HEAD), lambda b, h, q: (0, 0)),
            pl.BlockSpec((tq, LANES), lambda b, h, q: (tok(b, q), 4 + h)),
        ],
        out_specs=pl.BlockSpec((tq, LANES), lambda b, h, q: (tok(b, q), h)),
        out_shape=jax.ShapeDtypeStruct((t, 512), BF16),
        scratch_shapes=scratch,
        compiler_params=_params(3),
        name="diff_attention" if bounded else "diff_attention_online",
    )(dqz, dk, dvt, lam, subln_g.reshape(1, 2 * HEAD), gate)


def _out_proj_kernel(y1_ref, y2_ref, ym_ref, x_ref, w_ref, o_ref):
    acc = _dot(y1_ref[...], w_ref[0:512, :])
    acc = acc + _dot(y2_ref[...], w_ref[512:1024, :])
    acc = acc + _dot(ym_ref[...], w_ref[1024:MIX_W, :])
    o_ref[...] = x_ref[...] + acc


def _out_proj(y1, y2, ym, x2, w):
    t = x2.shape[0]
    return pl.pallas_call(
        _out_proj_kernel,
        grid=(t // ROW_TILE,),
        in_specs=[
            pl.BlockSpec((ROW_TILE, 512), lambda i: (i, 0)),
            pl.BlockSpec((ROW_TILE, 512), lambda i: (i, 0)),
            pl.BlockSpec((ROW_TILE, 256), lambda i: (i, 0)),
            pl.BlockSpec((ROW_TILE, D_MODEL), lambda i: (i, 0)),
            pl.BlockSpec((MIX_W, D_MODEL), lambda i: (0, 0)),
        ],
        out_specs=pl.BlockSpec((ROW_TILE, D_MODEL), lambda i: (i, 0)),
        out_shape=jax.ShapeDtypeStruct((t, D_MODEL), F32),
        compiler_params=_params(1),
        name="out_proj",
    )(y1, y2, ym, x2, w)


def _rope_angles(pos, dim):
    inv = ROPE_THETA ** (-jnp.arange(0, dim, 2, dtype=F32) / dim)
    return pos.astype(F32)[:, None] * inv[None, :]


def _rope_tables(seq):
    pos = jnp.arange(seq)
    a1 = _rope_angles(pos, HEAD)
    cos1 = jnp.concatenate([jnp.cos(a1), jnp.cos(a1)], axis=-1)
    sin1 = jnp.concatenate([-jnp.sin(a1), jnp.sin(a1)], axis=-1)
    ar = _rope_angles(pos // GRID_W, HEAD // 2)
    ac = _rope_angles(pos % GRID_W, HEAD // 2)
    cosa = jnp.concatenate([jnp.cos(ar), jnp.cos(ar), jnp.cos(ac), jnp.cos(ac)], axis=-1)
    sina = jnp.concatenate([-jnp.sin(ar), jnp.sin(ar), -jnp.sin(ac), jnp.sin(ac)], axis=-1)
    rep = lambda a: jnp.tile(a, (1, LANES // HEAD))
    return rep(cos1), rep(sin1), rep(cosa), rep(sina)


def _scores_bounded(qk_g):
    bound = HEAD * Q_SCALE * 1.01 * jnp.max(jnp.abs(qk_g[0])) * jnp.max(jnp.abs(qk_g[1]))
    return bound <= SAFE_SCORE_BOUND


def _tile_gain(g, width):
    return jnp.tile(g.astype(F32), width // HEAD).reshape(1, width)


def _trunk(x, mem, p):
    batch, seq, _ = x.shape
    x2 = x.reshape(batch * seq, D_MODEL)
    cos1, sin1, cosa, sina = _rope_tables(seq)
    bd = p["bd"]
    gk_mem = jnp.stack([_tile_gain(p["mem_qk_g"][l, 1], 256) for l in range(DEPTH)])
    mk_all, mvt_all = _memkv(mem, p["mem_norm_g"], p["w_mem_kv"], gk_mem, bd)
    for l in range(DEPTH):
        g_in, w_in = p["norm_g"][l].reshape(1, D_MODEL), p["w_in"][l]
        gmq = _tile_gain(p["mem_qk_g"][l, 0], 256)
        mk, mvt = mk_all[l], mvt_all[l]
        if l % 2 == 0:
            e = l // 2
            y1, qz, k, vt, mqz, gate = _proj_even(
                x2, seq, g_in, w_in, cos1, sin1, _tile_gain(p["swa_qk_g"][e, 0], 512),
                _tile_gain(p["swa_qk_g"][e, 1], LANES), gmq, p["conv_w"][e], bd)
            sink_ok = jnp.max(jnp.abs(p["swa_sink"][e])) * LOG2E <= SAFE_SCORE_BOUND
            y2, ym = lax.cond(
                _scores_bounded(p["swa_qk_g"][e]) & _scores_bounded(p["mem_qk_g"][l]) & sink_ok,
                functools.partial(_window_attention, batch=batch, seq=seq, bounded=True),
                functools.partial(_window_attention, batch=batch, seq=seq, bounded=False),
                p["swa_sink"][e], qz, k, vt, mqz, mk, mvt, gate)
        else:
            o = l // 2
            qz, k, vt, dqz, dk, dvt, mqz, gate = _proj_odd(
                x2, seq, g_in, w_in, cos1, sin1, cosa, sina, _tile_gain(p["ax_qk_g"][o, 0], 512),
                _tile_gain(p["ax_qk_g"][o, 1], LANES), _tile_gain(p["diff_qk_g"][o, 0], 512),
                _tile_gain(p["diff_qk_g"][o, 1], 512), gmq, bd)
            y1, ym = lax.cond(
                _scores_bounded(p["ax_qk_g"][o]) & _scores_bounded(p["mem_qk_g"][l]),
                functools.partial(_axial_attention, batch=batch, seq=seq, bounded=True),
                functools.partial(_axial_attention, batch=batch, seq=seq, bounded=False),
                qz, k, vt, mqz, mk, mvt, gate)
            y2 = lax.cond(
                _scores_bounded(p["diff_qk_g"][o]),
                functools.partial(_diff_attention, batch=batch, seq=seq, layer=l, bounded=True),
                functools.partial(_diff_attention, batch=batch, seq=seq, layer=l, bounded=False),
                dqz, dk, dvt, p["diff_lambda"][o], p["diff_subln_g"][o], gate)
        x2 = _out_proj(y1, y2, ym, x2, p["w_out"][l])
    return x2.reshape(batch, seq, D_MODEL)


def kernel(x_prompt, x_sample, mem_prompt, mem_sample, norm_g, w_in, w_out, mem_norm_g, w_mem_kv,
           mem_qk_g, conv_w, swa_qk_g, swa_sink, ax_qk_g, diff_qk_g, diff_lambda, diff_subln_g):
    grp = jnp.arange(2 * LANES) // HEAD
    p = dict(
        norm_g=norm_g, w_in=w_in.astype(BF16), w_out=w_out.astype(BF16), mem_norm_g=mem_norm_g,
        w_mem_kv=w_mem_kv.astype(BF16), mem_qk_g=mem_qk_g, conv_w=conv_w, swa_qk_g=swa_qk_g,
        swa_sink=swa_sink, ax_qk_g=ax_qk_g, diff_qk_g=diff_qk_g, diff_lambda=diff_lambda,
        diff_subln_g=diff_subln_g, bd=(grp[:, None] == grp[None, :]).astype(BF16))
    return (_trunk(x_prompt, mem_prompt, p), _trunk(x_sample, mem_sample, p))
```

```python
import functools
import math

import jax
import jax.numpy as jnp
from jax import lax
from jax.experimental import pallas as pl
from jax.experimental.pallas import tpu as pltpu

F32 = jnp.float32
BF16 = jnp.bfloat16

D_MODEL = 1024
DEPTH = 4
HEAD = 64
LANES = 128
SUBLANES = 8
N_Q_HEADS = 8
GQA_GROUP = 4
DIFF_HEADS = 4
MEM_HEADS = 4
CONV_W = 512
WINDOW = 128
GRID_W = 64
ROPE_THETA = 10000.0
EPS = 1e-6
NEG = -1e30
LOG2E = 1.4426950408889634
Q_SCALE = HEAD ** -0.5 * LOG2E
SAFE_SCORE_BOUND = 60.0

IN_W = 3840
MIX_W = 1280
GATE_OFF = 2560
MQ_OFF = 2304

ROW_TILE = 512
Q_TILE = 256
STREAM_Q_TILE = 512
KV_CHUNK = 256
LOOP_GROUP = 32
KV_SKEW = 3
VMEM_LIMIT = 56 * 1024 * 1024

_NT = (((1,), (1,)), ((), ()))


def _params(n_axes):
    return pltpu.CompilerParams(dimension_semantics=("arbitrary",) * n_axes, vmem_limit_bytes=VMEM_LIMIT)


def _dot(a, b):
    return jnp.dot(a, b, preferred_element_type=F32)


def _dot_nt(a, b):
    return lax.dot_general(a, b, _NT, preferred_element_type=F32)


def _rms_rows(x, g):
    ms = jnp.mean(x * x, axis=-1, keepdims=True)
    return x * lax.rsqrt(ms + EPS) * g


def _head_sumsq(x, bd):
    w = x.shape[1]
    outs = []
    step = 2 * LANES if w % (2 * LANES) == 0 else LANES
    for c in range(0, w, step):
        sq = x[:, c:c + step]
        outs.append(_dot((sq * sq).astype(BF16), bd[:step, :step]))
    return outs[0] if len(outs) == 1 else jnp.concatenate(outs, axis=1)


def _head_rms(x, g, bd):
    return x * lax.rsqrt(_head_sumsq(x, bd) * (1.0 / HEAD) + EPS) * g


def _rope(x, cos, sin_signed, half):
    r = x.shape[0]
    lane = lax.broadcasted_iota(jnp.int32, (r, LANES), 1)
    first = (lane & (2 * half - 1)) < half
    outs = []
    for c in range(0, x.shape[1], LANES):
        xc = x[:, c:c + LANES]
        partner = jnp.where(first, pltpu.roll(xc, LANES - half, 1), pltpu.roll(xc, half, 1))
        outs.append(xc * cos + partner * sin_signed)
    return outs[0] if len(outs) == 1 else jnp.concatenate(outs, axis=1)


def _place_heads(x, target_half, out_ref):
    r = x.shape[0]
    low = lax.broadcasted_iota(jnp.int32, (r, LANES), 1) < HEAD
    for h in range(x.shape[1] // HEAD):
        xc = x[:, (h // 2) * LANES:(h // 2 + 1) * LANES]
        th = target_half(h)
        if th != h % 2:
            xc = pltpu.roll(xc, HEAD, 1)
        keep = low if th == 0 else jnp.logical_not(low)
        out_ref[h] = jnp.where(keep, xc, 0.0).astype(BF16)


def _silu(z):
    return z * (1.0 / (1.0 + jnp.exp(-z)))


def _project(h, w_ref, splits):
    return [_dot(h, w_ref[:, a:b]) for a, b in splits]


def _memkv_kernel(mem_ref, g_ref, w_ref, gk_ref, bd_ref, mk_ref, mvt_ref):
    h = _rms_rows(mem_ref[0], g_ref[0]).astype(BF16)
    mkv = _dot(h, w_ref[0])
    half = mkv.shape[1] // 2
    mk = _head_rms(mkv[:, :half], gk_ref[0], bd_ref[...])
    mk_ref[0, 0] = mk.astype(BF16)
    mvt_ref[0, 0] = mkv[:, half:].T.astype(BF16)


def _memkv(mem, mem_norm_g, w_mem_kv, gk_tiled, bd):
    b, n_mem, _ = mem.shape
    width = MEM_HEADS * HEAD
    return pl.pallas_call(
        _memkv_kernel,
        grid=(DEPTH, b),
        in_specs=[
            pl.BlockSpec((1, n_mem, D_MODEL), lambda l, i: (i, 0, 0)),
            pl.BlockSpec((1, 1, D_MODEL), lambda l, i: (l, 0, 0)),
            pl.BlockSpec((1, D_MODEL, 2 * width), lambda l, i: (l, 0, 0)),
            pl.BlockSpec((1, 1, width), lambda l, i: (l, 0, 0)),
            pl.BlockSpec((2 * LANES, 2 * LANES), lambda l, i: (0, 0)),
        ],
        out_specs=[
            pl.BlockSpec((1, 1, n_mem, width), lambda l, i: (l, i, 0, 0)),
            pl.BlockSpec((1, 1, width, n_mem), lambda l, i: (l, i, 0, 0)),
        ],
        out_shape=[
            jax.ShapeDtypeStruct((DEPTH, b, n_mem, width), BF16),
            jax.ShapeDtypeStruct((DEPTH, b, width, n_mem), BF16),
        ],
        compiler_params=_params(2),
        name="mem_kv",
    )(mem, mem_norm_g.reshape(DEPTH, 1, D_MODEL), w_mem_kv, gk_tiled, bd)


def _proj_even_kernel(x_ref, xp_ref, xn_ref, g_ref, w_ref, cos_ref, sin_ref, gq_ref, gk_ref,
                      gmq_ref, cw_ref, bd_ref,
                      y1_ref, qz_ref, k_ref, vt_ref, mqz_ref, gate_ref, conv_sc, *, tiles_per_seq):
    i = pl.program_id(0)
    pos_tile = i % tiles_per_seq
    bd = bd_ref[...]
    cos = cos_ref[...]
    sin = sin_ref[...]
    rows = x_ref.shape[0]

    x_ext = jnp.concatenate([x_ref[...], xp_ref[...], xn_ref[...]], axis=0)
    h_ext = _rms_rows(x_ext, g_ref[...]).astype(BF16)
    h = h_ext[:rows]
    proj = lambda a, b: _dot(h, w_ref[:, a:b])

    z = proj(GATE_OFF, IN_W)
    gchc = _dot(h_ext, w_ref[:, 512:1536])
    gate_ref[...] = _silu(z[:, 512:MIX_W])
    gb = proj(0, 512)
    q = proj(1536, 2048)

    inner_ext = gchc[:, 0:512] * gchc[:, 512:1024]
    inner = inner_ext[:rows]
    prev_row = inner_ext[rows + SUBLANES - 1:rows + SUBLANES]
    next_row = inner_ext[rows + SUBLANES:rows + SUBLANES + 1]
    prev_row = jnp.where(pos_tile == 0, 0.0, prev_row)
    next_row = jnp.where(pos_tile == tiles_per_seq - 1, 0.0, next_row)
    conv_sc[pl.ds(SUBLANES, rows), :] = inner
    conv_sc[pl.ds(SUBLANES - 1, 1), :] = prev_row
    conv_sc[pl.ds(SUBLANES + rows, 1), :] = next_row
    cw = cw_ref[...]
    conv = (conv_sc[pl.ds(SUBLANES - 1, rows), :] * cw[0:1, :] + inner * cw[1:2, :]
            + conv_sc[pl.ds(SUBLANES + 1, rows), :] * cw[2:3, :])
    y1_ref[...] = (gb * conv * _silu(z[:, 0:512])).astype(BF16)

    mq = proj(MQ_OFF, GATE_OFF)
    q = _rope(_head_rms(q, gq_ref[...], bd), cos, sin, HEAD // 2) * Q_SCALE
    _place_heads(q, lambda h: h // GQA_GROUP, qz_ref)

    kv = proj(2048, 2304)
    mq = _head_rms(mq, gmq_ref[...], bd) * Q_SCALE
    _place_heads(mq, lambda h: h % 2, mqz_ref)

    k = _rope(_head_rms(kv[:, 0:LANES], gk_ref[...], bd), cos, sin, HEAD // 2)
    k_ref[...] = k.astype(BF16)
    vt_ref[0] = kv[:, LANES:2 * LANES].T.astype(BF16)


def _proj_even(x2, seq, g, w, cos, sin, gq, gk, gmq, conv_w, bd):
    t = x2.shape[0]
    n_tiles = t // ROW_TILE
    tiles_per_seq = seq // ROW_TILE
    sub_per_tile = ROW_TILE // SUBLANES
    last_sub = t // SUBLANES - 1
    row = lambda w: pl.BlockSpec((1, w), lambda i: (0, 0))
    tab = pl.BlockSpec((ROW_TILE, LANES), lambda i: (i % tiles_per_seq, 0))
    return pl.pallas_call(
        functools.partial(_proj_even_kernel, tiles_per_seq=tiles_per_seq),
        grid=(n_tiles,),
        in_specs=[
            pl.BlockSpec((ROW_TILE, D_MODEL), lambda i: (i, 0)),
            pl.BlockSpec((SUBLANES, D_MODEL), lambda i: (jnp.maximum(i * sub_per_tile - 1, 0), 0)),
            pl.BlockSpec((SUBLANES, D_MODEL), lambda i: (jnp.minimum((i + 1) * sub_per_tile, last_sub), 0)),
            row(D_MODEL),
            pl.BlockSpec((D_MODEL, IN_W), lambda i: (0, 0)),
            tab, tab, row(512), row(LANES), row(256),
            pl.BlockSpec((3, CONV_W), lambda i: (0, 0)),
            pl.BlockSpec((2 * LANES, 2 * LANES), lambda i: (0, 0)),
        ],
        out_specs=[
            pl.BlockSpec((ROW_TILE, CONV_W), lambda i: (i, 0)),
            pl.BlockSpec((N_Q_HEADS, ROW_TILE, LANES), lambda i: (0, i, 0)),
            pl.BlockSpec((ROW_TILE, LANES), lambda i: (i, 0)),
            pl.BlockSpec((1, LANES, ROW_TILE), lambda i: (i, 0, 0)),
            pl.BlockSpec((MEM_HEADS, ROW_TILE, LANES), lambda i: (0, i, 0)),
            pl.BlockSpec((ROW_TILE, 768), lambda i: (i, 0)),
        ],
        out_shape=[
            jax.ShapeDtypeStruct((t, CONV_W), BF16),
            jax.ShapeDtypeStruct((N_Q_HEADS, t, LANES), BF16),
            jax.ShapeDtypeStruct((t, LANES), BF16),
            jax.ShapeDtypeStruct((n_tiles, LANES, ROW_TILE), BF16),
            jax.ShapeDtypeStruct((MEM_HEADS, t, LANES), BF16),
            jax.ShapeDtypeStruct((t, 768), F32),
        ],
        scratch_shapes=[pltpu.VMEM((ROW_TILE + 2 * SUBLANES, CONV_W), F32)],
        compiler_params=_params(1),
        name="proj_even",
    )(x2, x2, x2, g, w, cos, sin, gq, gk, gmq, conv_w, bd)


def _mem_attention_t(mqz_ref, mk_ref, mvt_ref, bounded=False):
    outs = []
    if bounded:
        tq = mqz_ref.shape[1]
        pairs = []
        for grp in range(MEM_HEADS // 2):
            q2 = jnp.concatenate([mqz_ref[2 * grp], mqz_ref[2 * grp + 1]], axis=0)
            pairs.append(jnp.exp2(_dot_nt(mk_ref[0, :, grp * LANES:(grp + 1) * LANES], q2)))
        for h in range(MEM_HEADS):
            p = pairs[h // 2][:, (h % 2) * tq:(h % 2 + 1) * tq]
            l = jnp.sum(p, axis=0, keepdims=True)
            o = _dot(mvt_ref[0, h * HEAD:(h + 1) * HEAD, :], p.astype(BF16))
            outs.append(o * (1.0 / l))
        return jnp.concatenate(outs, axis=0)
    for h in range(MEM_HEADS):
        grp = h // 2
        s = _dot_nt(mk_ref[0, :, grp * LANES:(grp + 1) * LANES], mqz_ref[h])
        m = jnp.max(s, axis=0, keepdims=True)
        p = jnp.exp2(s - m)
        l = jnp.sum(p, axis=0, keepdims=True)
        o = _dot(mvt_ref[0, h * HEAD:(h + 1) * HEAD, :], p.astype(BF16))
        outs.append(o * (1.0 / l))
    return jnp.concatenate(outs, axis=0)


def _window_kernel(sink_ref, qz_ref, kp_ref, kc_ref, kn_ref, vp_ref, vc_ref, vn_ref, mqz_ref, mk_ref,
                   mvt_ref, g2_ref, gm_ref, y2_ref, ym_ref, *, seq):
    tq = qz_ref.shape[1]
    t0 = pl.program_id(1) * tq
    kband = jnp.concatenate([kp_ref[...], kc_ref[...], kn_ref[...]], axis=0)
    vband = jnp.concatenate([vp_ref[0], vc_ref[0], vn_ref[0]], axis=1)
    nk = tq + 2 * WINDOW
    j = lax.broadcasted_iota(jnp.int32, (nk, tq), 0)
    i = lax.broadcasted_iota(jnp.int32, (nk, tq), 1)
    kpos = j + (t0 - WINDOW)
    mask = (j >= i) & (j <= i + 2 * WINDOW) & (kpos >= 0) & (kpos < seq)
    outs = []
    for g in range(N_Q_HEADS):
        kv = g // GQA_GROUP
        s = jnp.where(mask, _dot_nt(kband, qz_ref[g]), NEG)
        sink = sink_ref[g] * LOG2E
        m = jnp.maximum(jnp.max(s, axis=0, keepdims=True), sink)
        p = jnp.exp2(s - m)
        l = jnp.sum(p, axis=0, keepdims=True) + jnp.exp2(sink - m)
        o = _dot(vband[kv * HEAD:(kv + 1) * HEAD, :], p.astype(BF16))
        outs.append(o * (1.0 / l))
    y2 = jnp.concatenate(outs, axis=0).T
    y2_ref[...] = (y2 * g2_ref[...]).astype(BF16)
    ym = _mem_attention_t(mqz_ref, mk_ref, mvt_ref).T
    ym_ref[...] = (ym * gm_ref[...]).astype(BF16)


def _window_stream_kernel(sink_ref, qz_ref, kp_ref, kc_ref, kn_ref, vp_ref, vc_ref, vn_ref, mqz_ref,
                          mk_ref, mvt_ref, g2_ref, gm_ref, y2_ref, ym_ref, *, seq):
    tq = qz_ref.shape[1]
    t0 = pl.program_id(1) * tq
    kband = jnp.concatenate([kp_ref[...], kc_ref[...], kn_ref[...]], axis=0)
    vband = jnp.concatenate([vp_ref[0], vc_ref[0], vn_ref[0]], axis=1)
    nk = 3 * WINDOW
    j = lax.broadcasted_iota(jnp.int32, (nk, WINDOW), 0)
    i = lax.broadcasted_iota(jnp.int32, (nk, WINDOW), 1)
    in_band = (j >= i) & (j <= i + 2 * WINDOW)
    sink_p = [jnp.exp2(jnp.full((1, WINDOW), sink_ref[g] * LOG2E, F32)) for g in range(N_Q_HEADS)]
    scores = []
    for a in range(tq // WINDOW):
        for kv in range(N_Q_HEADS // GQA_GROUP):
            q4 = jnp.concatenate([qz_ref[kv * GQA_GROUP + r, a * WINDOW:(a + 1) * WINDOW, :]
                                  for r in range(GQA_GROUP)], axis=0)
            scores.append(_dot_nt(kband[a * WINDOW:a * WINDOW + nk], q4))
    blocks = [[None] * (tq // WINDOW) for _ in range(N_Q_HEADS)]
    for a in range(tq // WINDOW):
        kpos = j + (t0 + (a - 1) * WINDOW)
        bias1 = jnp.where(in_band & (kpos >= 0) & (kpos < seq), 0.0, NEG)
        bias = jnp.concatenate([bias1] * GQA_GROUP, axis=1)
        for kv in range(N_Q_HEADS // GQA_GROUP):
            p = jnp.exp2(scores[a * (N_Q_HEADS // GQA_GROUP) + kv] + bias)
            sinks = jnp.concatenate(sink_p[kv * GQA_GROUP:(kv + 1) * GQA_GROUP], axis=1)
            l = jnp.sum(p, axis=0, keepdims=True) + sinks
            o = _dot(vband[kv * HEAD:(kv + 1) * HEAD, a * WINDOW:a * WINDOW + nk], p.astype(BF16))
            o = o * (1.0 / l)
            for r in range(GQA_GROUP):
                blocks[kv * GQA_GROUP + r][a] = o[:, r * WINDOW:(r + 1) * WINDOW]
    y2 = jnp.concatenate([jnp.concatenate(b, axis=1) for b in blocks], axis=0).T
    y2_ref[...] = (y2 * g2_ref[...]).astype(BF16)
    ym = _mem_attention_t(mqz_ref, mk_ref, mvt_ref, bounded=True).T
    ym_ref[...] = (ym * gm_ref[...]).astype(BF16)


def _window_attention(sink, qz, k, vt, mqz, mk, mvt, gate, batch, seq, bounded):
    t = k.shape[0]
    tq = STREAM_Q_TILE if bounded else Q_TILE
    nq = seq // tq
    w_per_q = tq // WINDOW
    w_per_seq = seq // WINDOW
    q_per_chunk = ROW_TILE // tq
    w_per_chunk = ROW_TILE // WINDOW
    n_mem = mk.shape[1]

    def prev_w(b, q):
        return b * w_per_seq + jnp.maximum(q * w_per_q - 1, 0)

    def next_w(b, q):
        return b * w_per_seq + jnp.minimum((q + 1) * w_per_q, w_per_seq - 1)

    tok = lambda b, q: b * nq + q
    return pl.pallas_call(
        functools.partial(_window_stream_kernel if bounded else _window_kernel, seq=seq),
        grid=(batch, nq),
        in_specs=[
            pl.BlockSpec(memory_space=pltpu.SMEM),
            pl.BlockSpec((N_Q_HEADS, tq, LANES), lambda b, q: (0, tok(b, q), 0)),
            pl.BlockSpec((WINDOW, LANES), lambda b, q: (prev_w(b, q), 0)),
            pl.BlockSpec((tq, LANES), lambda b, q: (tok(b, q), 0)),
            pl.BlockSpec((WINDOW, LANES), lambda b, q: (next_w(b, q), 0)),
            pl.BlockSpec((1, LANES, WINDOW), lambda b, q: (prev_w(b, q) // w_per_chunk, 0, prev_w(b, q) % w_per_chunk)),
            pl.BlockSpec((1, LANES, tq), lambda b, q: (tok(b, q) // q_per_chunk, 0, tok(b, q) % q_per_chunk)),
            pl.BlockSpec((1, LANES, WINDOW), lambda b, q: (next_w(b, q) // w_per_chunk, 0, next_w(b, q) % w_per_chunk)),
            pl.BlockSpec((MEM_HEADS, tq, LANES), lambda b, q: (0, tok(b, q), 0)),
            pl.BlockSpec((1, n_mem, MEM_HEADS * HEAD), lambda b, q: (b, 0, 0)),
            pl.BlockSpec((1, MEM_HEADS * HEAD, n_mem), lambda b, q: (b, 0, 0)),
            pl.BlockSpec((tq, 512), lambda b, q: (tok(b, q), 0)),
            pl.BlockSpec((tq, 256), lambda b, q: (tok(b, q), 2)),
        ],
        out_specs=[
            pl.BlockSpec((tq, 512), lambda b, q: (tok(b, q), 0)),
            pl.BlockSpec((tq, 256), lambda b, q: (tok(b, q), 0)),
        ],
        out_shape=[
            jax.ShapeDtypeStruct((t, 512), BF16),
            jax.ShapeDtypeStruct((t, 256), BF16),
        ],
        compiler_params=_params(2),
        name="window_attention" if bounded else "window_attention_max",
    )(sink, qz, k, k, k, vt, vt, vt, mqz, mk, mvt, gate, gate)


def _proj_odd_kernel(x_ref, g_ref, w_ref, cos1_ref, sin1_ref, cosa_ref, sina_ref, gq_ref, gk_ref,
                     gdq_ref, gdk_ref, gmq_ref, bd_ref,
                     qz_ref, k_ref, vt_ref, dqz_ref, dk_ref, dvt_ref, mqz_ref, gate_ref):
    bd = bd_ref[...]
    cos1, sin1 = cos1_ref[...], sin1_ref[...]
    cosa, sina = cosa_ref[...], sina_ref[...]
    h = _rms_rows(x_ref[...], g_ref[...]).astype(BF16)
    proj = lambda a, b: _dot(h, w_ref[:, a:b])

    z = proj(GATE_OFF, IN_W)
    q = proj(0, 512)
    gate_ref[...] = _silu(z)

    dq = proj(768, 1280)
    q = _rope(_head_rms(q, gq_ref[...], bd), cosa, sina, HEAD // 4) * Q_SCALE
    _place_heads(q, lambda h: h // GQA_GROUP, qz_ref)

    dk = proj(1280, 1792)
    dq = _rope(_head_rms(dq, gdq_ref[...], bd), cos1, sin1, HEAD // 2) * Q_SCALE
    _place_heads(dq, lambda h: h % 2, dqz_ref)

    dv = proj(1792, 2304)
    dk = _rope(_head_rms(dk, gdk_ref[...], bd), cos1, sin1, HEAD // 2)
    dk_ref[...] = dk.astype(BF16)

    mq = proj(MQ_OFF, GATE_OFF)
    dvt_ref[0] = dv.T.astype(BF16)

    kv = proj(512, 768)
    mq = _head_rms(mq, gmq_ref[...], bd) * Q_SCALE
    _place_heads(mq, lambda h: h % 2, mqz_ref)

    k = _rope(_head_rms(kv[:, 0:LANES], gk_ref[...], bd), cosa, sina, HEAD // 4)
    k_ref[...] = k.astype(BF16)
    vt_ref[0] = kv[:, LANES:2 * LANES].T.astype(BF16)


def _proj_odd(x2, seq, g, w, cos1, sin1, cosa, sina, gq, gk, gdq, gdk, gmq, bd):
    t = x2.shape[0]
    n_tiles = t // ROW_TILE
    tiles_per_seq = seq // ROW_TILE
    row = lambda w: pl.BlockSpec((1, w), lambda i: (0, 0))
    tab = pl.BlockSpec((ROW_TILE, LANES), lambda i: (i % tiles_per_seq, 0))
    return pl.pallas_call(
        _proj_odd_kernel,
        grid=(n_tiles,),
        in_specs=[
            pl.BlockSpec((ROW_TILE, D_MODEL), lambda i: (i, 0)),
            row(D_MODEL),
            pl.BlockSpec((D_MODEL, IN_W), lambda i: (0, 0)),
            tab, tab, tab, tab, row(512), row(LANES), row(512), row(512), row(256),
            pl.BlockSpec((2 * LANES, 2 * LANES), lambda i: (0, 0)),
        ],
        out_specs=[
            pl.BlockSpec((N_Q_HEADS, ROW_TILE, LANES), lambda i: (0, i, 0)),
            pl.BlockSpec((ROW_TILE, LANES), lambda i: (i, 0)),
            pl.BlockSpec((1, LANES, ROW_TILE), lambda i: (i, 0, 0)),
            pl.BlockSpec((N_Q_HEADS, ROW_TILE, LANES), lambda i: (0, i, 0)),
            pl.BlockSpec((ROW_TILE, 512), lambda i: (i, 0)),
            pl.BlockSpec((1, 512, ROW_TILE), lambda i: (i, 0, 0)),
            pl.BlockSpec((MEM_HEADS, ROW_TILE, LANES), lambda i: (0, i, 0)),
            pl.BlockSpec((ROW_TILE, MIX_W), lambda i: (i, 0)),
        ],
        out_shape=[
            jax.ShapeDtypeStruct((N_Q_HEADS, t, LANES), BF16),
            jax.ShapeDtypeStruct((t, LANES), BF16),
            jax.ShapeDtypeStruct((n_tiles, LANES, ROW_TILE), BF16),
            jax.ShapeDtypeStruct((N_Q_HEADS, t, LANES), BF16),
            jax.ShapeDtypeStruct((t, 512), BF16),
            jax.ShapeDtypeStruct((n_tiles, 512, ROW_TILE), BF16),
            jax.ShapeDtypeStruct((MEM_HEADS, t, LANES), BF16),
            jax.ShapeDtypeStruct((t, MIX_W), F32),
        ],
        compiler_params=_params(1),
        name="proj_odd",
    )(x2, g, w, cos1, sin1, cosa, sina, gq, gk, gdq, gdk, gmq, bd)


def _flash_t(qz_ref, k_ref, vt_ref, v_rows, m_sc, l_sc, acc_sc):
    n_maps = qz_ref.shape[0]
    n_chunks, _, chunk = vt_ref.shape
    m_sc[...] = jnp.full(m_sc.shape, NEG, F32)
    l_sc[...] = jnp.zeros(l_sc.shape, F32)
    acc_sc[...] = jnp.zeros(acc_sc.shape, F32)

    def body(c, carry):
        kc = k_ref[pl.ds(pl.multiple_of(c * chunk, chunk), chunk), :]
        vc = vt_ref[c]
        for g in range(n_maps):
            r0, nr = v_rows(g)
            s = _dot_nt(kc, qz_ref[g])
            m_old = m_sc[g]
            m_new = jnp.maximum(m_old, jnp.max(s, axis=0, keepdims=True))
            alpha = jnp.exp2(m_old - m_new)
            p = jnp.exp2(s - m_new)
            l_sc[g] = alpha * l_sc[g] + jnp.sum(p, axis=0, keepdims=True)
            acc_sc[g] = alpha * acc_sc[g] + _dot(vc[r0:r0 + nr, :], p.astype(BF16))
            m_sc[g] = m_new
        return carry

    lax.fori_loop(0, n_chunks, body, 0)


def _stream_t(qz_ref, k_ref, vt_ref, v_rows, acc_sc, *, group, ahead, rolled):
    n_maps, tq, _ = qz_ref.shape
    n_tiles, _, tile = vt_ref.shape
    per_tile = tile // KV_CHUNK
    n_chunks = n_tiles * per_tile
    n_groups = n_chunks // group
    assert n_chunks % group == 0

    def one_map(g, carry):
        qt = qz_ref[g].astype(F32).T.astype(BF16)
        r0, nr = v_rows(g)

        def scores(j):
            return [_dot(k_ref[c * KV_CHUNK:(c + 1) * KV_CHUNK, :], qt)
                    for c in range(j * group, (j + 1) * group)]

        pending = [scores(j) for j in range(min(ahead, n_groups))]
        l8 = jnp.zeros((SUBLANES, tq), F32)
        pv = None
        for j in range(n_groups):
            if j + ahead < n_groups:
                pending.append(scores(j + ahead))
            for u, s in enumerate(pending.pop(0)):
                c = j * group + u
                off = (c % per_tile) * KV_CHUNK
                p = jnp.exp2(s)
                l8 = l8 + jnp.sum(p.reshape(KV_CHUNK // SUBLANES, SUBLANES, tq), axis=0)
                d = _dot(vt_ref[c // per_tile, pl.ds(r0, nr), off:off + KV_CHUNK], p.astype(BF16))
                pv = d if pv is None else pv + d
        acc_sc[g] = pv * (1.0 / jnp.sum(l8, axis=0, keepdims=True))
        return carry

    if rolled:
        lax.fori_loop(0, n_maps, one_map, 0)
    else:
        for g in range(n_maps):
            one_map(g, 0)
    return [acc_sc[g] for g in range(n_maps)]


def _stream_loop_t(qz_ref, k_ref, vt_ref, nr, acc_sc):
    n_maps, tq, _ = qz_ref.shape
    n_tiles, _, tile = vt_ref.shape
    per_tile = tile // KV_CHUNK
    group = math.gcd(LOOP_GROUP, n_tiles * per_tile)
    assert group % per_tile == 0
    outs = []
    for g in range(n_maps):
        qg = qz_ref[g]
        acc_sc[g] = jnp.zeros((nr, tq), F32)

        def body(i, l8, g=g, qg=qg):
            where = [(i * (group // per_tile) + u // per_tile, (u % per_tile) * KV_CHUNK)
                     for u in range(group)]
            scores = []
            for t, off in where:
                kc = k_ref[pl.ds(pl.multiple_of(t * tile + off, KV_CHUNK), KV_CHUNK), :]
                scores.append(_dot_nt(kc, qg))
            pv = None
            for (t, off), s in zip(where, scores):
                p = jnp.exp2(s)
                l8 = l8 + jnp.sum(p.reshape(KV_CHUNK // SUBLANES, SUBLANES, tq), axis=0)
                d = _dot(vt_ref[t, 0:nr, off:off + KV_CHUNK], p.astype(BF16))
                pv = d if pv is None else pv + d
            acc_sc[g] += pv
            return l8

        l8 = lax.fori_loop(0, n_tiles * per_tile // group, body, jnp.zeros((SUBLANES, tq), F32))
        outs.append(acc_sc[g] * (1.0 / jnp.sum(l8, axis=0, keepdims=True)))
    return outs


def _axial_kernel(qz_ref, k_ref, vt_ref, mqz_ref, mk_ref, mvt_ref, g1_ref, gm_ref, y1_ref, ym_ref,
                  *scratch, bounded):
    v_rows = lambda g: ((g // GQA_GROUP) * HEAD, HEAD)
    if bounded:
        outs = _stream_t(qz_ref, k_ref, vt_ref, v_rows, *scratch, group=1, ahead=KV_SKEW, rolled=False)
    else:
        m_sc, l_sc, acc_sc = scratch
        _flash_t(qz_ref, k_ref, vt_ref, v_rows, m_sc, l_sc, acc_sc)
        outs = [acc_sc[g] * (1.0 / l_sc[g]) for g in range(N_Q_HEADS)]
    y1 = jnp.concatenate(outs, axis=0).T
    y1_ref[...] = (y1 * g1_ref[...]).astype(BF16)
    ym = _mem_attention_t(mqz_ref, mk_ref, mvt_ref, bounded=bounded).T
    ym_ref[...] = (ym * gm_ref[...]).astype(BF16)


def _axial_attention(qz, k, vt, mqz, mk, mvt, gate, batch, seq, bounded):
    t = k.shape[0]
    tq = STREAM_Q_TILE if bounded else Q_TILE
    nq = seq // tq
    n_chunks = seq // ROW_TILE
    n_mem = mk.shape[1]
    tok = lambda b, q: b * nq + q
    scratch = [pltpu.VMEM((N_Q_HEADS, HEAD, tq), F32)] if bounded else [
        pltpu.VMEM((N_Q_HEADS, 1, tq), F32),
        pltpu.VMEM((N_Q_HEADS, 1, tq), F32),
        pltpu.VMEM((N_Q_HEADS, HEAD, tq), F32),
    ]
    return pl.pallas_call(
        functools.partial(_axial_kernel, bounded=bounded),
        grid=(batch, nq),
        in_specs=[
            pl.BlockSpec((N_Q_HEADS, tq, LANES), lambda b, q: (0, tok(b, q), 0)),
            pl.BlockSpec((seq, LANES), lambda b, q: (b, 0)),
            pl.BlockSpec((n_chunks, LANES, ROW_TILE), lambda b, q: (b, 0, 0)),
            pl.BlockSpec((MEM_HEADS, tq, LANES), lambda b, q: (0, tok(b, q), 0)),
            pl.BlockSpec((1, n_mem, MEM_HEADS * HEAD), lambda b, q: (b, 0, 0)),
            pl.BlockSpec((1, MEM_HEADS * HEAD, n_mem), lambda b, q: (b, 0, 0)),
            pl.BlockSpec((tq, 512), lambda b, q: (tok(b, q), 0)),
            pl.BlockSpec((tq, 256), lambda b, q: (tok(b, q), 4)),
        ],
        out_specs=[
            pl.BlockSpec((tq, 512), lambda b, q: (tok(b, q), 0)),
            pl.BlockSpec((tq, 256), lambda b, q: (tok(b, q), 0)),
        ],
        out_shape=[
            jax.ShapeDtypeStruct((t, 512), BF16),
            jax.ShapeDtypeStruct((t, 256), BF16),
        ],
        scratch_shapes=scratch,
        compiler_params=_params(2),
        name="axial_attention" if bounded else "axial_attention_online",
    )(qz, k, vt, mqz, mk, mvt, gate, gate)


def _diff_kernel(qz_ref, k_ref, vt_ref, lam_ref, sg_ref, g2_ref, y2_ref, *scratch, lambda_init, bounded):
    v_rows = lambda g: (0, 2 * HEAD)
    if bounded:
        o0, o1 = _stream_loop_t(qz_ref, k_ref, vt_ref, 2 * HEAD, *scratch)
    else:
        m_sc, l_sc, acc_sc = scratch
        _flash_t(qz_ref, k_ref, vt_ref, v_rows, m_sc, l_sc, acc_sc)
        o0, o1 = (acc_sc[g] * (1.0 / l_sc[g]) for g in range(2))
    lv = lam_ref[...]
    lam = (jnp.exp(jnp.sum(lv[0:1] * lv[1:2], axis=-1, keepdims=True))
           - jnp.exp(jnp.sum(lv[2:3] * lv[3:4], axis=-1, keepdims=True)) + lambda_init)
    o = o0 - lam * o1
    ms = jnp.mean(o * o, axis=0, keepdims=True)
    on = (o * lax.rsqrt(ms + EPS)).T
    y2_ref[...] = (on * sg_ref[...] * (1.0 - lambda_init) * g2_ref[...]).astype(BF16)


def _diff_attention(dqz, dk, dvt, lam, subln_g, gate, batch, seq, layer, bounded):
    t = dk.shape[0]
    tq = STREAM_Q_TILE if bounded else Q_TILE
    nq = seq // tq
    n_chunks = seq // ROW_TILE
    lambda_init = 0.8 - 0.6 * math.exp(-0.3 * layer)
    tok = lambda b, q: b * nq + q
    scratch = [pltpu.VMEM((2, 2 * HEAD, tq), F32)] if bounded else [
        pltpu.VMEM((2, 1, tq), F32),
        pltpu.VMEM((2, 1, tq), F32),
        pltpu.VMEM((2, 2 * HEAD, tq), F32),
    ]
    return pl.pallas_call(
        functools.partial(_diff_kernel, lambda_init=lambda_init, bounded=bounded),
        grid=(batch, DIFF_HEADS, nq),
        in_specs=[
            pl.BlockSpec((2, tq, LANES), lambda b, h, q: (h, tok(b, q), 0)),
            pl.BlockSpec((seq, LANES), lambda b, h, q: (b, h)),
            pl.BlockSpec((n_chunks, LANES, ROW_TILE), lambda b, h, q: (b, h, 0)),
            pl.BlockSpec((4, HEAD), lambda b, h, q: (0, 0)),
            pl.BlockSpec((1, 2 * HEAD), lambda b, h, q: (0, 0)),
            pl.BlockSpec((tq, LANES), lambda b, h, q: (tok(b, q), 4 + h)),
        ],
        out_specs=pl.BlockSpec((tq, LANES), lambda b, h, q: (tok(b, q), h)),
        out_shape=jax.ShapeDtypeStruct((t, 512), BF16),
        scratch_shapes=scratch,
        compiler_params=_params(3),
        name="diff_attention" if bounded else "diff_attention_online",
    )(dqz, dk, dvt, lam, subln_g.reshape(1, 2 * HEAD), gate)


def _out_proj_kernel(y1_ref, y2_ref, ym_ref, x_ref, w_ref, o_ref):
    acc = _dot(y1_ref[...], w_ref[0:512, :])
    acc = acc + _dot(y2_ref[...], w_ref[512:1024, :])
    acc = acc + _dot(ym_ref[...], w_ref[1024:MIX_W, :])
    o_ref[...] = x_ref[...] + acc


def _out_proj(y1, y2, ym, x2, w):
    t = x2.shape[0]
    return pl.pallas_call(
        _out_proj_kernel,
        grid=(t // ROW_TILE,),
        in_specs=[
            pl.BlockSpec((ROW_TILE, 512), lambda i: (i, 0)),
            pl.BlockSpec((ROW_TILE, 512), lambda i: (i, 0)),
            pl.BlockSpec((ROW_TILE, 256), lambda i: (i, 0)),
            pl.BlockSpec((ROW_TILE, D_MODEL), lambda i: (i, 0)),
            pl.BlockSpec((MIX_W, D_MODEL), lambda i: (0, 0)),
        ],
        out_specs=pl.BlockSpec((ROW_TILE, D_MODEL), lambda i: (i, 0)),
        out_shape=jax.ShapeDtypeStruct((t, D_MODEL), F32),
        compiler_params=_params(1),
        name="out_proj",
    )(y1, y2, ym, x2, w)


def _rope_angles(pos, dim):
    inv = ROPE_THETA ** (-jnp.arange(0, dim, 2, dtype=F32) / dim)
    return pos.astype(F32)[:, None] * inv[None, :]


def _rope_tables(seq):
    pos = jnp.arange(seq)
    a1 = _rope_angles(pos, HEAD)
    cos1 = jnp.concatenate([jnp.cos(a1), jnp.cos(a1)], axis=-1)
    sin1 = jnp.concatenate([-jnp.sin(a1), jnp.sin(a1)], axis=-1)
    ar = _rope_angles(pos // GRID_W, HEAD // 2)
    ac = _rope_angles(pos % GRID_W, HEAD // 2)
    cosa = jnp.concatenate([jnp.cos(ar), jnp.cos(ar), jnp.cos(ac), jnp.cos(ac)], axis=-1)
    sina = jnp.concatenate([-jnp.sin(ar), jnp.sin(ar), -jnp.sin(ac), jnp.sin(ac)], axis=-1)
    rep = lambda a: jnp.tile(a, (1, LANES // HEAD))
    return rep(cos1), rep(sin1), rep(cosa), rep(sina)


def _scores_bounded(qk_g):
    bound = HEAD * Q_SCALE * 1.01 * jnp.max(jnp.abs(qk_g[0])) * jnp.max(jnp.abs(qk_g[1]))
    return bound <= SAFE_SCORE_BOUND


def _tile_gain(g, width):
    return jnp.tile(g.astype(F32), width // HEAD).reshape(1, width)


def _trunk(x, mem, p):
    batch, seq, _ = x.shape
    x2 = x.reshape(batch * seq, D_MODEL)
    cos1, sin1, cosa, sina = _rope_tables(seq)
    bd = p["bd"]
    gk_mem = jnp.stack([_tile_gain(p["mem_qk_g"][l, 1], 256) for l in range(DEPTH)])
    mk_all, mvt_all = _memkv(mem, p["mem_norm_g"], p["w_mem_kv"], gk_mem, bd)
    for l in range(DEPTH):
        g_in, w_in = p["norm_g"][l].reshape(1, D_MODEL), p["w_in"][l]
        gmq = _tile_gain(p["mem_qk_g"][l, 0], 256)
        mk, mvt = mk_all[l], mvt_all[l]
        if l % 2 == 0:
            e = l // 2
            y1, qz, k, vt, mqz, gate = _proj_even(
                x2, seq, g_in, w_in, cos1, sin1, _tile_gain(p["swa_qk_g"][e, 0], 512),
                _tile_gain(p["swa_qk_g"][e, 1], LANES), gmq, p["conv_w"][e], bd)
            sink_ok = jnp.max(jnp.abs(p["swa_sink"][e])) * LOG2E <= SAFE_SCORE_BOUND
            y2, ym = lax.cond(
                _scores_bounded(p["swa_qk_g"][e]) & _scores_bounded(p["mem_qk_g"][l]) & sink_ok,
                functools.partial(_window_attention, batch=batch, seq=seq, bounded=True),
                functools.partial(_window_attention, batch=batch, seq=seq, bounded=False),
                p["swa_sink"][e], qz, k, vt, mqz, mk, mvt, gate)
        else:
            o = l // 2
            qz, k, vt, dqz, dk, dvt, mqz, gate = _proj_odd(
                x2, seq, g_in, w_in, cos1, sin1, cosa, sina, _tile_gain(p["ax_qk_g"][o, 0], 512),
                _tile_gain(p["ax_qk_g"][o, 1], LANES), _tile_gain(p["diff_qk_g"][o, 0], 512),
                _tile_gain(p["diff_qk_g"][o, 1], 512), gmq, bd)
            y1, ym = lax.cond(
                _scores_bounded(p["ax_qk_g"][o]) & _scores_bounded(p["mem_qk_g"][l]),
                functools.partial(_axial_attention, batch=batch, seq=seq, bounded=True),
                functools.partial(_axial_attention, batch=batch, seq=seq, bounded=False),
                qz, k, vt, mqz, mk, mvt, gate)
            y2 = lax.cond(
                _scores_bounded(p["diff_qk_g"][o]),
                functools.partial(_diff_attention, batch=batch, seq=seq, layer=l, bounded=True),
                functools.partial(_diff_attention, batch=batch, seq=seq, layer=l, bounded=False),
                dqz, dk, dvt, p["diff_lambda"][o], p["diff_subln_g"][o], gate)
        x2 = _out_proj(y1, y2, ym, x2, p["w_out"][l])
    return x2.reshape(batch, seq, D_MODEL)


def kernel(x_prompt, x_sample, mem_prompt, mem_sample, norm_g, w_in, w_out, mem_norm_g, w_mem_kv,
           mem_qk_g, conv_w, swa_qk_g, swa_sink, ax_qk_g, diff_qk_g, diff_lambda, diff_subln_g):
    grp = jnp.arange(2 * LANES) // HEAD
    p = dict(
        norm_g=norm_g, w_in=w_in.astype(BF16), w_out=w_out.astype(BF16), mem_norm_g=mem_norm_g,
        w_mem_kv=w_mem_kv.astype(BF16), mem_qk_g=mem_qk_g, conv_w=conv_w, swa_qk_g=swa_qk_g,
        swa_sink=swa_sink, ax_qk_g=ax_qk_g, diff_qk_g=diff_qk_g, diff_lambda=diff_lambda,
        diff_subln_g=diff_subln_g, bd=(grp[:, None] == grp[None, :]).astype(BF16))
    return (_trunk(x_prompt, mem_prompt, p), _trunk(x_sample, mem_sample, p))
```

```python
import functools
import math

import jax
import jax.numpy as jnp
from jax import lax
from jax.experimental import pallas as pl
from jax.experimental.pallas import tpu as pltpu

F32 = jnp.float32
BF16 = jnp.bfloat16

D_MODEL = 1024
DEPTH = 4
HEAD = 64
LANES = 128
SUBLANES = 8
N_Q_HEADS = 8
GQA_GROUP = 4
DIFF_HEADS = 4
MEM_HEADS = 4
CONV_W = 512
WINDOW = 128
GRID_W = 64
ROPE_THETA = 10000.0
EPS = 1e-6
NEG = -1e30
LOG2E = 1.4426950408889634
Q_SCALE = HEAD ** -0.5 * LOG2E
SAFE_SCORE_BOUND = 60.0

IN_W = 3840
MIX_W = 1280
GATE_OFF = 2560
MQ_OFF = 2304

ROW_TILE = 512
OUT_ROW_TILE = 1024
Q_TILE = 256
STREAM_Q_TILE = 512
KV_CHUNK = 256
LOOP_GROUP = 32
KV_SKEW = 3
VMEM_LIMIT = 56 * 1024 * 1024

_NT = (((1,), (1,)), ((), ()))


def _params(n_axes):
    return pltpu.CompilerParams(dimension_semantics=("arbitrary",) * n_axes, vmem_limit_bytes=VMEM_LIMIT)


def _dot(a, b):
    return jnp.dot(a, b, preferred_element_type=F32)


def _dot_nt(a, b):
    return lax.dot_general(a, b, _NT, preferred_element_type=F32)


def _rms_rows(x, g):
    ms = jnp.mean(x * x, axis=-1, keepdims=True)
    return x * lax.rsqrt(ms + EPS) * g


def _head_sumsq(x, bd):
    w = x.shape[1]
    outs = []
    step = 2 * LANES if w % (2 * LANES) == 0 else LANES
    for c in range(0, w, step):
        sq = x[:, c:c + step]
        outs.append(_dot((sq * sq).astype(BF16), bd[:step, :step]))
    return outs[0] if len(outs) == 1 else jnp.concatenate(outs, axis=1)


def _head_rms(x, g, bd):
    return x * lax.rsqrt(_head_sumsq(x, bd) * (1.0 / HEAD) + EPS) * g


def _rope(x, cos, sin_signed, half):
    r = x.shape[0]
    lane = lax.broadcasted_iota(jnp.int32, (r, LANES), 1)
    first = (lane & (2 * half - 1)) < half
    outs = []
    for c in range(0, x.shape[1], LANES):
        xc = x[:, c:c + LANES]
        partner = jnp.where(first, pltpu.roll(xc, LANES - half, 1), pltpu.roll(xc, half, 1))
        outs.append(xc * cos + partner * sin_signed)
    return outs[0] if len(outs) == 1 else jnp.concatenate(outs, axis=1)


def _place_heads(x, target_half, out_ref):
    r = x.shape[0]
    low = lax.broadcasted_iota(jnp.int32, (r, LANES), 1) < HEAD
    for h in range(x.shape[1] // HEAD):
        xc = x[:, (h // 2) * LANES:(h // 2 + 1) * LANES]
        th = target_half(h)
        if th != h % 2:
            xc = pltpu.roll(xc, HEAD, 1)
        keep = low if th == 0 else jnp.logical_not(low)
        out_ref[h] = jnp.where(keep, xc, 0.0).astype(BF16)


def _silu(z):
    return z * (1.0 / (1.0 + jnp.exp(-z)))


def _project(h, w_ref, splits):
    return [_dot(h, w_ref[:, a:b]) for a, b in splits]


def _memkv_kernel(mem_ref, g_ref, w_ref, gk_ref, bd_ref, mk_ref, mvt_ref):
    h = _rms_rows(mem_ref[0], g_ref[0]).astype(BF16)
    mkv = _dot(h, w_ref[0])
    half = mkv.shape[1] // 2
    mk = _head_rms(mkv[:, :half], gk_ref[0], bd_ref[...])
    mk_ref[0, 0] = mk.astype(BF16)
    mvt_ref[0, 0] = mkv[:, half:].T.astype(BF16)


def _memkv(mem, mem_norm_g, w_mem_kv, gk_tiled, bd):
    b, n_mem, _ = mem.shape
    width = MEM_HEADS * HEAD
    return pl.pallas_call(
        _memkv_kernel,
        grid=(DEPTH, b),
        in_specs=[
            pl.BlockSpec((1, n_mem, D_MODEL), lambda l, i: (i, 0, 0)),
            pl.BlockSpec((1, 1, D_MODEL), lambda l, i: (l, 0, 0)),
            pl.BlockSpec((1, D_MODEL, 2 * width), lambda l, i: (l, 0, 0)),
            pl.BlockSpec((1, 1, width), lambda l, i: (l, 0, 0)),
            pl.BlockSpec((2 * LANES, 2 * LANES), lambda l, i: (0, 0)),
        ],
        out_specs=[
            pl.BlockSpec((1, 1, n_mem, width), lambda l, i: (l, i, 0, 0)),
            pl.BlockSpec((1, 1, width, n_mem), lambda l, i: (l, i, 0, 0)),
        ],
        out_shape=[
            jax.ShapeDtypeStruct((DEPTH, b, n_mem, width), BF16),
            jax.ShapeDtypeStruct((DEPTH, b, width, n_mem), BF16),
        ],
        compiler_params=_params(2),
        name="mem_kv",
    )(mem, mem_norm_g.reshape(DEPTH, 1, D_MODEL), w_mem_kv, gk_tiled, bd)


def _proj_even_kernel(x_ref, xp_ref, xn_ref, g_ref, w_ref, cos_ref, sin_ref, gq_ref, gk_ref,
                      gmq_ref, cw_ref, bd_ref,
                      y1_ref, qz_ref, k_ref, vt_ref, mqz_ref, gate_ref, conv_sc, *, tiles_per_seq):
    i = pl.program_id(0)
    pos_tile = i % tiles_per_seq
    bd = bd_ref[...]
    cos = cos_ref[...]
    sin = sin_ref[...]
    rows = x_ref.shape[0]

    x_ext = jnp.concatenate([x_ref[...], xp_ref[...], xn_ref[...]], axis=0)
    h_ext = _rms_rows(x_ext, g_ref[...]).astype(BF16)
    h = h_ext[:rows]
    proj = lambda a, b: _dot(h, w_ref[:, a:b])

    z = proj(GATE_OFF, IN_W)
    gchc = _dot(h_ext, w_ref[:, 512:1536])
    gate_ref[...] = _silu(z[:, 512:MIX_W])
    gb = proj(0, 512)
    q = proj(1536, 2048)

    inner_ext = gchc[:, 0:512] * gchc[:, 512:1024]
    inner = inner_ext[:rows]
    prev_row = inner_ext[rows + SUBLANES - 1:rows + SUBLANES]
    next_row = inner_ext[rows + SUBLANES:rows + SUBLANES + 1]
    prev_row = jnp.where(pos_tile == 0, 0.0, prev_row)
    next_row = jnp.where(pos_tile == tiles_per_seq - 1, 0.0, next_row)
    conv_sc[pl.ds(SUBLANES, rows), :] = inner
    conv_sc[pl.ds(SUBLANES - 1, 1), :] = prev_row
    conv_sc[pl.ds(SUBLANES + rows, 1), :] = next_row
    cw = cw_ref[...]
    conv = (conv_sc[pl.ds(SUBLANES - 1, rows), :] * cw[0:1, :] + inner * cw[1:2, :]
            + conv_sc[pl.ds(SUBLANES + 1, rows), :] * cw[2:3, :])
    y1_ref[...] = (gb * conv * _silu(z[:, 0:512])).astype(BF16)

    mq = proj(MQ_OFF, GATE_OFF)
    q = _rope(_head_rms(q, gq_ref[...], bd), cos, sin, HEAD // 2) * Q_SCALE
    _place_heads(q, lambda h: h // GQA_GROUP, qz_ref)

    kv = proj(2048, 2304)
    mq = _head_rms(mq, gmq_ref[...], bd) * Q_SCALE
    _place_heads(mq, lambda h: h % 2, mqz_ref)

    k = _rope(_head_rms(kv[:, 0:LANES], gk_ref[...], bd), cos, sin, HEAD // 2)
    k_ref[...] = k.astype(BF16)
    vt_ref[0] = kv[:, LANES:2 * LANES].T.astype(BF16)


def _proj_even(x2, seq, g, w, cos, sin, gq, gk, gmq, conv_w, bd):
    t = x2.shape[0]
    n_tiles = t // ROW_TILE
    tiles_per_seq = seq // ROW_TILE
    sub_per_tile = ROW_TILE // SUBLANES
    last_sub = t // SUBLANES - 1
    row = lambda w: pl.BlockSpec((1, w), lambda i: (0, 0))
    tab = pl.BlockSpec((ROW_TILE, LANES), lambda i: (i % tiles_per_seq, 0))
    return pl.pallas_call(
        functools.partial(_proj_even_kernel, tiles_per_seq=tiles_per_seq),
        grid=(n_tiles,),
        in_specs=[
            pl.BlockSpec((ROW_TILE, D_MODEL), lambda i: (i, 0)),
            pl.BlockSpec((SUBLANES, D_MODEL), lambda i: (jnp.maximum(i * sub_per_tile - 1, 0), 0)),
            pl.BlockSpec((SUBLANES, D_MODEL), lambda i: (jnp.minimum((i + 1) * sub_per_tile, last_sub), 0)),
            row(D_MODEL),
            pl.BlockSpec((D_MODEL, IN_W), lambda i: (0, 0)),
            tab, tab, row(512), row(LANES), row(256),
            pl.BlockSpec((3, CONV_W), lambda i: (0, 0)),
            pl.BlockSpec((2 * LANES, 2 * LANES), lambda i: (0, 0)),
        ],
        out_specs=[
            pl.BlockSpec((ROW_TILE, CONV_W), lambda i: (i, 0)),
            pl.BlockSpec((N_Q_HEADS, ROW_TILE, LANES), lambda i: (0, i, 0)),
            pl.BlockSpec((ROW_TILE, LANES), lambda i: (i, 0)),
            pl.BlockSpec((1, LANES, ROW_TILE), lambda i: (i, 0, 0)),
            pl.BlockSpec((MEM_HEADS, ROW_TILE, LANES), lambda i: (0, i, 0)),
            pl.BlockSpec((ROW_TILE, 768), lambda i: (i, 0)),
        ],
        out_shape=[
            jax.ShapeDtypeStruct((t, CONV_W), BF16),
            jax.ShapeDtypeStruct((N_Q_HEADS, t, LANES), BF16),
            jax.ShapeDtypeStruct((t, LANES), BF16),
            jax.ShapeDtypeStruct((n_tiles, LANES, ROW_TILE), BF16),
            jax.ShapeDtypeStruct((MEM_HEADS, t, LANES), BF16),
            jax.ShapeDtypeStruct((t, 768), F32),
        ],
        scratch_shapes=[pltpu.VMEM((ROW_TILE + 2 * SUBLANES, CONV_W), F32)],
        compiler_params=_params(1),
        name="proj_even",
    )(x2, x2, x2, g, w, cos, sin, gq, gk, gmq, conv_w, bd)


def _mem_scores(mqz_ref, mk_ref):
    pairs = []
    for grp in range(MEM_HEADS // 2):
        q2 = jnp.concatenate([mqz_ref[2 * grp], mqz_ref[2 * grp + 1]], axis=0)
        pairs.append(_dot_nt(mk_ref[0, :, grp * LANES:(grp + 1) * LANES], q2))
    return pairs


def _mem_finish(pairs, mvt_ref):
    tq = pairs[0].shape[1] // 2
    outs = []
    for h in range(MEM_HEADS):
        p = jnp.exp2(pairs[h // 2][:, (h % 2) * tq:(h % 2 + 1) * tq])
        l = jnp.sum(p, axis=0, keepdims=True)
        o = _dot(mvt_ref[0, h * HEAD:(h + 1) * HEAD, :], p.astype(BF16))
        outs.append(o * (1.0 / l))
    return jnp.concatenate(outs, axis=0)


def _mem_attention_t(mqz_ref, mk_ref, mvt_ref):
    outs = []
    for h in range(MEM_HEADS):
        grp = h // 2
        s = _dot_nt(mk_ref[0, :, grp * LANES:(grp + 1) * LANES], mqz_ref[h])
        m = jnp.max(s, axis=0, keepdims=True)
        p = jnp.exp2(s - m)
        l = jnp.sum(p, axis=0, keepdims=True)
        o = _dot(mvt_ref[0, h * HEAD:(h + 1) * HEAD, :], p.astype(BF16))
        outs.append(o * (1.0 / l))
    return jnp.concatenate(outs, axis=0)


def _window_kernel(sink_ref, qz_ref, kp_ref, kc_ref, kn_ref, vp_ref, vc_ref, vn_ref, mqz_ref, mk_ref,
                   mvt_ref, g2_ref, gm_ref, y2_ref, ym_ref, *, seq):
    tq = qz_ref.shape[1]
    t0 = pl.program_id(1) * tq
    kband = jnp.concatenate([kp_ref[...], kc_ref[...], kn_ref[...]], axis=0)
    vband = jnp.concatenate([vp_ref[0], vc_ref[0], vn_ref[0]], axis=1)
    nk = tq + 2 * WINDOW
    j = lax.broadcasted_iota(jnp.int32, (nk, tq), 0)
    i = lax.broadcasted_iota(jnp.int32, (nk, tq), 1)
    kpos = j + (t0 - WINDOW)
    mask = (j >= i) & (j <= i + 2 * WINDOW) & (kpos >= 0) & (kpos < seq)
    outs = []
    for g in range(N_Q_HEADS):
        kv = g // GQA_GROUP
        s = jnp.where(mask, _dot_nt(kband, qz_ref[g]), NEG)
        sink = sink_ref[g] * LOG2E
        m = jnp.maximum(jnp.max(s, axis=0, keepdims=True), sink)
        p = jnp.exp2(s - m)
        l = jnp.sum(p, axis=0, keepdims=True) + jnp.exp2(sink - m)
        o = _dot(vband[kv * HEAD:(kv + 1) * HEAD, :], p.astype(BF16))
        outs.append(o * (1.0 / l))
    y2 = jnp.concatenate(outs, axis=0).T
    y2_ref[...] = (y2 * g2_ref[...]).astype(BF16)
    ym = _mem_attention_t(mqz_ref, mk_ref, mvt_ref).T
    ym_ref[...] = (ym * gm_ref[...]).astype(BF16)


def _window_stream_kernel(sink_ref, qz_ref, kp_ref, kc_ref, kn_ref, vp_ref, vc_ref, vn_ref, mqz_ref,
                          mk_ref, mvt_ref, g2_ref, gm_ref, y2_ref, ym_ref, *, seq):
    tq = qz_ref.shape[1]
    t0 = pl.program_id(1) * tq
    kband = jnp.concatenate([kp_ref[...], kc_ref[...], kn_ref[...]], axis=0)
    vband = jnp.concatenate([vp_ref[0], vc_ref[0], vn_ref[0]], axis=1)
    nk = 3 * WINDOW
    j = lax.broadcasted_iota(jnp.int32, (nk, WINDOW), 0)
    i = lax.broadcasted_iota(jnp.int32, (nk, WINDOW), 1)
    in_band = (j >= i) & (j <= i + 2 * WINDOW)
    sink_p = [jnp.exp2(jnp.full((1, WINDOW), sink_ref[g] * LOG2E, F32)) for g in range(N_Q_HEADS)]
    scores = []
    for a in range(tq // WINDOW):
        for kv in range(N_Q_HEADS // GQA_GROUP):
            q4 = jnp.concatenate([qz_ref[kv * GQA_GROUP + r, a * WINDOW:(a + 1) * WINDOW, :]
                                  for r in range(GQA_GROUP)], axis=0)
            scores.append(_dot_nt(kband[a * WINDOW:a * WINDOW + nk], q4))
    mem_pairs = _mem_scores(mqz_ref, mk_ref)
    blocks = [[None] * (tq // WINDOW) for _ in range(N_Q_HEADS)]
    for a in range(tq // WINDOW):
        kpos = j + (t0 + (a - 1) * WINDOW)
        bias1 = jnp.where(in_band & (kpos >= 0) & (kpos < seq), 0.0, NEG)
        bias = jnp.concatenate([bias1] * GQA_GROUP, axis=1)
        for kv in range(N_Q_HEADS // GQA_GROUP):
            p = jnp.exp2(scores[a * (N_Q_HEADS // GQA_GROUP) + kv] + bias)
            sinks = jnp.concatenate(sink_p[kv * GQA_GROUP:(kv + 1) * GQA_GROUP], axis=1)
            l = jnp.sum(p, axis=0, keepdims=True) + sinks
            o = _dot(vband[kv * HEAD:(kv + 1) * HEAD, a * WINDOW:a * WINDOW + nk], p.astype(BF16))
            o = o * (1.0 / l)
            for r in range(GQA_GROUP):
                blocks[kv * GQA_GROUP + r][a] = o[:, r * WINDOW:(r + 1) * WINDOW]
    y2 = jnp.concatenate([jnp.concatenate(b, axis=1) for b in blocks], axis=0).T
    y2_ref[...] = (y2 * g2_ref[...]).astype(BF16)
    ym = _mem_finish(mem_pairs, mvt_ref).T
    ym_ref[...] = (ym * gm_ref[...]).astype(BF16)


def _window_attention(sink, qz, k, vt, mqz, mk, mvt, gate, batch, seq, bounded):
    t = k.shape[0]
    tq = STREAM_Q_TILE if bounded else Q_TILE
    nq = seq // tq
    w_per_q = tq // WINDOW
    w_per_seq = seq // WINDOW
    q_per_chunk = ROW_TILE // tq
    w_per_chunk = ROW_TILE // WINDOW
    n_mem = mk.shape[1]

    def prev_w(b, q):
        return b * w_per_seq + jnp.maximum(q * w_per_q - 1, 0)

    def next_w(b, q):
        return b * w_per_seq + jnp.minimum((q + 1) * w_per_q, w_per_seq - 1)

    tok = lambda b, q: b * nq + q
    return pl.pallas_call(
        functools.partial(_window_stream_kernel if bounded else _window_kernel, seq=seq),
        grid=(batch, nq),
        in_specs=[
            pl.BlockSpec(memory_space=pltpu.SMEM),
            pl.BlockSpec((N_Q_HEADS, tq, LANES), lambda b, q: (0, tok(b, q), 0)),
            pl.BlockSpec((WINDOW, LANES), lambda b, q: (prev_w(b, q), 0)),
            pl.BlockSpec((tq, LANES), lambda b, q: (tok(b, q), 0)),
            pl.BlockSpec((WINDOW, LANES), lambda b, q: (next_w(b, q), 0)),
            pl.BlockSpec((1, LANES, WINDOW), lambda b, q: (prev_w(b, q) // w_per_chunk, 0, prev_w(b, q) % w_per_chunk)),
            pl.BlockSpec((1, LANES, tq), lambda b, q: (tok(b, q) // q_per_chunk, 0, tok(b, q) % q_per_chunk)),
            pl.BlockSpec((1, LANES, WINDOW), lambda b, q: (next_w(b, q) // w_per_chunk, 0, next_w(b, q) % w_per_chunk)),
            pl.BlockSpec((MEM_HEADS, tq, LANES), lambda b, q: (0, tok(b, q), 0)),
            pl.BlockSpec((1, n_mem, MEM_HEADS * HEAD), lambda b, q: (b, 0, 0)),
            pl.BlockSpec((1, MEM_HEADS * HEAD, n_mem), lambda b, q: (b, 0, 0)),
            pl.BlockSpec((tq, 512), lambda b, q: (tok(b, q), 0)),
            pl.BlockSpec((tq, 256), lambda b, q: (tok(b, q), 2)),
        ],
        out_specs=[
            pl.BlockSpec((tq, 512), lambda b, q: (tok(b, q), 0)),
            pl.BlockSpec((tq, 256), lambda b, q: (tok(b, q), 0)),
        ],
        out_shape=[
            jax.ShapeDtypeStruct((t, 512), BF16),
            jax.ShapeDtypeStruct((t, 256), BF16),
        ],
        compiler_params=_params(2),
        name="window_attention" if bounded else "window_attention_max",
    )(sink, qz, k, k, k, vt, vt, vt, mqz, mk, mvt, gate, gate)


def _proj_odd_kernel(x_ref, g_ref, w_ref, cos1_ref, sin1_ref, cosa_ref, sina_ref, gq_ref, gk_ref,
                     gdq_ref, gdk_ref, gmq_ref, bd_ref,
                     qz_ref, k_ref, vt_ref, dqz_ref, dk_ref, dvt_ref, mqz_ref, gate_ref):
    bd = bd_ref[...]
    cos1, sin1 = cos1_ref[...], sin1_ref[...]
    cosa, sina = cosa_ref[...], sina_ref[...]
    h = _rms_rows(x_ref[...], g_ref[...]).astype(BF16)
    proj = lambda a, b: _dot(h, w_ref[:, a:b])

    z = proj(GATE_OFF, IN_W)
    q = proj(0, 512)
    gate_ref[...] = _silu(z)

    dq = proj(768, 1280)
    q = _rope(_head_rms(q, gq_ref[...], bd), cosa, sina, HEAD // 4) * Q_SCALE
    _place_heads(q, lambda h: h // GQA_GROUP, qz_ref)

    dk = proj(1280, 1792)
    dq = _rope(_head_rms(dq, gdq_ref[...], bd), cos1, sin1, HEAD // 2) * Q_SCALE
    _place_heads(dq, lambda h: h % 2, dqz_ref)

    dv = proj(1792, 2304)
    dk = _rope(_head_rms(dk, gdk_ref[...], bd), cos1, sin1, HEAD // 2)
    dk_ref[...] = dk.astype(BF16)

    mq = proj(MQ_OFF, GATE_OFF)
    dvt_ref[0] = dv.T.astype(BF16)

    kv = proj(512, 768)
    mq = _head_rms(mq, gmq_ref[...], bd) * Q_SCALE
    _place_heads(mq, lambda h: h % 2, mqz_ref)

    k = _rope(_head_rms(kv[:, 0:LANES], gk_ref[...], bd), cosa, sina, HEAD // 4)
    k_ref[...] = k.astype(BF16)
    vt_ref[0] = kv[:, LANES:2 * LANES].T.astype(BF16)


def _proj_odd(x2, seq, g, w, cos1, sin1, cosa, sina, gq, gk, gdq, gdk, gmq, bd):
    t = x2.shape[0]
    n_tiles = t // ROW_TILE
    tiles_per_seq = seq // ROW_TILE
    row = lambda w: pl.BlockSpec((1, w), lambda i: (0, 0))
    tab = pl.BlockSpec((ROW_TILE, LANES), lambda i: (i % tiles_per_seq, 0))
    return pl.pallas_call(
        _proj_odd_kernel,
        grid=(n_tiles,),
        in_specs=[
            pl.BlockSpec((ROW_TILE, D_MODEL), lambda i: (i, 0)),
            row(D_MODEL),
            pl.BlockSpec((D_MODEL, IN_W), lambda i: (0, 0)),
            tab, tab, tab, tab, row(512), row(LANES), row(512), row(512), row(256),
            pl.BlockSpec((2 * LANES, 2 * LANES), lambda i: (0, 0)),
        ],
        out_specs=[
            pl.BlockSpec((N_Q_HEADS, ROW_TILE, LANES), lambda i: (0, i, 0)),
            pl.BlockSpec((ROW_TILE, LANES), lambda i: (i, 0)),
            pl.BlockSpec((1, LANES, ROW_TILE), lambda i: (i, 0, 0)),
            pl.BlockSpec((N_Q_HEADS, ROW_TILE, LANES), lambda i: (0, i, 0)),
            pl.BlockSpec((ROW_TILE, 512), lambda i: (i, 0)),
            pl.BlockSpec((1, 512, ROW_TILE), lambda i: (i, 0, 0)),
            pl.BlockSpec((MEM_HEADS, ROW_TILE, LANES), lambda i: (0, i, 0)),
            pl.BlockSpec((ROW_TILE, MIX_W), lambda i: (i, 0)),
        ],
        out_shape=[
            jax.ShapeDtypeStruct((N_Q_HEADS, t, LANES), BF16),
            jax.ShapeDtypeStruct((t, LANES), BF16),
            jax.ShapeDtypeStruct((n_tiles, LANES, ROW_TILE), BF16),
            jax.ShapeDtypeStruct((N_Q_HEADS, t, LANES), BF16),
            jax.ShapeDtypeStruct((t, 512), BF16),
            jax.ShapeDtypeStruct((n_tiles, 512, ROW_TILE), BF16),
            jax.ShapeDtypeStruct((MEM_HEADS, t, LANES), BF16),
            jax.ShapeDtypeStruct((t, MIX_W), F32),
        ],
        compiler_params=_params(1),
        name="proj_odd",
    )(x2, g, w, cos1, sin1, cosa, sina, gq, gk, gdq, gdk, gmq, bd)


def _flash_t(qz_ref, k_ref, vt_ref, v_rows, m_sc, l_sc, acc_sc):
    n_maps = qz_ref.shape[0]
    n_chunks, _, chunk = vt_ref.shape
    m_sc[...] = jnp.full(m_sc.shape, NEG, F32)
    l_sc[...] = jnp.zeros(l_sc.shape, F32)
    acc_sc[...] = jnp.zeros(acc_sc.shape, F32)

    def body(c, carry):
        kc = k_ref[pl.ds(pl.multiple_of(c * chunk, chunk), chunk), :]
        vc = vt_ref[c]
        for g in range(n_maps):
            r0, nr = v_rows(g)
            s = _dot_nt(kc, qz_ref[g])
            m_old = m_sc[g]
            m_new = jnp.maximum(m_old, jnp.max(s, axis=0, keepdims=True))
            alpha = jnp.exp2(m_old - m_new)
            p = jnp.exp2(s - m_new)
            l_sc[g] = alpha * l_sc[g] + jnp.sum(p, axis=0, keepdims=True)
            acc_sc[g] = alpha * acc_sc[g] + _dot(vc[r0:r0 + nr, :], p.astype(BF16))
            m_sc[g] = m_new
        return carry

    lax.fori_loop(0, n_chunks, body, 0)


def _stream_t(qz_ref, k_ref, vt_ref, v_rows, emit, *, group, ahead):
    n_maps, tq, _ = qz_ref.shape
    n_tiles, _, tile = vt_ref.shape
    per_tile = tile // KV_CHUNK
    n_chunks = n_tiles * per_tile
    n_groups = n_chunks // group
    assert n_chunks % group == 0

    for g in range(n_maps):
        qt = qz_ref[g].astype(F32).T.astype(BF16)
        r0, nr = v_rows(g)

        def scores(j, qt=qt):
            return [_dot(k_ref[c * KV_CHUNK:(c + 1) * KV_CHUNK, :], qt)
                    for c in range(j * group, (j + 1) * group)]

        pending = [scores(j) for j in range(min(ahead, n_groups))]
        l8 = jnp.zeros((SUBLANES, tq), F32)
        pv = None
        for j in range(n_groups):
            if j + ahead < n_groups:
                pending.append(scores(j + ahead))
            for u, s in enumerate(pending.pop(0)):
                c = j * group + u
                off = (c % per_tile) * KV_CHUNK
                p = jnp.exp2(s)
                l8 = l8 + jnp.sum(p.reshape(KV_CHUNK // SUBLANES, SUBLANES, tq), axis=0)
                d = _dot(vt_ref[c // per_tile, r0:r0 + nr, off:off + KV_CHUNK], p.astype(BF16))
                pv = d if pv is None else pv + d
        emit(g, pv * (1.0 / jnp.sum(l8, axis=0, keepdims=True)))


def _stream_loop_t(qz_ref, k_ref, vt_ref, nr, acc_sc):
    n_maps, tq, _ = qz_ref.shape
    n_tiles, _, tile = vt_ref.shape
    per_tile = tile // KV_CHUNK
    group = math.gcd(LOOP_GROUP, n_tiles * per_tile)
    assert group % per_tile == 0
    outs = []
    for g in range(n_maps):
        qg = qz_ref[g]
        acc_sc[g] = jnp.zeros((nr, tq), F32)

        def body(i, l8, g=g, qg=qg):
            where = [(i * (group // per_tile) + u // per_tile, (u % per_tile) * KV_CHUNK)
                     for u in range(group)]
            scores = []
            for t, off in where:
                kc = k_ref[pl.ds(pl.multiple_of(t * tile + off, KV_CHUNK), KV_CHUNK), :]
                scores.append(_dot_nt(kc, qg))
            pv = None
            for (t, off), s in zip(where, scores):
                p = jnp.exp2(s)
                l8 = l8 + jnp.sum(p.reshape(KV_CHUNK // SUBLANES, SUBLANES, tq), axis=0)
                d = _dot(vt_ref[t, 0:nr, off:off + KV_CHUNK], p.astype(BF16))
                pv = d if pv is None else pv + d
            acc_sc[g] += pv
            return l8

        l8 = lax.fori_loop(0, n_tiles * per_tile // group, body, jnp.zeros((SUBLANES, tq), F32))
        outs.append(acc_sc[g] * (1.0 / jnp.sum(l8, axis=0, keepdims=True)))
    return outs


def _axial_kernel(qz_ref, k_ref, vt_ref, mqz_ref, mk_ref, mvt_ref, g1_ref, gm_ref, y1_ref, ym_ref,
                  *scratch, bounded):
    v_rows = lambda g: ((g // GQA_GROUP) * HEAD, HEAD)
    if bounded:
        mem_pairs = _mem_scores(mqz_ref, mk_ref)
        ym = _mem_finish(mem_pairs, mvt_ref).T
        ym_ref[...] = (ym * gm_ref[...]).astype(BF16)
        held = {}

        def emit(g, o):
            held[g] = o
            if g % 2 == 1:
                cols = slice((g // 2) * LANES, (g // 2 + 1) * LANES)
                y = jnp.concatenate([held.pop(g - 1), held.pop(g)], axis=0).T
                y1_ref[:, cols] = (y * g1_ref[:, cols]).astype(BF16)

        _stream_t(qz_ref, k_ref, vt_ref, v_rows, emit, group=1, ahead=KV_SKEW)
    else:
        ym = _mem_attention_t(mqz_ref, mk_ref, mvt_ref).T
        ym_ref[...] = (ym * gm_ref[...]).astype(BF16)
        m_sc, l_sc, acc_sc = scratch
        _flash_t(qz_ref, k_ref, vt_ref, v_rows, m_sc, l_sc, acc_sc)
        outs = [acc_sc[g] * (1.0 / l_sc[g]) for g in range(N_Q_HEADS)]
        y1 = jnp.concatenate(outs, axis=0).T
        y1_ref[...] = (y1 * g1_ref[...]).astype(BF16)


def _axial_attention(qz, k, vt, mqz, mk, mvt, gate, batch, seq, bounded):
    t = k.shape[0]
    tq = STREAM_Q_TILE if bounded else Q_TILE
    nq = seq // tq
    n_chunks = seq // ROW_TILE
    n_mem = mk.shape[1]
    tok = lambda b, q: b * nq + q
    scratch = [] if bounded else [
        pltpu.VMEM((N_Q_HEADS, 1, tq), F32),
        pltpu.VMEM((N_Q_HEADS, 1, tq), F32),
        pltpu.VMEM((N_Q_HEADS, HEAD, tq), F32),
    ]
    return pl.pallas_call(
        functools.partial(_axial_kernel, bounded=bounded),
        grid=(batch, nq),
        in_specs=[
            pl.BlockSpec((N_Q_HEADS, tq, LANES), lambda b, q: (0, tok(b, q), 0)),
            pl.BlockSpec((seq, LANES), lambda b, q: (b, 0)),
            pl.BlockSpec((n_chunks, LANES, ROW_TILE), lambda b, q: (b, 0, 0)),
            pl.BlockSpec((MEM_HEADS, tq, LANES), lambda b, q: (0, tok(b, q), 0)),
            pl.BlockSpec((1, n_mem, MEM_HEADS * HEAD), lambda b, q: (b, 0, 0)),
            pl.BlockSpec((1, MEM_HEADS * HEAD, n_mem), lambda b, q: (b, 0, 0)),
            pl.BlockSpec((tq, 512), lambda b, q: (tok(b, q), 0)),
            pl.BlockSpec((tq, 256), lambda b, q: (tok(b, q), 4)),
        ],
        out_specs=[
            pl.BlockSpec((tq, 512), lambda b, q: (tok(b, q), 0)),
            pl.BlockSpec((tq, 256), lambda b, q: (tok(b, q), 0)),
        ],
        out_shape=[
            jax.ShapeDtypeStruct((t, 512), BF16),
            jax.ShapeDtypeStruct((t, 256), BF16),
        ],
        scratch_shapes=scratch,
        compiler_params=_params(2),
        name="axial_attention" if bounded else "axial_attention_online",
    )(qz, k, vt, mqz, mk, mvt, gate, gate)


def _diff_kernel(qz_ref, k_ref, vt_ref, lam_ref, sg_ref, g2_ref, y2_ref, *scratch, lambda_init, bounded):
    v_rows = lambda g: (0, 2 * HEAD)
    if bounded:
        o0, o1 = _stream_loop_t(qz_ref, k_ref, vt_ref, 2 * HEAD, *scratch)
    else:
        m_sc, l_sc, acc_sc = scratch
        _flash_t(qz_ref, k_ref, vt_ref, v_rows, m_sc, l_sc, acc_sc)
        o0, o1 = (acc_sc[g] * (1.0 / l_sc[g]) for g in range(2))
    lv = lam_ref[...]
    lam = (jnp.exp(jnp.sum(lv[0:1] * lv[1:2], axis=-1, keepdims=True))
           - jnp.exp(jnp.sum(lv[2:3] * lv[3:4], axis=-1, keepdims=True)) + lambda_init)
    o = o0 - lam * o1
    ms = jnp.mean(o * o, axis=0, keepdims=True)
    on = (o * lax.rsqrt(ms + EPS)).T
    y2_ref[...] = (on * sg_ref[...] * (1.0 - lambda_init) * g2_ref[...]).astype(BF16)


def _diff_attention(dqz, dk, dvt, lam, subln_g, gate, batch, seq, layer, bounded):
    t = dk.shape[0]
    tq = STREAM_Q_TILE if bounded else Q_TILE
    nq = seq // tq
    n_chunks = seq // ROW_TILE
    lambda_init = 0.8 - 0.6 * math.exp(-0.3 * layer)
    tok = lambda b, q: b * nq + q
    scratch = [pltpu.VMEM((2, 2 * HEAD, tq), F32)] if bounded else [
        pltpu.VMEM((2, 1, tq), F32),
        pltpu.VMEM((2, 1, tq), F32),
        pltpu.VMEM((2, 2 * HEAD, tq), F32),
    ]
    return pl.pallas_call(
        functools.partial(_diff_kernel, lambda_init=lambda_init, bounded=bounded),
        grid=(batch, DIFF_HEADS, nq),
        in_specs=[
            pl.BlockSpec((2, tq, LANES), lambda b, h, q: (h, tok(b, q), 0)),
            pl.BlockSpec((seq, LANES), lambda b, h, q: (b, h)),
            pl.BlockSpec((n_chunks, LANES, ROW_TILE), lambda b, h, q: (b, h, 0)),
            pl.BlockSpec((4, HEAD), lambda b, h, q: (0, 0)),
            pl.BlockSpec((1, 2 * HEAD), lambda b, h, q: (0, 0)),
            pl.BlockSpec((tq, LANES), lambda b, h, q: (tok(b, q), 4 + h)),
        ],
        out_specs=pl.BlockSpec((tq, LANES), lambda b, h, q: (tok(b, q), h)),
        out_shape=jax.ShapeDtypeStruct((t, 512), BF16),
        scratch_shapes=scratch,
        compiler_params=_params(3),
        name="diff_attention" if bounded else "diff_attention_online",
    )(dqz, dk, dvt, lam, subln_g.reshape(1, 2 * HEAD), gate)


def _out_proj_kernel(y1_ref, y2_ref, ym_ref, x_ref, w_ref, o_ref):
    acc = _dot(y1_ref[...], w_ref[0:512, :])
    acc = acc + _dot(y2_ref[...], w_ref[512:1024, :])
    acc = acc + _dot(ym_ref[...], w_ref[1024:MIX_W, :])
    o_ref[...] = x_ref[...] + acc


def _out_proj(y1, y2, ym, x2, w):
    t = x2.shape[0]
    return pl.pallas_call(
        _out_proj_kernel,
        grid=(t // OUT_ROW_TILE,),
        in_specs=[
            pl.BlockSpec((OUT_ROW_TILE, 512), lambda i: (i, 0)),
            pl.BlockSpec((OUT_ROW_TILE, 512), lambda i: (i, 0)),
            pl.BlockSpec((OUT_ROW_TILE, 256), lambda i: (i, 0)),
            pl.BlockSpec((OUT_ROW_TILE, D_MODEL), lambda i: (i, 0)),
            pl.BlockSpec((MIX_W, D_MODEL), lambda i: (0, 0)),
        ],
        out_specs=pl.BlockSpec((OUT_ROW_TILE, D_MODEL), lambda i: (i, 0)),
        out_shape=jax.ShapeDtypeStruct((t, D_MODEL), F32),
        compiler_params=_params(1),
        name="out_proj",
    )(y1, y2, ym, x2, w)


def _rope_angles(pos, dim):
    inv = ROPE_THETA ** (-jnp.arange(0, dim, 2, dtype=F32) / dim)
    return pos.astype(F32)[:, None] * inv[None, :]


def _rope_tables(seq):
    pos = jnp.arange(seq)
    a1 = _rope_angles(pos, HEAD)
    cos1 = jnp.concatenate([jnp.cos(a1), jnp.cos(a1)], axis=-1)
    sin1 = jnp.concatenate([-jnp.sin(a1), jnp.sin(a1)], axis=-1)
    ar = _rope_angles(pos // GRID_W, HEAD // 2)
    ac = _rope_angles(pos % GRID_W, HEAD // 2)
    cosa = jnp.concatenate([jnp.cos(ar), jnp.cos(ar), jnp.cos(ac), jnp.cos(ac)], axis=-1)
    sina = jnp.concatenate([-jnp.sin(ar), jnp.sin(ar), -jnp.sin(ac), jnp.sin(ac)], axis=-1)
    rep = lambda a: jnp.tile(a, (1, LANES // HEAD))
    return rep(cos1), rep(sin1), rep(cosa), rep(sina)


def _scores_bounded(qk_g):
    bound = HEAD * Q_SCALE * 1.01 * jnp.max(jnp.abs(qk_g[0])) * jnp.max(jnp.abs(qk_g[1]))
    return bound <= SAFE_SCORE_BOUND


def _tile_gain(g, width):
    return jnp.tile(g.astype(F32), width // HEAD).reshape(1, width)


def _trunk(x, mem, p):
    batch, seq, _ = x.shape
    x2 = x.reshape(batch * seq, D_MODEL)
    cos1, sin1, cosa, sina = _rope_tables(seq)
    bd = p["bd"]
    gk_mem = jnp.stack([_tile_gain(p["mem_qk_g"][l, 1], 256) for l in range(DEPTH)])
    mk_all, mvt_all = _memkv(mem, p["mem_norm_g"], p["w_mem_kv"], gk_mem, bd)
    for l in range(DEPTH):
        g_in, w_in = p["norm_g"][l].reshape(1, D_MODEL), p["w_in"][l]
        gmq = _tile_gain(p["mem_qk_g"][l, 0], 256)
        mk, mvt = mk_all[l], mvt_all[l]
        if l % 2 == 0:
            e = l // 2
            y1, qz, k, vt, mqz, gate = _proj_even(
                x2, seq, g_in, w_in, cos1, sin1, _tile_gain(p["swa_qk_g"][e, 0], 512),
                _tile_gain(p["swa_qk_g"][e, 1], LANES), gmq, p["conv_w"][e], bd)
            sink_ok = jnp.max(jnp.abs(p["swa_sink"][e])) * LOG2E <= SAFE_SCORE_BOUND
            y2, ym = lax.cond(
                _scores_bounded(p["swa_qk_g"][e]) & _scores_bounded(p["mem_qk_g"][l]) & sink_ok,
                functools.partial(_window_attention, batch=batch, seq=seq, bounded=True),
                functools.partial(_window_attention, batch=batch, seq=seq, bounded=False),
                p["swa_sink"][e], qz, k, vt, mqz, mk, mvt, gate)
        else:
            o = l // 2
            qz, k, vt, dqz, dk, dvt, mqz, gate = _proj_odd(
                x2, seq, g_in, w_in, cos1, sin1, cosa, sina, _tile_gain(p["ax_qk_g"][o, 0], 512),
                _tile_gain(p["ax_qk_g"][o, 1], LANES), _tile_gain(p["diff_qk_g"][o, 0], 512),
                _tile_gain(p["diff_qk_g"][o, 1], 512), gmq, bd)
            y1, ym = lax.cond(
                _scores_bounded(p["ax_qk_g"][o]) & _scores_bounded(p["mem_qk_g"][l]),
                functools.partial(_axial_attention, batch=batch, seq=seq, bounded=True),
                functools.partial(_axial_attention, batch=batch, seq=seq, bounded=False),
                qz, k, vt, mqz, mk, mvt, gate)
            y2 = lax.cond(
                _scores_bounded(p["diff_qk_g"][o]),
                functools.partial(_diff_attention, batch=batch, seq=seq, layer=l, bounded=True),
                functools.partial(_diff_attention, batch=batch, seq=seq, layer=l, bounded=False),
                dqz, dk, dvt, p["diff_lambda"][o], p["diff_subln_g"][o], gate)
        x2 = _out_proj(y1, y2, ym, x2, p["w_out"][l])
    return x2.reshape(batch, seq, D_MODEL)


def kernel(x_prompt, x_sample, mem_prompt, mem_sample, norm_g, w_in, w_out, mem_norm_g, w_mem_kv,
           mem_qk_g, conv_w, swa_qk_g, swa_sink, ax_qk_g, diff_qk_g, diff_lambda, diff_subln_g):
    grp = jnp.arange(2 * LANES) // HEAD
    p = dict(
        norm_g=norm_g, w_in=w_in.astype(BF16), w_out=w_out.astype(BF16), mem_norm_g=mem_norm_g,
        w_mem_kv=w_mem_kv.astype(BF16), mem_qk_g=mem_qk_g, conv_w=conv_w, swa_qk_g=swa_qk_g,
        swa_sink=swa_sink, ax_qk_g=ax_qk_g, diff_qk_g=diff_qk_g, diff_lambda=diff_lambda,
        diff_subln_g=diff_subln_g, bd=(grp[:, None] == grp[None, :]).astype(BF16))
    return (_trunk(x_prompt, mem_prompt, p), _trunk(x_sample, mem_sample, p))
```

```python
import functools
import math

import jax
import jax.numpy as jnp
from jax import lax
from jax.experimental import pallas as pl
from jax.experimental.pallas import tpu as pltpu

F32 = jnp.float32
BF16 = jnp.bfloat16

D_MODEL = 1024
DEPTH = 4
HEAD = 64
LANES = 128
SUBLANES = 8
N_Q_HEADS = 8
GQA_GROUP = 4
DIFF_HEADS = 4
MEM_HEADS = 4
CONV_W = 512
WINDOW = 128
GRID_W = 64
ROPE_THETA = 10000.0
EPS = 1e-6
NEG = -1e30
LOG2E = 1.4426950408889634
Q_SCALE = HEAD ** -0.5 * LOG2E
SAFE_SCORE_BOUND = 60.0

IN_W = 3840
MIX_W = 1280
GATE_OFF = 2560
MQ_OFF = 2304

ROW_TILE = 512
OUT_ROW_TILE = 1024
Q_TILE = 256
STREAM_Q_TILE = 512
WINDOW_Q_TILE = 1024
DIFF_Q_TILE = 1024
KV_CHUNK = 256
LOOP_GROUP = 32
KV_SKEW = 3
VMEM_LIMIT = 56 * 1024 * 1024

_NT = (((1,), (1,)), ((), ()))


def _params(n_axes):
    return pltpu.CompilerParams(dimension_semantics=("arbitrary",) * n_axes, vmem_limit_bytes=VMEM_LIMIT)


def _dot(a, b):
    return jnp.dot(a, b, preferred_element_type=F32)


def _dot_nt(a, b):
    return lax.dot_general(a, b, _NT, preferred_element_type=F32)


def _rms_rows(x, g):
    ms = jnp.mean(x * x, axis=-1, keepdims=True)
    return x * lax.rsqrt(ms + EPS) * g


def _head_sumsq(x, bd):
    w = x.shape[1]
    outs = []
    step = 2 * LANES if w % (2 * LANES) == 0 else LANES
    for c in range(0, w, step):
        sq = x[:, c:c + step]
        outs.append(_dot((sq * sq).astype(BF16), bd[:step, :step]))
    return outs[0] if len(outs) == 1 else jnp.concatenate(outs, axis=1)


def _head_rms(x, g, bd):
    return x * lax.rsqrt(_head_sumsq(x, bd) * (1.0 / HEAD) + EPS) * g


def _rope(x, cos, sin_signed, half):
    r = x.shape[0]
    lane = lax.broadcasted_iota(jnp.int32, (r, LANES), 1)
    first = (lane & (2 * half - 1)) < half
    outs = []
    for c in range(0, x.shape[1], LANES):
        xc = x[:, c:c + LANES]
        partner = jnp.where(first, pltpu.roll(xc, LANES - half, 1), pltpu.roll(xc, half, 1))
        outs.append(xc * cos + partner * sin_signed)
    return outs[0] if len(outs) == 1 else jnp.concatenate(outs, axis=1)


def _place_heads(x, target_half, out_ref):
    r = x.shape[0]
    low = lax.broadcasted_iota(jnp.int32, (r, LANES), 1) < HEAD
    for h in range(x.shape[1] // HEAD):
        xc = x[:, (h // 2) * LANES:(h // 2 + 1) * LANES]
        th = target_half(h)
        if th != h % 2:
            xc = pltpu.roll(xc, HEAD, 1)
        keep = low if th == 0 else jnp.logical_not(low)
        out_ref[h] = jnp.where(keep, xc, 0.0).astype(BF16)


def _silu(z):
    return z * (1.0 / (1.0 + jnp.exp(-z)))


def _project(h, w_ref, splits):
    return [_dot(h, w_ref[:, a:b]) for a, b in splits]


def _memkv_kernel(mem_ref, g_ref, w_ref, gk_ref, bd_ref, mk_ref, mvt_ref):
    h = _rms_rows(mem_ref[0], g_ref[0]).astype(BF16)
    mkv = _dot(h, w_ref[0])
    half = mkv.shape[1] // 2
    mk = _head_rms(mkv[:, :half], gk_ref[0], bd_ref[...])
    mk_ref[0, 0] = mk.astype(BF16)
    mvt_ref[0, 0] = mkv[:, half:].T.astype(BF16)


def _memkv(mem, mem_norm_g, w_mem_kv, gk_tiled, bd):
    b, n_mem, _ = mem.shape
    width = MEM_HEADS * HEAD
    return pl.pallas_call(
        _memkv_kernel,
        grid=(DEPTH, b),
        in_specs=[
            pl.BlockSpec((1, n_mem, D_MODEL), lambda l, i: (i, 0, 0)),
            pl.BlockSpec((1, 1, D_MODEL), lambda l, i: (l, 0, 0)),
            pl.BlockSpec((1, D_MODEL, 2 * width), lambda l, i: (l, 0, 0)),
            pl.BlockSpec((1, 1, width), lambda l, i: (l, 0, 0)),
            pl.BlockSpec((2 * LANES, 2 * LANES), lambda l, i: (0, 0)),
        ],
        out_specs=[
            pl.BlockSpec((1, 1, n_mem, width), lambda l, i: (l, i, 0, 0)),
            pl.BlockSpec((1, 1, width, n_mem), lambda l, i: (l, i, 0, 0)),
        ],
        out_shape=[
            jax.ShapeDtypeStruct((DEPTH, b, n_mem, width), BF16),
            jax.ShapeDtypeStruct((DEPTH, b, width, n_mem), BF16),
        ],
        compiler_params=_params(2),
        name="mem_kv",
    )(mem, mem_norm_g.reshape(DEPTH, 1, D_MODEL), w_mem_kv, gk_tiled, bd)


def _proj_even_kernel(x_ref, xp_ref, xn_ref, g_ref, w_ref, cos_ref, sin_ref, gq_ref, gk_ref,
                      gmq_ref, cw_ref, bd_ref,
                      y1_ref, qz_ref, k_ref, vt_ref, mqz_ref, gate_ref, conv_sc, *, tiles_per_seq):
    i = pl.program_id(0)
    pos_tile = i % tiles_per_seq
    bd = bd_ref[...]
    cos = cos_ref[...]
    sin = sin_ref[...]
    rows = x_ref.shape[0]

    x_ext = jnp.concatenate([x_ref[...], xp_ref[...], xn_ref[...]], axis=0)
    h_ext = _rms_rows(x_ext, g_ref[...]).astype(BF16)
    h = h_ext[:rows]
    proj = lambda a, b: _dot(h, w_ref[:, a:b])

    z = proj(GATE_OFF, IN_W)
    gchc = _dot(h_ext, w_ref[:, 512:1536])
    gate_ref[...] = _silu(z[:, 512:MIX_W])
    gb = proj(0, 512)
    q = proj(1536, 2048)

    inner_ext = gchc[:, 0:512] * gchc[:, 512:1024]
    inner = inner_ext[:rows]
    prev_row = inner_ext[rows + SUBLANES - 1:rows + SUBLANES]
    next_row = inner_ext[rows + SUBLANES:rows + SUBLANES + 1]
    prev_row = jnp.where(pos_tile == 0, 0.0, prev_row)
    next_row = jnp.where(pos_tile == tiles_per_seq - 1, 0.0, next_row)
    conv_sc[pl.ds(SUBLANES, rows), :] = inner
    conv_sc[pl.ds(SUBLANES - 1, 1), :] = prev_row
    conv_sc[pl.ds(SUBLANES + rows, 1), :] = next_row
    cw = cw_ref[...]
    conv = (conv_sc[pl.ds(SUBLANES - 1, rows), :] * cw[0:1, :] + inner * cw[1:2, :]
            + conv_sc[pl.ds(SUBLANES + 1, rows), :] * cw[2:3, :])
    y1_ref[...] = (gb * conv * _silu(z[:, 0:512])).astype(BF16)

    mq = proj(MQ_OFF, GATE_OFF)
    q = _rope(_head_rms(q, gq_ref[...], bd), cos, sin, HEAD // 2) * Q_SCALE
    _place_heads(q, lambda h: h // GQA_GROUP, qz_ref)

    kv = proj(2048, 2304)
    mq = _head_rms(mq, gmq_ref[...], bd) * Q_SCALE
    _place_heads(mq, lambda h: h % 2, mqz_ref)

    k = _rope(_head_rms(kv[:, 0:LANES], gk_ref[...], bd), cos, sin, HEAD // 2)
    k_ref[...] = k.astype(BF16)
    vt_ref[0] = kv[:, LANES:2 * LANES].T.astype(BF16)


def _proj_even(x2, seq, g, w, cos, sin, gq, gk, gmq, conv_w, bd):
    t = x2.shape[0]
    n_tiles = t // ROW_TILE
    tiles_per_seq = seq // ROW_TILE
    sub_per_tile = ROW_TILE // SUBLANES
    last_sub = t // SUBLANES - 1
    row = lambda w: pl.BlockSpec((1, w), lambda i: (0, 0))
    tab = pl.BlockSpec((ROW_TILE, LANES), lambda i: (i % tiles_per_seq, 0))
    return pl.pallas_call(
        functools.partial(_proj_even_kernel, tiles_per_seq=tiles_per_seq),
        grid=(n_tiles,),
        in_specs=[
            pl.BlockSpec((ROW_TILE, D_MODEL), lambda i: (i, 0)),
            pl.BlockSpec((SUBLANES, D_MODEL), lambda i: (jnp.maximum(i * sub_per_tile - 1, 0), 0)),
            pl.BlockSpec((SUBLANES, D_MODEL), lambda i: (jnp.minimum((i + 1) * sub_per_tile, last_sub), 0)),
            row(D_MODEL),
            pl.BlockSpec((D_MODEL, IN_W), lambda i: (0, 0)),
            tab, tab, row(512), row(LANES), row(256),
            pl.BlockSpec((3, CONV_W), lambda i: (0, 0)),
            pl.BlockSpec((2 * LANES, 2 * LANES), lambda i: (0, 0)),
        ],
        out_specs=[
            pl.BlockSpec((ROW_TILE, CONV_W), lambda i: (i, 0)),
            pl.BlockSpec((N_Q_HEADS, ROW_TILE, LANES), lambda i: (0, i, 0)),
            pl.BlockSpec((ROW_TILE, LANES), lambda i: (i, 0)),
            pl.BlockSpec((1, LANES, ROW_TILE), lambda i: (i, 0, 0)),
            pl.BlockSpec((MEM_HEADS, ROW_TILE, LANES), lambda i: (0, i, 0)),
            pl.BlockSpec((ROW_TILE, 768), lambda i: (i, 0)),
        ],
        out_shape=[
            jax.ShapeDtypeStruct((t, CONV_W), BF16),
            jax.ShapeDtypeStruct((N_Q_HEADS, t, LANES), BF16),
            jax.ShapeDtypeStruct((t, LANES), BF16),
            jax.ShapeDtypeStruct((n_tiles, LANES, ROW_TILE), BF16),
            jax.ShapeDtypeStruct((MEM_HEADS, t, LANES), BF16),
            jax.ShapeDtypeStruct((t, 768), F32),
        ],
        scratch_shapes=[pltpu.VMEM((ROW_TILE + 2 * SUBLANES, CONV_W), F32)],
        compiler_params=_params(1),
        name="proj_even",
    )(x2, x2, x2, g, w, cos, sin, gq, gk, gmq, conv_w, bd)


def _mem_scores(mqz_ref, mk_ref):
    pairs = []
    for grp in range(MEM_HEADS // 2):
        q2 = jnp.concatenate([mqz_ref[2 * grp], mqz_ref[2 * grp + 1]], axis=0)
        pairs.append(_dot_nt(mk_ref[0, :, grp * LANES:(grp + 1) * LANES], q2))
    return pairs


def _mem_finish(pairs, mvt_ref):
    tq = pairs[0].shape[1] // 2
    outs = []
    for h in range(MEM_HEADS):
        p = jnp.exp2(pairs[h // 2][:, (h % 2) * tq:(h % 2 + 1) * tq])
        l = jnp.sum(p, axis=0, keepdims=True)
        o = _dot(mvt_ref[0, h * HEAD:(h + 1) * HEAD, :], p.astype(BF16))
        outs.append(o * (1.0 / l))
    return jnp.concatenate(outs, axis=0)


def _mem_attention_t(mqz_ref, mk_ref, mvt_ref):
    outs = []
    for h in range(MEM_HEADS):
        grp = h // 2
        s = _dot_nt(mk_ref[0, :, grp * LANES:(grp + 1) * LANES], mqz_ref[h])
        m = jnp.max(s, axis=0, keepdims=True)
        p = jnp.exp2(s - m)
        l = jnp.sum(p, axis=0, keepdims=True)
        o = _dot(mvt_ref[0, h * HEAD:(h + 1) * HEAD, :], p.astype(BF16))
        outs.append(o * (1.0 / l))
    return jnp.concatenate(outs, axis=0)


def _window_kernel(sink_ref, qz_ref, kp_ref, kc_ref, kn_ref, vp_ref, vc_ref, vn_ref, mqz_ref, mk_ref,
                   mvt_ref, g2_ref, gm_ref, y2_ref, ym_ref, *, seq):
    tq = qz_ref.shape[1]
    t0 = pl.program_id(1) * tq
    kband = jnp.concatenate([kp_ref[...], kc_ref[...], kn_ref[...]], axis=0)
    vband = jnp.concatenate([vp_ref[0]] + [vc_ref[c] for c in range(vc_ref.shape[0])] + [vn_ref[0]],
                            axis=1)
    nk = tq + 2 * WINDOW
    j = lax.broadcasted_iota(jnp.int32, (nk, tq), 0)
    i = lax.broadcasted_iota(jnp.int32, (nk, tq), 1)
    kpos = j + (t0 - WINDOW)
    mask = (j >= i) & (j <= i + 2 * WINDOW) & (kpos >= 0) & (kpos < seq)
    outs = []
    for g in range(N_Q_HEADS):
        kv = g // GQA_GROUP
        s = jnp.where(mask, _dot_nt(kband, qz_ref[g]), NEG)
        sink = sink_ref[g] * LOG2E
        m = jnp.maximum(jnp.max(s, axis=0, keepdims=True), sink)
        p = jnp.exp2(s - m)
        l = jnp.sum(p, axis=0, keepdims=True) + jnp.exp2(sink - m)
        o = _dot(vband[kv * HEAD:(kv + 1) * HEAD, :], p.astype(BF16))
        outs.append(o * (1.0 / l))
    y2 = jnp.concatenate(outs, axis=0).T
    y2_ref[...] = (y2 * g2_ref[...]).astype(BF16)
    ym = _mem_attention_t(mqz_ref, mk_ref, mvt_ref).T
    ym_ref[...] = (ym * gm_ref[...]).astype(BF16)


def _window_stream_kernel(sink_ref, qz_ref, kp_ref, kc_ref, kn_ref, vp_ref, vc_ref, vn_ref, mqz_ref,
                          mk_ref, mvt_ref, g2_ref, gm_ref, y2_ref, ym_ref, *, seq):
    tq = qz_ref.shape[1]
    t0 = pl.program_id(1) * tq
    kband = jnp.concatenate([kp_ref[...], kc_ref[...], kn_ref[...]], axis=0)
    vband = jnp.concatenate([vp_ref[0]] + [vc_ref[c] for c in range(vc_ref.shape[0])] + [vn_ref[0]],
                            axis=1)
    nk = 3 * WINDOW
    j = lax.broadcasted_iota(jnp.int32, (nk, WINDOW), 0)
    i = lax.broadcasted_iota(jnp.int32, (nk, WINDOW), 1)
    in_band = (j >= i) & (j <= i + 2 * WINDOW)
    sink_p = [jnp.exp2(jnp.full((1, WINDOW), sink_ref[g] * LOG2E, F32)) for g in range(N_Q_HEADS)]
    scores = []
    for a in range(tq // WINDOW):
        for kv in range(N_Q_HEADS // GQA_GROUP):
            q4 = jnp.concatenate([qz_ref[kv * GQA_GROUP + r, a * WINDOW:(a + 1) * WINDOW, :]
                                  for r in range(GQA_GROUP)], axis=0)
            scores.append(_dot_nt(kband[a * WINDOW:a * WINDOW + nk], q4))
    mem_pairs = _mem_scores(mqz_ref, mk_ref)
    blocks = [[None] * (tq // WINDOW) for _ in range(N_Q_HEADS)]
    for a in range(tq // WINDOW):
        kpos = j + (t0 + (a - 1) * WINDOW)
        bias1 = jnp.where(in_band & (kpos >= 0) & (kpos < seq), 0.0, NEG)
        bias = jnp.concatenate([bias1] * GQA_GROUP, axis=1)
        for kv in range(N_Q_HEADS // GQA_GROUP):
            p = jnp.exp2(scores[a * (N_Q_HEADS // GQA_GROUP) + kv] + bias)
            sinks = jnp.concatenate(sink_p[kv * GQA_GROUP:(kv + 1) * GQA_GROUP], axis=1)
            l = jnp.sum(p, axis=0, keepdims=True) + sinks
            o = _dot(vband[kv * HEAD:(kv + 1) * HEAD, a * WINDOW:a * WINDOW + nk], p.astype(BF16))
            o = o * (1.0 / l)
            for r in range(GQA_GROUP):
                blocks[kv * GQA_GROUP + r][a] = o[:, r * WINDOW:(r + 1) * WINDOW]
    y2 = jnp.concatenate([jnp.concatenate(b, axis=1) for b in blocks], axis=0).T
    y2_ref[...] = (y2 * g2_ref[...]).astype(BF16)
    ym = _mem_finish(mem_pairs, mvt_ref).T
    ym_ref[...] = (ym * gm_ref[...]).astype(BF16)


def _window_attention(sink, qz, k, vt, mqz, mk, mvt, gate, batch, seq, bounded):
    t = k.shape[0]
    tq = WINDOW_Q_TILE if bounded else Q_TILE
    nq = seq // tq
    w_per_q = tq // WINDOW
    w_per_seq = seq // WINDOW
    w_per_chunk = ROW_TILE // WINDOW
    n_mem = mk.shape[1]
    if tq >= ROW_TILE:
        cur_v = pl.BlockSpec((tq // ROW_TILE, LANES, ROW_TILE), lambda b, q: (b * nq + q, 0, 0))
    else:
        q_per_chunk = ROW_TILE // tq
        cur_v = pl.BlockSpec((1, LANES, tq),
                             lambda b, q: ((b * nq + q) // q_per_chunk, 0, (b * nq + q) % q_per_chunk))

    def prev_w(b, q):
        return b * w_per_seq + jnp.maximum(q * w_per_q - 1, 0)

    def next_w(b, q):
        return b * w_per_seq + jnp.minimum((q + 1) * w_per_q, w_per_seq - 1)

    tok = lambda b, q: b * nq + q
    return pl.pallas_call(
        functools.partial(_window_stream_kernel if bounded else _window_kernel, seq=seq),
        grid=(batch, nq),
        in_specs=[
            pl.BlockSpec(memory_space=pltpu.SMEM),
            pl.BlockSpec((N_Q_HEADS, tq, LANES), lambda b, q: (0, tok(b, q), 0)),
            pl.BlockSpec((WINDOW, LANES), lambda b, q: (prev_w(b, q), 0)),
            pl.BlockSpec((tq, LANES), lambda b, q: (tok(b, q), 0)),
            pl.BlockSpec((WINDOW, LANES), lambda b, q: (next_w(b, q), 0)),
            pl.BlockSpec((1, LANES, WINDOW), lambda b, q: (prev_w(b, q) // w_per_chunk, 0, prev_w(b, q) % w_per_chunk)),
            cur_v,
            pl.BlockSpec((1, LANES, WINDOW), lambda b, q: (next_w(b, q) // w_per_chunk, 0, next_w(b, q) % w_per_chunk)),
            pl.BlockSpec((MEM_HEADS, tq, LANES), lambda b, q: (0, tok(b, q), 0)),
            pl.BlockSpec((1, n_mem, MEM_HEADS * HEAD), lambda b, q: (b, 0, 0)),
            pl.BlockSpec((1, MEM_HEADS * HEAD, n_mem), lambda b, q: (b, 0, 0)),
            pl.BlockSpec((tq, 512), lambda b, q: (tok(b, q), 0)),
            pl.BlockSpec((tq, 256), lambda b, q: (tok(b, q), 2)),
        ],
        out_specs=[
            pl.BlockSpec((tq, 512), lambda b, q: (tok(b, q), 0)),
            pl.BlockSpec((tq, 256), lambda b, q: (tok(b, q), 0)),
        ],
        out_shape=[
            jax.ShapeDtypeStruct((t, 512), BF16),
            jax.ShapeDtypeStruct((t, 256), BF16),
        ],
        compiler_params=_params(2),
        name="window_attention" if bounded else "window_attention_max",
    )(sink, qz, k, k, k, vt, vt, vt, mqz, mk, mvt, gate, gate)


def _proj_odd_kernel(x_ref, g_ref, w_ref, cos1_ref, sin1_ref, cosa_ref, sina_ref, gq_ref, gk_ref,
                     gdq_ref, gdk_ref, gmq_ref, bd_ref,
                     qz_ref, k_ref, vt_ref, dqz_ref, dk_ref, dvt_ref, mqz_ref, gate_ref):
    bd = bd_ref[...]
    cos1, sin1 = cos1_ref[...], sin1_ref[...]
    cosa, sina = cosa_ref[...], sina_ref[...]
    h = _rms_rows(x_ref[...], g_ref[...]).astype(BF16)
    proj = lambda a, b: _dot(h, w_ref[:, a:b])

    z = proj(GATE_OFF, IN_W)
    q = proj(0, 512)
    gate_ref[...] = _silu(z)

    dq = proj(768, 1280)
    q = _rope(_head_rms(q, gq_ref[...], bd), cosa, sina, HEAD // 4) * Q_SCALE
    _place_heads(q, lambda h: h // GQA_GROUP, qz_ref)

    dk = proj(1280, 1792)
    dq = _rope(_head_rms(dq, gdq_ref[...], bd), cos1, sin1, HEAD // 2) * Q_SCALE
    _place_heads(dq, lambda h: h % 2, dqz_ref)

    dv = proj(1792, 2304)
    dk = _rope(_head_rms(dk, gdk_ref[...], bd), cos1, sin1, HEAD // 2)
    dk_ref[...] = dk.astype(BF16)

    mq = proj(MQ_OFF, GATE_OFF)
    dvt_ref[0] = dv.T.astype(BF16)

    kv = proj(512, 768)
    mq = _head_rms(mq, gmq_ref[...], bd) * Q_SCALE
    _place_heads(mq, lambda h: h % 2, mqz_ref)

    k = _rope(_head_rms(kv[:, 0:LANES], gk_ref[...], bd), cosa, sina, HEAD // 4)
    k_ref[...] = k.astype(BF16)
    vt_ref[0] = kv[:, LANES:2 * LANES].T.astype(BF16)


def _proj_odd(x2, seq, g, w, cos1, sin1, cosa, sina, gq, gk, gdq, gdk, gmq, bd):
    t = x2.shape[0]
    n_tiles = t // ROW_TILE
    tiles_per_seq = seq // ROW_TILE
    row = lambda w: pl.BlockSpec((1, w), lambda i: (0, 0))
    tab = pl.BlockSpec((ROW_TILE, LANES), lambda i: (i % tiles_per_seq, 0))
    return pl.pallas_call(
        _proj_odd_kernel,
        grid=(n_tiles,),
        in_specs=[
            pl.BlockSpec((ROW_TILE, D_MODEL), lambda i: (i, 0)),
            row(D_MODEL),
            pl.BlockSpec((D_MODEL, IN_W), lambda i: (0, 0)),
            tab, tab, tab, tab, row(512), row(LANES), row(512), row(512), row(256),
            pl.BlockSpec((2 * LANES, 2 * LANES), lambda i: (0, 0)),
        ],
        out_specs=[
            pl.BlockSpec((N_Q_HEADS, ROW_TILE, LANES), lambda i: (0, i, 0)),
            pl.BlockSpec((ROW_TILE, LANES), lambda i: (i, 0)),
            pl.BlockSpec((1, LANES, ROW_TILE), lambda i: (i, 0, 0)),
            pl.BlockSpec((N_Q_HEADS, ROW_TILE, LANES), lambda i: (0, i, 0)),
            pl.BlockSpec((ROW_TILE, 512), lambda i: (i, 0)),
            pl.BlockSpec((1, 512, ROW_TILE), lambda i: (i, 0, 0)),
            pl.BlockSpec((MEM_HEADS, ROW_TILE, LANES), lambda i: (0, i, 0)),
            pl.BlockSpec((ROW_TILE, MIX_W), lambda i: (i, 0)),
        ],
        out_shape=[
            jax.ShapeDtypeStruct((N_Q_HEADS, t, LANES), BF16),
            jax.ShapeDtypeStruct((t, LANES), BF16),
            jax.ShapeDtypeStruct((n_tiles, LANES, ROW_TILE), BF16),
            jax.ShapeDtypeStruct((N_Q_HEADS, t, LANES), BF16),
            jax.ShapeDtypeStruct((t, 512), BF16),
            jax.ShapeDtypeStruct((n_tiles, 512, ROW_TILE), BF16),
            jax.ShapeDtypeStruct((MEM_HEADS, t, LANES), BF16),
            jax.ShapeDtypeStruct((t, MIX_W), F32),
        ],
        compiler_params=_params(1),
        name="proj_odd",
    )(x2, g, w, cos1, sin1, cosa, sina, gq, gk, gdq, gdk, gmq, bd)


def _flash_t(qz_ref, k_ref, vt_ref, v_rows, m_sc, l_sc, acc_sc):
    n_maps = qz_ref.shape[0]
    n_chunks, _, chunk = vt_ref.shape
    m_sc[...] = jnp.full(m_sc.shape, NEG, F32)
    l_sc[...] = jnp.zeros(l_sc.shape, F32)
    acc_sc[...] = jnp.zeros(acc_sc.shape, F32)

    def body(c, carry):
        kc = k_ref[pl.ds(pl.multiple_of(c * chunk, chunk), chunk), :]
        vc = vt_ref[c]
        for g in range(n_maps):
            r0, nr = v_rows(g)
            s = _dot_nt(kc, qz_ref[g])
            m_old = m_sc[g]
            m_new = jnp.maximum(m_old, jnp.max(s, axis=0, keepdims=True))
            alpha = jnp.exp2(m_old - m_new)
            p = jnp.exp2(s - m_new)
            l_sc[g] = alpha * l_sc[g] + jnp.sum(p, axis=0, keepdims=True)
            acc_sc[g] = alpha * acc_sc[g] + _dot(vc[r0:r0 + nr, :], p.astype(BF16))
            m_sc[g] = m_new
        return carry

    lax.fori_loop(0, n_chunks, body, 0)


def _stream_t(qz_ref, k_ref, vt_ref, v_rows, emit, *, group, ahead):
    n_maps, tq, _ = qz_ref.shape
    n_tiles, _, tile = vt_ref.shape
    per_tile = tile // KV_CHUNK
    n_chunks = n_tiles * per_tile
    n_groups = n_chunks // group
    assert n_chunks % group == 0

    for g in range(n_maps):
        qt = qz_ref[g].astype(F32).T.astype(BF16)
        r0, nr = v_rows(g)

        def scores(j, qt=qt):
            return [_dot(k_ref[c * KV_CHUNK:(c + 1) * KV_CHUNK, :], qt)
                    for c in range(j * group, (j + 1) * group)]

        pending = [scores(j) for j in range(min(ahead, n_groups))]
        l8 = jnp.zeros((SUBLANES, tq), F32)
        pv = None
        for j in range(n_groups):
            if j + ahead < n_groups:
                pending.append(scores(j + ahead))
            for u, s in enumerate(pending.pop(0)):
                c = j * group + u
                off = (c % per_tile) * KV_CHUNK
                p = jnp.exp2(s)
                l8 = l8 + jnp.sum(p.reshape(KV_CHUNK // SUBLANES, SUBLANES, tq), axis=0)
                d = _dot(vt_ref[c // per_tile, r0:r0 + nr, off:off + KV_CHUNK], p.astype(BF16))
                pv = d if pv is None else pv + d
        emit(g, pv * (1.0 / jnp.sum(l8, axis=0, keepdims=True)))


def _stream_loop_t(qz_ref, k_ref, vt_ref, nr, acc_sc):
    n_maps, tq, _ = qz_ref.shape
    n_tiles, _, tile = vt_ref.shape
    per_tile = tile // KV_CHUNK
    group = math.gcd(LOOP_GROUP, n_tiles * per_tile)
    assert group % per_tile == 0
    outs = []
    for g in range(n_maps):
        qg = qz_ref[g]
        acc_sc[g] = jnp.zeros((nr, tq), F32)

        def body(i, l8, g=g, qg=qg):
            where = [(i * (group // per_tile) + u // per_tile, (u % per_tile) * KV_CHUNK)
                     for u in range(group)]
            scores = []
            for t, off in where:
                kc = k_ref[pl.ds(pl.multiple_of(t * tile + off, KV_CHUNK), KV_CHUNK), :]
                scores.append(_dot_nt(kc, qg))
            pv = None
            for (t, off), s in zip(where, scores):
                p = jnp.exp2(s)
                l8 = l8 + jnp.sum(p.reshape(KV_CHUNK // SUBLANES, SUBLANES, tq), axis=0)
                d = _dot(vt_ref[t, 0:nr, off:off + KV_CHUNK], p.astype(BF16))
                pv = d if pv is None else pv + d
            acc_sc[g] += pv
            return l8

        l8 = lax.fori_loop(0, n_tiles * per_tile // group, body, jnp.zeros((SUBLANES, tq), F32))
        outs.append(acc_sc[g] * (1.0 / jnp.sum(l8, axis=0, keepdims=True)))
    return outs


def _axial_kernel(qz_ref, k_ref, vt_ref, mqz_ref, mk_ref, mvt_ref, g1_ref, gm_ref, y1_ref, ym_ref,
                  *scratch, bounded):
    v_rows = lambda g: ((g // GQA_GROUP) * HEAD, HEAD)
    if bounded:
        mem_pairs = _mem_scores(mqz_ref, mk_ref)
        ym = _mem_finish(mem_pairs, mvt_ref).T
        ym_ref[...] = (ym * gm_ref[...]).astype(BF16)
        held = {}

        def emit(g, o):
            held[g] = o
            if g % 2 == 1:
                cols = slice((g // 2) * LANES, (g // 2 + 1) * LANES)
                y = jnp.concatenate([held.pop(g - 1), held.pop(g)], axis=0).T
                y1_ref[:, cols] = (y * g1_ref[:, cols]).astype(BF16)

        _stream_t(qz_ref, k_ref, vt_ref, v_rows, emit, group=1, ahead=KV_SKEW)
    else:
        ym = _mem_attention_t(mqz_ref, mk_ref, mvt_ref).T
        ym_ref[...] = (ym * gm_ref[...]).astype(BF16)
        m_sc, l_sc, acc_sc = scratch
        _flash_t(qz_ref, k_ref, vt_ref, v_rows, m_sc, l_sc, acc_sc)
        outs = [acc_sc[g] * (1.0 / l_sc[g]) for g in range(N_Q_HEADS)]
        y1 = jnp.concatenate(outs, axis=0).T
        y1_ref[...] = (y1 * g1_ref[...]).astype(BF16)


def _axial_attention(qz, k, vt, mqz, mk, mvt, gate, batch, seq, bounded):
    t = k.shape[0]
    tq = STREAM_Q_TILE if bounded else Q_TILE
    nq = seq // tq
    n_chunks = seq // ROW_TILE
    n_mem = mk.shape[1]
    tok = lambda b, q: b * nq + q
    scratch = [] if bounded else [
        pltpu.VMEM((N_Q_HEADS, 1, tq), F32),
        pltpu.VMEM((N_Q_HEADS, 1, tq), F32),
        pltpu.VMEM((N_Q_HEADS, HEAD, tq), F32),
    ]
    return pl.pallas_call(
        functools.partial(_axial_kernel, bounded=bounded),
        grid=(batch, nq),
        in_specs=[
            pl.BlockSpec((N_Q_HEADS, tq, LANES), lambda b, q: (0, tok(b, q), 0)),
            pl.BlockSpec((seq, LANES), lambda b, q: (b, 0)),
            pl.BlockSpec((n_chunks, LANES, ROW_TILE), lambda b, q: (b, 0, 0)),
            pl.BlockSpec((MEM_HEADS, tq, LANES), lambda b, q: (0, tok(b, q), 0)),
            pl.BlockSpec((1, n_mem, MEM_HEADS * HEAD), lambda b, q: (b, 0, 0)),
            pl.BlockSpec((1, MEM_HEADS * HEAD, n_mem), lambda b, q: (b, 0, 0)),
            pl.BlockSpec((tq, 512), lambda b, q: (tok(b, q), 0)),
            pl.BlockSpec((tq, 256), lambda b, q: (tok(b, q), 4)),
        ],
        out_specs=[
            pl.BlockSpec((tq, 512), lambda b, q: (tok(b, q), 0)),
            pl.BlockSpec((tq, 256), lambda b, q: (tok(b, q), 0)),
        ],
        out_shape=[
            jax.ShapeDtypeStruct((t, 512), BF16),
            jax.ShapeDtypeStruct((t, 256), BF16),
        ],
        scratch_shapes=scratch,
        compiler_params=_params(2),
        name="axial_attention" if bounded else "axial_attention_online",
    )(qz, k, vt, mqz, mk, mvt, gate, gate)


def _diff_kernel(qz_ref, k_ref, vt_ref, lam_ref, sg_ref, g2_ref, y2_ref, *scratch, lambda_init, bounded):
    v_rows = lambda g: (0, 2 * HEAD)
    if bounded:
        o0, o1 = _stream_loop_t(qz_ref, k_ref, vt_ref, 2 * HEAD, *scratch)
    else:
        m_sc, l_sc, acc_sc = scratch
        _flash_t(qz_ref, k_ref, vt_ref, v_rows, m_sc, l_sc, acc_sc)
        o0, o1 = (acc_sc[g] * (1.0 / l_sc[g]) for g in range(2))
    lv = lam_ref[...]
    lam = (jnp.exp(jnp.sum(lv[0:1] * lv[1:2], axis=-1, keepdims=True))
           - jnp.exp(jnp.sum(lv[2:3] * lv[3:4], axis=-1, keepdims=True)) + lambda_init)
    o = o0 - lam * o1
    ms = jnp.mean(o * o, axis=0, keepdims=True)
    on = (o * lax.rsqrt(ms + EPS)).T
    y2_ref[...] = (on * sg_ref[...] * (1.0 - lambda_init) * g2_ref[...]).astype(BF16)


def _diff_attention(dqz, dk, dvt, lam, subln_g, gate, batch, seq, layer, bounded):
    t = dk.shape[0]
    tq = DIFF_Q_TILE if bounded else Q_TILE
    nq = seq // tq
    n_chunks = seq // ROW_TILE
    lambda_init = 0.8 - 0.6 * math.exp(-0.3 * layer)
    tok = lambda b, q: b * nq + q
    scratch = [pltpu.VMEM((2, 2 * HEAD, tq), F32)] if bounded else [
        pltpu.VMEM((2, 1, tq), F32),
        pltpu.VMEM((2, 1, tq), F32),
        pltpu.VMEM((2, 2 * HEAD, tq), F32),
    ]
    return pl.pallas_call(
        functools.partial(_diff_kernel, lambda_init=lambda_init, bounded=bounded),
        grid=(batch, DIFF_HEADS, nq),
        in_specs=[
            pl.BlockSpec((2, tq, LANES), lambda b, h, q: (h, tok(b, q), 0)),
            pl.BlockSpec((seq, LANES), lambda b, h, q: (b, h)),
            pl.BlockSpec((n_chunks, LANES, ROW_TILE), lambda b, h, q: (b, h, 0)),
            pl.BlockSpec((4, HEAD), lambda b, h, q: (0, 0)),
            pl.BlockSpec((1, 2 * HEAD), lambda b, h, q: (0, 0)),
            pl.BlockSpec((tq, LANES), lambda b, h, q: (tok(b, q), 4 + h)),
        ],
        out_specs=pl.BlockSpec((tq, LANES), lambda b, h, q: (tok(b, q), h)),
        out_shape=jax.ShapeDtypeStruct((t, 512), BF16),
        scratch_shapes=scratch,
        compiler_params=_params(3),
        name="diff_attention" if bounded else "diff_attention_online",
    )(dqz, dk, dvt, lam, subln_g.reshape(1, 2 * HEAD), gate)


def _out_proj_kernel(y1_ref, y2_ref, ym_ref, x_ref, w_ref, o_ref):
    acc = _dot(y1_ref[...], w_ref[0:512, :])
    acc = acc + _dot(y2_ref[...], w_ref[512:1024, :])
    acc = acc + _dot(ym_ref[...], w_ref[1024:MIX_W, :])
    o_ref[...] = x_ref[...] + acc


def _out_proj(y1, y2, ym, x2, w):
    t = x2.shape[0]
    return pl.pallas_call(
        _out_proj_kernel,
        grid=(t // OUT_ROW_TILE,),
        in_specs=[
            pl.BlockSpec((OUT_ROW_TILE, 512), lambda i: (i, 0)),
            pl.BlockSpec((OUT_ROW_TILE, 512), lambda i: (i, 0)),
            pl.BlockSpec((OUT_ROW_TILE, 256), lambda i: (i, 0)),
            pl.BlockSpec((OUT_ROW_TILE, D_MODEL), lambda i: (i, 0)),
            pl.BlockSpec((MIX_W, D_MODEL), lambda i: (0, 0)),
        ],
        out_specs=pl.BlockSpec((OUT_ROW_TILE, D_MODEL), lambda i: (i, 0)),
        out_shape=jax.ShapeDtypeStruct((t, D_MODEL), F32),
        compiler_params=_params(1),
        name="out_proj",
    )(y1, y2, ym, x2, w)


def _rope_angles(pos, dim):
    inv = ROPE_THETA ** (-jnp.arange(0, dim, 2, dtype=F32) / dim)
    return pos.astype(F32)[:, None] * inv[None, :]


def _rope_tables(seq):
    pos = jnp.arange(seq)
    a1 = _rope_angles(pos, HEAD)
    cos1 = jnp.concatenate([jnp.cos(a1), jnp.cos(a1)], axis=-1)
    sin1 = jnp.concatenate([-jnp.sin(a1), jnp.sin(a1)], axis=-1)
    ar = _rope_angles(pos // GRID_W, HEAD // 2)
    ac = _rope_angles(pos % GRID_W, HEAD // 2)
    cosa = jnp.concatenate([jnp.cos(ar), jnp.cos(ar), jnp.cos(ac), jnp.cos(ac)], axis=-1)
    sina = jnp.concatenate([-jnp.sin(ar), jnp.sin(ar), -jnp.sin(ac), jnp.sin(ac)], axis=-1)
    rep = lambda a: jnp.tile(a, (1, LANES // HEAD))
    return rep(cos1), rep(sin1), rep(cosa), rep(sina)


def _scores_bounded(qk_g):
    bound = HEAD * Q_SCALE * 1.01 * jnp.max(jnp.abs(qk_g[0])) * jnp.max(jnp.abs(qk_g[1]))
    return bound <= SAFE_SCORE_BOUND


def _tile_gain(g, width):
    return jnp.tile(g.astype(F32), width // HEAD).reshape(1, width)


def _trunk(x, mem, p):
    batch, seq, _ = x.shape
    x2 = x.reshape(batch * seq, D_MODEL)
    cos1, sin1, cosa, sina = _rope_tables(seq)
    bd = p["bd"]
    gk_mem = jnp.stack([_tile_gain(p["mem_qk_g"][l, 1], 256) for l in range(DEPTH)])
    mk_all, mvt_all = _memkv(mem, p["mem_norm_g"], p["w_mem_kv"], gk_mem, bd)
    for l in range(DEPTH):
        g_in, w_in = p["norm_g"][l].reshape(1, D_MODEL), p["w_in"][l]
        gmq = _tile_gain(p["mem_qk_g"][l, 0], 256)
        mk, mvt = mk_all[l], mvt_all[l]
        if l % 2 == 0:
            e = l // 2
            y1, qz, k, vt, mqz, gate = _proj_even(
                x2, seq, g_in, w_in, cos1, sin1, _tile_gain(p["swa_qk_g"][e, 0], 512),
                _tile_gain(p["swa_qk_g"][e, 1], LANES), gmq, p["conv_w"][e], bd)
            sink_ok = jnp.max(jnp.abs(p["swa_sink"][e])) * LOG2E <= SAFE_SCORE_BOUND
            y2, ym = lax.cond(
                _scores_bounded(p["swa_qk_g"][e]) & _scores_bounded(p["mem_qk_g"][l]) & sink_ok,
                functools.partial(_window_attention, batch=batch, seq=seq, bounded=True),
                functools.partial(_window_attention, batch=batch, seq=seq, bounded=False),
                p["swa_sink"][e], qz, k, vt, mqz, mk, mvt, gate)
        else:
            o = l // 2
            qz, k, vt, dqz, dk, dvt, mqz, gate = _proj_odd(
                x2, seq, g_in, w_in, cos1, sin1, cosa, sina, _tile_gain(p["ax_qk_g"][o, 0], 512),
                _tile_gain(p["ax_qk_g"][o, 1], LANES), _tile_gain(p["diff_qk_g"][o, 0], 512),
                _tile_gain(p["diff_qk_g"][o, 1], 512), gmq, bd)
            y1, ym = lax.cond(
                _scores_bounded(p["ax_qk_g"][o]) & _scores_bounded(p["mem_qk_g"][l]),
                functools.partial(_axial_attention, batch=batch, seq=seq, bounded=True),
                functools.partial(_axial_attention, batch=batch, seq=seq, bounded=False),
                qz, k, vt, mqz, mk, mvt, gate)
            y2 = lax.cond(
                _scores_bounded(p["diff_qk_g"][o]),
                functools.partial(_diff_attention, batch=batch, seq=seq, layer=l, bounded=True),
                functools.partial(_diff_attention, batch=batch, seq=seq, layer=l, bounded=False),
                dqz, dk, dvt, p["diff_lambda"][o], p["diff_subln_g"][o], gate)
        x2 = _out_proj(y1, y2, ym, x2, p["w_out"][l])
    return x2.reshape(batch, seq, D_MODEL)


def kernel(x_prompt, x_sample, mem_prompt, mem_sample, norm_g, w_in, w_out, mem_norm_g, w_mem_kv,
           mem_qk_g, conv_w, swa_qk_g, swa_sink, ax_qk_g, diff_qk_g, diff_lambda, diff_subln_g):
    grp = jnp.arange(2 * LANES) // HEAD
    p = dict(
        norm_g=norm_g, w_in=w_in.astype(BF16), w_out=w_out.astype(BF16), mem_norm_g=mem_norm_g,
        w_mem_kv=w_mem_kv.astype(BF16), mem_qk_g=mem_qk_g, conv_w=conv_w, swa_qk_g=swa_qk_g,
        swa_sink=swa_sink, ax_qk_g=ax_qk_g, diff_qk_g=diff_qk_g, diff_lambda=diff_lambda,
        diff_subln_g=diff_subln_g, bd=(grp[:, None] == grp[None, :]).astype(BF16))
    return (_trunk(x_prompt, mem_prompt, p), _trunk(x_sample, mem_sample, p))
```

```python
import functools
import math

import jax
import jax.numpy as jnp
from jax import lax
from jax.experimental import pallas as pl
from jax.experimental.pallas import tpu as pltpu

F32 = jnp.float32
BF16 = jnp.bfloat16

D_MODEL = 1024
DEPTH = 4
HEAD = 64
LANES = 128
SUBLANES = 8
N_Q_HEADS = 8
GQA_GROUP = 4
DIFF_HEADS = 4
MEM_HEADS = 4
CONV_W = 512
WINDOW = 128
GRID_W = 64
ROPE_THETA = 10000.0
EPS = 1e-6
NEG = -1e30
LOG2E = 1.4426950408889634
Q_SCALE = HEAD ** -0.5 * LOG2E
SAFE_SCORE_BOUND = 60.0

IN_W = 3840
MIX_W = 1280
GATE_OFF = 2560
MQ_OFF = 2304

ROW_TILE = 512
OUT_ROW_TILE = 1024
Q_TILE = 256
STREAM_Q_TILE = 512
WINDOW_Q_TILE = 1024
DIFF_Q_TILE = 1024
KV_CHUNK = 256
LOOP_GROUP = 32
WINDOW_SKEW = 2
KV_SKEW = 3
VMEM_LIMIT = 56 * 1024 * 1024

_NT = (((1,), (1,)), ((), ()))


def _params(n_axes):
    return pltpu.CompilerParams(dimension_semantics=("arbitrary",) * n_axes, vmem_limit_bytes=VMEM_LIMIT)


def _dot(a, b):
    return jnp.dot(a, b, preferred_element_type=F32)


def _dot_nt(a, b):
    return lax.dot_general(a, b, _NT, preferred_element_type=F32)


def _rms_rows(x, g):
    ms = jnp.mean(x * x, axis=-1, keepdims=True)
    return x * lax.rsqrt(ms + EPS) * g


def _head_sumsq(x, bd):
    w = x.shape[1]
    outs = []
    step = 2 * LANES if w % (2 * LANES) == 0 else LANES
    for c in range(0, w, step):
        sq = x[:, c:c + step]
        outs.append(_dot((sq * sq).astype(BF16), bd[:step, :step]))
    return outs[0] if len(outs) == 1 else jnp.concatenate(outs, axis=1)


def _head_rms(x, g, bd):
    return x * lax.rsqrt(_head_sumsq(x, bd) * (1.0 / HEAD) + EPS) * g


def _rope(x, cos, sin_signed, half):
    r = x.shape[0]
    lane = lax.broadcasted_iota(jnp.int32, (r, LANES), 1)
    first = (lane & (2 * half - 1)) < half
    outs = []
    for c in range(0, x.shape[1], LANES):
        xc = x[:, c:c + LANES]
        partner = jnp.where(first, pltpu.roll(xc, LANES - half, 1), pltpu.roll(xc, half, 1))
        outs.append(xc * cos + partner * sin_signed)
    return outs[0] if len(outs) == 1 else jnp.concatenate(outs, axis=1)


def _place_heads(x, target_half, out_ref):
    r = x.shape[0]
    low = lax.broadcasted_iota(jnp.int32, (r, LANES), 1) < HEAD
    for h in range(x.shape[1] // HEAD):
        xc = x[:, (h // 2) * LANES:(h // 2 + 1) * LANES]
        th = target_half(h)
        if th != h % 2:
            xc = pltpu.roll(xc, HEAD, 1)
        keep = low if th == 0 else jnp.logical_not(low)
        out_ref[h] = jnp.where(keep, xc, 0.0).astype(BF16)


def _silu(z):
    return z * (1.0 / (1.0 + jnp.exp(-z)))


def _project(h, w_ref, splits):
    return [_dot(h, w_ref[:, a:b]) for a, b in splits]


def _memkv_kernel(mem_ref, g_ref, w_ref, gk_ref, bd_ref, mk_ref, mvt_ref):
    h = _rms_rows(mem_ref[0], g_ref[0]).astype(BF16)
    mkv = _dot(h, w_ref[0])
    half = mkv.shape[1] // 2
    mk = _head_rms(mkv[:, :half], gk_ref[0], bd_ref[...])
    mk_ref[0, 0] = mk.astype(BF16)
    mvt_ref[0, 0] = mkv[:, half:].T.astype(BF16)


def _memkv(mem, mem_norm_g, w_mem_kv, gk_tiled, bd):
    b, n_mem, _ = mem.shape
    width = MEM_HEADS * HEAD
    return pl.pallas_call(
        _memkv_kernel,
        grid=(DEPTH, b),
        in_specs=[
            pl.BlockSpec((1, n_mem, D_MODEL), lambda l, i: (i, 0, 0)),
            pl.BlockSpec((1, 1, D_MODEL), lambda l, i: (l, 0, 0)),
            pl.BlockSpec((1, D_MODEL, 2 * width), lambda l, i: (l, 0, 0)),
            pl.BlockSpec((1, 1, width), lambda l, i: (l, 0, 0)),
            pl.BlockSpec((2 * LANES, 2 * LANES), lambda l, i: (0, 0)),
        ],
        out_specs=[
            pl.BlockSpec((1, 1, n_mem, width), lambda l, i: (l, i, 0, 0)),
            pl.BlockSpec((1, 1, width, n_mem), lambda l, i: (l, i, 0, 0)),
        ],
        out_shape=[
            jax.ShapeDtypeStruct((DEPTH, b, n_mem, width), BF16),
            jax.ShapeDtypeStruct((DEPTH, b, width, n_mem), BF16),
        ],
        compiler_params=_params(2),
        name="mem_kv",
    )(mem, mem_norm_g.reshape(DEPTH, 1, D_MODEL), w_mem_kv, gk_tiled, bd)


def _proj_even_kernel(x_ref, xp_ref, xn_ref, g_ref, w_ref, cos_ref, sin_ref, gq_ref, gk_ref,
                      gmq_ref, cw_ref, bd_ref,
                      y1_ref, qz_ref, k_ref, vt_ref, mqz_ref, gate_ref, conv_sc, *, tiles_per_seq):
    i = pl.program_id(0)
    pos_tile = i % tiles_per_seq
    bd = bd_ref[...]
    cos = cos_ref[...]
    sin = sin_ref[...]
    rows = x_ref.shape[0]

    x_ext = jnp.concatenate([x_ref[...], xp_ref[...], xn_ref[...]], axis=0)
    h_ext = _rms_rows(x_ext, g_ref[...]).astype(BF16)
    h = h_ext[:rows]
    proj = lambda a, b: _dot(h, w_ref[:, a:b])

    z = proj(GATE_OFF, IN_W)
    gchc = _dot(h_ext, w_ref[:, 512:1536])
    gate_ref[...] = _silu(z[:, 512:MIX_W])
    gb = proj(0, 512)
    q = proj(1536, 2048)

    inner_ext = gchc[:, 0:512] * gchc[:, 512:1024]
    inner = inner_ext[:rows]
    prev_row = inner_ext[rows + SUBLANES - 1:rows + SUBLANES]
    next_row = inner_ext[rows + SUBLANES:rows + SUBLANES + 1]
    prev_row = jnp.where(pos_tile == 0, 0.0, prev_row)
    next_row = jnp.where(pos_tile == tiles_per_seq - 1, 0.0, next_row)
    conv_sc[pl.ds(SUBLANES, rows), :] = inner
    conv_sc[pl.ds(SUBLANES - 1, 1), :] = prev_row
    conv_sc[pl.ds(SUBLANES + rows, 1), :] = next_row
    cw = cw_ref[...]
    conv = (conv_sc[pl.ds(SUBLANES - 1, rows), :] * cw[0:1, :] + inner * cw[1:2, :]
            + conv_sc[pl.ds(SUBLANES + 1, rows), :] * cw[2:3, :])
    y1_ref[...] = (gb * conv * _silu(z[:, 0:512])).astype(BF16)

    mq = proj(MQ_OFF, GATE_OFF)
    q = _rope(_head_rms(q, gq_ref[...], bd), cos, sin, HEAD // 2) * Q_SCALE
    _place_heads(q, lambda h: h // GQA_GROUP, qz_ref)

    kv = proj(2048, 2304)
    mq = _head_rms(mq, gmq_ref[...], bd) * Q_SCALE
    _place_heads(mq, lambda h: h % 2, mqz_ref)

    k = _rope(_head_rms(kv[:, 0:LANES], gk_ref[...], bd), cos, sin, HEAD // 2)
    k_ref[...] = k.astype(BF16)
    vt_ref[0] = kv[:, LANES:2 * LANES].T.astype(BF16)


def _proj_even(x2, seq, g, w, cos, sin, gq, gk, gmq, conv_w, bd):
    t = x2.shape[0]
    n_tiles = t // ROW_TILE
    tiles_per_seq = seq // ROW_TILE
    sub_per_tile = ROW_TILE // SUBLANES
    last_sub = t // SUBLANES - 1
    row = lambda w: pl.BlockSpec((1, w), lambda i: (0, 0))
    tab = pl.BlockSpec((ROW_TILE, LANES), lambda i: (i % tiles_per_seq, 0))
    return pl.pallas_call(
        functools.partial(_proj_even_kernel, tiles_per_seq=tiles_per_seq),
        grid=(n_tiles,),
        in_specs=[
            pl.BlockSpec((ROW_TILE, D_MODEL), lambda i: (i, 0)),
            pl.BlockSpec((SUBLANES, D_MODEL), lambda i: (jnp.maximum(i * sub_per_tile - 1, 0), 0)),
            pl.BlockSpec((SUBLANES, D_MODEL), lambda i: (jnp.minimum((i + 1) * sub_per_tile, last_sub), 0)),
            row(D_MODEL),
            pl.BlockSpec((D_MODEL, IN_W), lambda i: (0, 0)),
            tab, tab, row(512), row(LANES), row(256),
            pl.BlockSpec((3, CONV_W), lambda i: (0, 0)),
            pl.BlockSpec((2 * LANES, 2 * LANES), lambda i: (0, 0)),
        ],
        out_specs=[
            pl.BlockSpec((ROW_TILE, CONV_W), lambda i: (i, 0)),
            pl.BlockSpec((N_Q_HEADS, ROW_TILE, LANES), lambda i: (0, i, 0)),
            pl.BlockSpec((ROW_TILE, LANES), lambda i: (i, 0)),
            pl.BlockSpec((1, LANES, ROW_TILE), lambda i: (i, 0, 0)),
            pl.BlockSpec((MEM_HEADS, ROW_TILE, LANES), lambda i: (0, i, 0)),
            pl.BlockSpec((ROW_TILE, 768), lambda i: (i, 0)),
        ],
        out_shape=[
            jax.ShapeDtypeStruct((t, CONV_W), BF16),
            jax.ShapeDtypeStruct((N_Q_HEADS, t, LANES), BF16),
            jax.ShapeDtypeStruct((t, LANES), BF16),
            jax.ShapeDtypeStruct((n_tiles, LANES, ROW_TILE), BF16),
            jax.ShapeDtypeStruct((MEM_HEADS, t, LANES), BF16),
            jax.ShapeDtypeStruct((t, 768), F32),
        ],
        scratch_shapes=[pltpu.VMEM((ROW_TILE + 2 * SUBLANES, CONV_W), F32)],
        compiler_params=_params(1),
        name="proj_even",
    )(x2, x2, x2, g, w, cos, sin, gq, gk, gmq, conv_w, bd)


def _mem_scores(mqz_ref, mk_ref):
    pairs = []
    for grp in range(MEM_HEADS // 2):
        q2 = jnp.concatenate([mqz_ref[2 * grp], mqz_ref[2 * grp + 1]], axis=0)
        pairs.append(_dot_nt(mk_ref[0, :, grp * LANES:(grp + 1) * LANES], q2))
    return pairs


def _mem_finish(pairs, mvt_ref):
    tq = pairs[0].shape[1] // 2
    outs = []
    for h in range(MEM_HEADS):
        p = jnp.exp2(pairs[h // 2][:, (h % 2) * tq:(h % 2 + 1) * tq])
        l = jnp.sum(p, axis=0, keepdims=True)
        o = _dot(mvt_ref[0, h * HEAD:(h + 1) * HEAD, :], p.astype(BF16))
        outs.append(o * (1.0 / l))
    return jnp.concatenate(outs, axis=0)


def _mem_attention_t(mqz_ref, mk_ref, mvt_ref):
    outs = []
    for h in range(MEM_HEADS):
        grp = h // 2
        s = _dot_nt(mk_ref[0, :, grp * LANES:(grp + 1) * LANES], mqz_ref[h])
        m = jnp.max(s, axis=0, keepdims=True)
        p = jnp.exp2(s - m)
        l = jnp.sum(p, axis=0, keepdims=True)
        o = _dot(mvt_ref[0, h * HEAD:(h + 1) * HEAD, :], p.astype(BF16))
        outs.append(o * (1.0 / l))
    return jnp.concatenate(outs, axis=0)


def _window_kernel(sink_ref, qz_ref, kp_ref, kc_ref, kn_ref, vp_ref, vc_ref, vn_ref, mqz_ref, mk_ref,
                   mvt_ref, g2_ref, gm_ref, y2_ref, ym_ref, *, seq):
    tq = qz_ref.shape[1]
    t0 = pl.program_id(1) * tq
    kband = jnp.concatenate([kp_ref[...], kc_ref[...], kn_ref[...]], axis=0)
    vband = jnp.concatenate([vp_ref[0]] + [vc_ref[c] for c in range(vc_ref.shape[0])] + [vn_ref[0]],
                            axis=1)
    nk = tq + 2 * WINDOW
    j = lax.broadcasted_iota(jnp.int32, (nk, tq), 0)
    i = lax.broadcasted_iota(jnp.int32, (nk, tq), 1)
    kpos = j + (t0 - WINDOW)
    mask = (j >= i) & (j <= i + 2 * WINDOW) & (kpos >= 0) & (kpos < seq)
    outs = []
    for g in range(N_Q_HEADS):
        kv = g // GQA_GROUP
        s = jnp.where(mask, _dot_nt(kband, qz_ref[g]), NEG)
        sink = sink_ref[g] * LOG2E
        m = jnp.maximum(jnp.max(s, axis=0, keepdims=True), sink)
        p = jnp.exp2(s - m)
        l = jnp.sum(p, axis=0, keepdims=True) + jnp.exp2(sink - m)
        o = _dot(vband[kv * HEAD:(kv + 1) * HEAD, :], p.astype(BF16))
        outs.append(o * (1.0 / l))
    y2 = jnp.concatenate(outs, axis=0).T
    y2_ref[...] = (y2 * g2_ref[...]).astype(BF16)
    ym = _mem_attention_t(mqz_ref, mk_ref, mvt_ref).T
    ym_ref[...] = (ym * gm_ref[...]).astype(BF16)


def _window_out_kernel(sink_ref, qz_ref, kp_ref, kc_ref, kn_ref, vp_ref, vc_ref, vn_ref, mqz_ref,
                       mk_ref, mvt_ref, g2_ref, gm_ref, y1_ref, x_ref, w_ref, o_ref, y2_sc, ym_sc,
                       *, seq, n_tiles, tiles_per_seq):
    step = pl.program_id(0)
    tq = qz_ref.shape[1]
    t0 = (jnp.minimum(step, n_tiles - 1) % tiles_per_seq) * tq

    @pl.when(step == 0)
    def _():
        y2_sc[...] = jnp.zeros(y2_sc.shape, BF16)
        ym_sc[...] = jnp.zeros(ym_sc.shape, BF16)

    y2_prev = y2_sc[...]
    ym_prev = ym_sc[...]
    kband = jnp.concatenate([kp_ref[...], kc_ref[...], kn_ref[...]], axis=0)
    vband = jnp.concatenate([vp_ref[0]] + [vc_ref[c] for c in range(vc_ref.shape[0])] + [vn_ref[0]],
                            axis=1)
    nk = 3 * WINDOW
    j = lax.broadcasted_iota(jnp.int32, (nk, WINDOW), 0)
    i = lax.broadcasted_iota(jnp.int32, (nk, WINDOW), 1)
    in_band = (j >= i) & (j <= i + 2 * WINDOW)
    sink_p = [jnp.exp2(jnp.full((1, WINDOW), sink_ref[g] * LOG2E, F32)) for g in range(N_Q_HEADS)]
    n_kv = N_Q_HEADS // GQA_GROUP
    units = [(a, kv) for a in range(tq // WINDOW) for kv in range(n_kv)]

    def stacked_qt(a, kv):
        q4 = jnp.concatenate([qz_ref[kv * GQA_GROUP + r, a * WINDOW:(a + 1) * WINDOW, :]
                              for r in range(GQA_GROUP)], axis=0)
        return q4.astype(F32).T.astype(BF16)

    qts = {u: stacked_qt(*u) for u in units}

    def score(a, kv):
        return _dot(kband[a * WINDOW:a * WINDOW + nk], qts[(a, kv)])

    pending = [score(*u) for u in units[:WINDOW_SKEW]]
    mem_pairs = _mem_scores(mqz_ref, mk_ref)
    acc = _dot(y1_ref[...], w_ref[0:512, :])
    acc = acc + _dot(y2_prev, w_ref[512:1024, :])
    acc = acc + _dot(ym_prev, w_ref[1024:MIX_W, :])
    o_ref[...] = x_ref[...] + acc
    blocks = [[None] * (tq // WINDOW) for _ in range(N_Q_HEADS)]
    for n, (a, kv) in enumerate(units):
        if n + WINDOW_SKEW < len(units):
            pending.append(score(*units[n + WINDOW_SKEW]))
        kpos = j + (t0 + (a - 1) * WINDOW)
        bias1 = jnp.where(in_band & (kpos >= 0) & (kpos < seq), 0.0, NEG)
        bias = jnp.concatenate([bias1] * GQA_GROUP, axis=1)
        p = jnp.exp2(pending.pop(0) + bias)
        sinks = jnp.concatenate(sink_p[kv * GQA_GROUP:(kv + 1) * GQA_GROUP], axis=1)
        l = jnp.sum(p, axis=0, keepdims=True) + sinks
        o = _dot(vband[kv * HEAD:(kv + 1) * HEAD, a * WINDOW:a * WINDOW + nk], p.astype(BF16))
        o = o * (1.0 / l)
        for r in range(GQA_GROUP):
            blocks[kv * GQA_GROUP + r][a] = o[:, r * WINDOW:(r + 1) * WINDOW]
    y2 = jnp.concatenate([jnp.concatenate(b, axis=1) for b in blocks], axis=0).T
    y2_sc[...] = (y2 * g2_ref[...]).astype(BF16)
    ym = _mem_finish(mem_pairs, mvt_ref).T
    ym_sc[...] = (ym * gm_ref[...]).astype(BF16)


def _window_out_proj(sink, qz, k, vt, mqz, mk, mvt, gate, y1, x2, w_out, batch, seq):
    t = k.shape[0]
    tq = WINDOW_Q_TILE
    nq = seq // tq
    n_tiles = batch * nq
    w_per_q = tq // WINDOW
    w_per_seq = seq // WINDOW
    w_per_chunk = ROW_TILE // WINDOW
    n_mem = mk.shape[1]
    att = lambda i: jnp.minimum(i, n_tiles - 1)
    out = lambda i: jnp.maximum(i - 1, 0)

    def prev_w(i):
        b, q = att(i) // nq, att(i) % nq
        return b * w_per_seq + jnp.maximum(q * w_per_q - 1, 0)

    def next_w(i):
        b, q = att(i) // nq, att(i) % nq
        return b * w_per_seq + jnp.minimum((q + 1) * w_per_q, w_per_seq - 1)

    return pl.pallas_call(
        functools.partial(_window_out_kernel, seq=seq, n_tiles=n_tiles, tiles_per_seq=nq),
        grid=(n_tiles + 1,),
        in_specs=[
            pl.BlockSpec(memory_space=pltpu.SMEM),
            pl.BlockSpec((N_Q_HEADS, tq, LANES), lambda i: (0, att(i), 0)),
            pl.BlockSpec((WINDOW, LANES), lambda i: (prev_w(i), 0)),
            pl.BlockSpec((tq, LANES), lambda i: (att(i), 0)),
            pl.BlockSpec((WINDOW, LANES), lambda i: (next_w(i), 0)),
            pl.BlockSpec((1, LANES, WINDOW), lambda i: (prev_w(i) // w_per_chunk, 0, prev_w(i) % w_per_chunk)),
            pl.BlockSpec((tq // ROW_TILE, LANES, ROW_TILE), lambda i: (att(i), 0, 0)),
            pl.BlockSpec((1, LANES, WINDOW), lambda i: (next_w(i) // w_per_chunk, 0, next_w(i) % w_per_chunk)),
            pl.BlockSpec((MEM_HEADS, tq, LANES), lambda i: (0, att(i), 0)),
            pl.BlockSpec((1, n_mem, MEM_HEADS * HEAD), lambda i: (att(i) // nq, 0, 0)),
            pl.BlockSpec((1, MEM_HEADS * HEAD, n_mem), lambda i: (att(i) // nq, 0, 0)),
            pl.BlockSpec((tq, 512), lambda i: (att(i), 0)),
            pl.BlockSpec((tq, 256), lambda i: (att(i), 2)),
            pl.BlockSpec((tq, 512), lambda i: (out(i), 0)),
            pl.BlockSpec((tq, D_MODEL), lambda i: (out(i), 0)),
            pl.BlockSpec((MIX_W, D_MODEL), lambda i: (0, 0)),
        ],
        out_specs=pl.BlockSpec((tq, D_MODEL), lambda i: (out(i), 0)),
        out_shape=jax.ShapeDtypeStruct((t, D_MODEL), F32),
        scratch_shapes=[pltpu.VMEM((tq, 512), BF16), pltpu.VMEM((tq, 256), BF16)],
        compiler_params=_params(1),
        name="window_out_proj",
    )(sink, qz, k, k, k, vt, vt, vt, mqz, mk, mvt, gate, gate, y1, x2, w_out)


def _window_attention(sink, qz, k, vt, mqz, mk, mvt, gate, batch, seq):
    t = k.shape[0]
    tq = Q_TILE
    nq = seq // tq
    w_per_q = tq // WINDOW
    w_per_seq = seq // WINDOW
    w_per_chunk = ROW_TILE // WINDOW
    n_mem = mk.shape[1]
    q_per_chunk = ROW_TILE // tq
    cur_v = pl.BlockSpec((1, LANES, tq),
                         lambda b, q: ((b * nq + q) // q_per_chunk, 0, (b * nq + q) % q_per_chunk))

    def prev_w(b, q):
        return b * w_per_seq + jnp.maximum(q * w_per_q - 1, 0)

    def next_w(b, q):
        return b * w_per_seq + jnp.minimum((q + 1) * w_per_q, w_per_seq - 1)

    tok = lambda b, q: b * nq + q
    return pl.pallas_call(
        functools.partial(_window_kernel, seq=seq),
        grid=(batch, nq),
        in_specs=[
            pl.BlockSpec(memory_space=pltpu.SMEM),
            pl.BlockSpec((N_Q_HEADS, tq, LANES), lambda b, q: (0, tok(b, q), 0)),
            pl.BlockSpec((WINDOW, LANES), lambda b, q: (prev_w(b, q), 0)),
            pl.BlockSpec((tq, LANES), lambda b, q: (tok(b, q), 0)),
            pl.BlockSpec((WINDOW, LANES), lambda b, q: (next_w(b, q), 0)),
            pl.BlockSpec((1, LANES, WINDOW), lambda b, q: (prev_w(b, q) // w_per_chunk, 0, prev_w(b, q) % w_per_chunk)),
            cur_v,
            pl.BlockSpec((1, LANES, WINDOW), lambda b, q: (next_w(b, q) // w_per_chunk, 0, next_w(b, q) % w_per_chunk)),
            pl.BlockSpec((MEM_HEADS, tq, LANES), lambda b, q: (0, tok(b, q), 0)),
            pl.BlockSpec((1, n_mem, MEM_HEADS * HEAD), lambda b, q: (b, 0, 0)),
            pl.BlockSpec((1, MEM_HEADS * HEAD, n_mem), lambda b, q: (b, 0, 0)),
            pl.BlockSpec((tq, 512), lambda b, q: (tok(b, q), 0)),
            pl.BlockSpec((tq, 256), lambda b, q: (tok(b, q), 2)),
        ],
        out_specs=[
            pl.BlockSpec((tq, 512), lambda b, q: (tok(b, q), 0)),
            pl.BlockSpec((tq, 256), lambda b, q: (tok(b, q), 0)),
        ],
        out_shape=[
            jax.ShapeDtypeStruct((t, 512), BF16),
            jax.ShapeDtypeStruct((t, 256), BF16),
        ],
        compiler_params=_params(2),
        name="window_attention_max",
    )(sink, qz, k, k, k, vt, vt, vt, mqz, mk, mvt, gate, gate)


def _proj_odd_kernel(x_ref, g_ref, w_ref, cos1_ref, sin1_ref, cosa_ref, sina_ref, gq_ref, gk_ref,
                     gdq_ref, gdk_ref, gmq_ref, bd_ref,
                     qz_ref, k_ref, vt_ref, dqz_ref, dk_ref, dvt_ref, mqz_ref, gate_ref):
    bd = bd_ref[...]
    cos1, sin1 = cos1_ref[...], sin1_ref[...]
    cosa, sina = cosa_ref[...], sina_ref[...]
    h = _rms_rows(x_ref[...], g_ref[...]).astype(BF16)
    proj = lambda a, b: _dot(h, w_ref[:, a:b])

    z = proj(GATE_OFF, IN_W)
    q = proj(0, 512)
    gate_ref[...] = _silu(z)

    dq = proj(768, 1280)
    q = _rope(_head_rms(q, gq_ref[...], bd), cosa, sina, HEAD // 4) * Q_SCALE
    _place_heads(q, lambda h: h // GQA_GROUP, qz_ref)

    dk = proj(1280, 1792)
    dq = _rope(_head_rms(dq, gdq_ref[...], bd), cos1, sin1, HEAD // 2) * Q_SCALE
    _place_heads(dq, lambda h: h % 2, dqz_ref)

    dv = proj(1792, 2304)
    dk = _rope(_head_rms(dk, gdk_ref[...], bd), cos1, sin1, HEAD // 2)
    dk_ref[...] = dk.astype(BF16)

    mq = proj(MQ_OFF, GATE_OFF)
    dvt_ref[0] = dv.T.astype(BF16)

    kv = proj(512, 768)
    mq = _head_rms(mq, gmq_ref[...], bd) * Q_SCALE
    _place_heads(mq, lambda h: h % 2, mqz_ref)

    k = _rope(_head_rms(kv[:, 0:LANES], gk_ref[...], bd), cosa, sina, HEAD // 4)
    k_ref[...] = k.astype(BF16)
    vt_ref[0] = kv[:, LANES:2 * LANES].T.astype(BF16)


def _proj_odd(x2, seq, g, w, cos1, sin1, cosa, sina, gq, gk, gdq, gdk, gmq, bd):
    t = x2.shape[0]
    n_tiles = t // ROW_TILE
    tiles_per_seq = seq // ROW_TILE
    row = lambda w: pl.BlockSpec((1, w), lambda i: (0, 0))
    tab = pl.BlockSpec((ROW_TILE, LANES), lambda i: (i % tiles_per_seq, 0))
    return pl.pallas_call(
        _proj_odd_kernel,
        grid=(n_tiles,),
        in_specs=[
            pl.BlockSpec((ROW_TILE, D_MODEL), lambda i: (i, 0)),
            row(D_MODEL),
            pl.BlockSpec((D_MODEL, IN_W), lambda i: (0, 0)),
            tab, tab, tab, tab, row(512), row(LANES), row(512), row(512), row(256),
            pl.BlockSpec((2 * LANES, 2 * LANES), lambda i: (0, 0)),
        ],
        out_specs=[
            pl.BlockSpec((N_Q_HEADS, ROW_TILE, LANES), lambda i: (0, i, 0)),
            pl.BlockSpec((ROW_TILE, LANES), lambda i: (i, 0)),
            pl.BlockSpec((1, LANES, ROW_TILE), lambda i: (i, 0, 0)),
            pl.BlockSpec((N_Q_HEADS, ROW_TILE, LANES), lambda i: (0, i, 0)),
            pl.BlockSpec((ROW_TILE, 512), lambda i: (i, 0)),
            pl.BlockSpec((1, 512, ROW_TILE), lambda i: (i, 0, 0)),
            pl.BlockSpec((MEM_HEADS, ROW_TILE, LANES), lambda i: (0, i, 0)),
            pl.BlockSpec((ROW_TILE, MIX_W), lambda i: (i, 0)),
        ],
        out_shape=[
            jax.ShapeDtypeStruct((N_Q_HEADS, t, LANES), BF16),
            jax.ShapeDtypeStruct((t, LANES), BF16),
            jax.ShapeDtypeStruct((n_tiles, LANES, ROW_TILE), BF16),
            jax.ShapeDtypeStruct((N_Q_HEADS, t, LANES), BF16),
            jax.ShapeDtypeStruct((t, 512), BF16),
            jax.ShapeDtypeStruct((n_tiles, 512, ROW_TILE), BF16),
            jax.ShapeDtypeStruct((MEM_HEADS, t, LANES), BF16),
            jax.ShapeDtypeStruct((t, MIX_W), F32),
        ],
        compiler_params=_params(1),
        name="proj_odd",
    )(x2, g, w, cos1, sin1, cosa, sina, gq, gk, gdq, gdk, gmq, bd)


def _flash_t(qz_ref, k_ref, vt_ref, v_rows, m_sc, l_sc, acc_sc):
    n_maps = qz_ref.shape[0]
    n_chunks, _, chunk = vt_ref.shape
    m_sc[...] = jnp.full(m_sc.shape, NEG, F32)
    l_sc[...] = jnp.zeros(l_sc.shape, F32)
    acc_sc[...] = jnp.zeros(acc_sc.shape, F32)

    def body(c, carry):
        kc = k_ref[pl.ds(pl.multiple_of(c * chunk, chunk), chunk), :]
        vc = vt_ref[c]
        for g in range(n_maps):
            r0, nr = v_rows(g)
            s = _dot_nt(kc, qz_ref[g])
            m_old = m_sc[g]
            m_new = jnp.maximum(m_old, jnp.max(s, axis=0, keepdims=True))
            alpha = jnp.exp2(m_old - m_new)
            p = jnp.exp2(s - m_new)
            l_sc[g] = alpha * l_sc[g] + jnp.sum(p, axis=0, keepdims=True)
            acc_sc[g] = alpha * acc_sc[g] + _dot(vc[r0:r0 + nr, :], p.astype(BF16))
            m_sc[g] = m_new
        return carry

    lax.fori_loop(0, n_chunks, body, 0)


def _stream_t(qz_ref, k_ref, vt_ref, v_rows, emit, *, group, ahead):
    n_maps, tq, _ = qz_ref.shape
    n_tiles, _, tile = vt_ref.shape
    per_tile = tile // KV_CHUNK
    n_chunks = n_tiles * per_tile
    n_groups = n_chunks // group
    assert n_chunks % group == 0

    for g in range(n_maps):
        qt = qz_ref[g].astype(F32).T.astype(BF16)
        r0, nr = v_rows(g)

        def scores(j, qt=qt):
            return [_dot(k_ref[c * KV_CHUNK:(c + 1) * KV_CHUNK, :], qt)
                    for c in range(j * group, (j + 1) * group)]

        pending = [scores(j) for j in range(min(ahead, n_groups))]
        l8 = jnp.zeros((SUBLANES, tq), F32)
        pv = None
        for j in range(n_groups):
            if j + ahead < n_groups:
                pending.append(scores(j + ahead))
            for u, s in enumerate(pending.pop(0)):
                c = j * group + u
                off = (c % per_tile) * KV_CHUNK
                p = jnp.exp2(s)
                l8 = l8 + jnp.sum(p.reshape(KV_CHUNK // SUBLANES, SUBLANES, tq), axis=0)
                d = _dot(vt_ref[c // per_tile, r0:r0 + nr, off:off + KV_CHUNK], p.astype(BF16))
                pv = d if pv is None else pv + d
        emit(g, pv * (1.0 / jnp.sum(l8, axis=0, keepdims=True)))


def _stream_loop_t(qz_ref, k_ref, vt_ref, nr, acc_sc):
    n_maps, tq, _ = qz_ref.shape
    n_tiles, _, tile = vt_ref.shape
    per_tile = tile // KV_CHUNK
    group = math.gcd(LOOP_GROUP, n_tiles * per_tile)
    assert group % per_tile == 0
    outs = []
    for g in range(n_maps):
        qg = qz_ref[g]
        acc_sc[g] = jnp.zeros((nr, tq), F32)

        def body(i, l8, g=g, qg=qg):
            where = [(i * (group // per_tile) + u // per_tile, (u % per_tile) * KV_CHUNK)
                     for u in range(group)]
            scores = []
            for t, off in where:
                kc = k_ref[pl.ds(pl.multiple_of(t * tile + off, KV_CHUNK), KV_CHUNK), :]
                scores.append(_dot_nt(kc, qg))
            pv = None
            for (t, off), s in zip(where, scores):
                p = jnp.exp2(s)
                l8 = l8 + jnp.sum(p.reshape(KV_CHUNK // SUBLANES, SUBLANES, tq), axis=0)
                d = _dot(vt_ref[t, 0:nr, off:off + KV_CHUNK], p.astype(BF16))
                pv = d if pv is None else pv + d
            acc_sc[g] += pv
            return l8

        l8 = lax.fori_loop(0, n_tiles * per_tile // group, body, jnp.zeros((SUBLANES, tq), F32))
        outs.append(acc_sc[g] * (1.0 / jnp.sum(l8, axis=0, keepdims=True)))
    return outs


def _axial_kernel(qz_ref, k_ref, vt_ref, mqz_ref, mk_ref, mvt_ref, g1_ref, gm_ref, y1_ref, ym_ref,
                  *scratch, bounded):
    v_rows = lambda g: ((g // GQA_GROUP) * HEAD, HEAD)
    if bounded:
        mem_pairs = _mem_scores(mqz_ref, mk_ref)
        ym = _mem_finish(mem_pairs, mvt_ref).T
        ym_ref[...] = (ym * gm_ref[...]).astype(BF16)
        held = {}

        def emit(g, o):
            held[g] = o
            if g % 2 == 1:
                cols = slice((g // 2) * LANES, (g // 2 + 1) * LANES)
                y = jnp.concatenate([held.pop(g - 1), held.pop(g)], axis=0).T
                y1_ref[:, cols] = (y * g1_ref[:, cols]).astype(BF16)

        _stream_t(qz_ref, k_ref, vt_ref, v_rows, emit, group=1, ahead=KV_SKEW)
    else:
        ym = _mem_attention_t(mqz_ref, mk_ref, mvt_ref).T
        ym_ref[...] = (ym * gm_ref[...]).astype(BF16)
        m_sc, l_sc, acc_sc = scratch
        _flash_t(qz_ref, k_ref, vt_ref, v_rows, m_sc, l_sc, acc_sc)
        outs = [acc_sc[g] * (1.0 / l_sc[g]) for g in range(N_Q_HEADS)]
        y1 = jnp.concatenate(outs, axis=0).T
        y1_ref[...] = (y1 * g1_ref[...]).astype(BF16)


def _axial_attention(qz, k, vt, mqz, mk, mvt, gate, batch, seq, bounded):
    t = k.shape[0]
    tq = STREAM_Q_TILE if bounded else Q_TILE
    nq = seq // tq
    n_chunks = seq // ROW_TILE
    n_mem = mk.shape[1]
    tok = lambda b, q: b * nq + q
    scratch = [] if bounded else [
        pltpu.VMEM((N_Q_HEADS, 1, tq), F32),
        pltpu.VMEM((N_Q_HEADS, 1, tq), F32),
        pltpu.VMEM((N_Q_HEADS, HEAD, tq), F32),
    ]
    return pl.pallas_call(
        functools.partial(_axial_kernel, bounded=bounded),
        grid=(batch, nq),
        in_specs=[
            pl.BlockSpec((N_Q_HEADS, tq, LANES), lambda b, q: (0, tok(b, q), 0)),
            pl.BlockSpec((seq, LANES), lambda b, q: (b, 0)),
            pl.BlockSpec((n_chunks, LANES, ROW_TILE), lambda b, q: (b, 0, 0)),
            pl.BlockSpec((MEM_HEADS, tq, LANES), lambda b, q: (0, tok(b, q), 0)),
            pl.BlockSpec((1, n_mem, MEM_HEADS * HEAD), lambda b, q: (b, 0, 0)),
            pl.BlockSpec((1, MEM_HEADS * HEAD, n_mem), lambda b, q: (b, 0, 0)),
            pl.BlockSpec((tq, 512), lambda b, q: (tok(b, q), 0)),
            pl.BlockSpec((tq, 256), lambda b, q: (tok(b, q), 4)),
        ],
        out_specs=[
            pl.BlockSpec((tq, 512), lambda b, q: (tok(b, q), 0)),
            pl.BlockSpec((tq, 256), lambda b, q: (tok(b, q), 0)),
        ],
        out_shape=[
            jax.ShapeDtypeStruct((t, 512), BF16),
            jax.ShapeDtypeStruct((t, 256), BF16),
        ],
        scratch_shapes=scratch,
        compiler_params=_params(2),
        name="axial_attention" if bounded else "axial_attention_online",
    )(qz, k, vt, mqz, mk, mvt, gate, gate)


def _diff_kernel(qz_ref, k_ref, vt_ref, lam_ref, sg_ref, g2_ref, y2_ref, *scratch, lambda_init, bounded):
    v_rows = lambda g: (0, 2 * HEAD)
    if bounded:
        o0, o1 = _stream_loop_t(qz_ref, k_ref, vt_ref, 2 * HEAD, *scratch)
    else:
        m_sc, l_sc, acc_sc = scratch
        _flash_t(qz_ref, k_ref, vt_ref, v_rows, m_sc, l_sc, acc_sc)
        o0, o1 = (acc_sc[g] * (1.0 / l_sc[g]) for g in range(2))
    lv = lam_ref[...]
    lam = (jnp.exp(jnp.sum(lv[0:1] * lv[1:2], axis=-1, keepdims=True))
           - jnp.exp(jnp.sum(lv[2:3] * lv[3:4], axis=-1, keepdims=True)) + lambda_init)
    o = o0 - lam * o1
    ms = jnp.mean(o * o, axis=0, keepdims=True)
    on = (o * lax.rsqrt(ms + EPS)).T
    y2_ref[...] = (on * sg_ref[...] * (1.0 - lambda_init) * g2_ref[...]).astype(BF16)


def _diff_attention(dqz, dk, dvt, lam, subln_g, gate, batch, seq, layer, bounded):
    t = dk.shape[0]
    tq = DIFF_Q_TILE if bounded else Q_TILE
    nq = seq // tq
    n_chunks = seq // ROW_TILE
    lambda_init = 0.8 - 0.6 * math.exp(-0.3 * layer)
    tok = lambda b, q: b * nq + q
    scratch = [pltpu.VMEM((2, 2 * HEAD, tq), F32)] if bounded else [
        pltpu.VMEM((2, 1, tq), F32),
        pltpu.VMEM((2, 1, tq), F32),
        pltpu.VMEM((2, 2 * HEAD, tq), F32),
    ]
    return pl.pallas_call(
        functools.partial(_diff_kernel, lambda_init=lambda_init, bounded=bounded),
        grid=(batch, DIFF_HEADS, nq),
        in_specs=[
            pl.BlockSpec((2, tq, LANES), lambda b, h, q: (h, tok(b, q), 0)),
            pl.BlockSpec((seq, LANES), lambda b, h, q: (b, h)),
            pl.BlockSpec((n_chunks, LANES, ROW_TILE), lambda b, h, q: (b, h, 0)),
            pl.BlockSpec((4, HEAD), lambda b, h, q: (0, 0)),
            pl.BlockSpec((1, 2 * HEAD), lambda b, h, q: (0, 0)),
            pl.BlockSpec((tq, LANES), lambda b, h, q: (tok(b, q), 4 + h)),
        ],
        out_specs=pl.BlockSpec((tq, LANES), lambda b, h, q: (tok(b, q), h)),
        out_shape=jax.ShapeDtypeStruct((t, 512), BF16),
        scratch_shapes=scratch,
        compiler_params=_params(3),
        name="diff_attention" if bounded else "diff_attention_online",
    )(dqz, dk, dvt, lam, subln_g.reshape(1, 2 * HEAD), gate)


def _out_proj_kernel(y1_ref, y2_ref, ym_ref, x_ref, w_ref, o_ref):
    acc = _dot(y1_ref[...], w_ref[0:512, :])
    acc = acc + _dot(y2_ref[...], w_ref[512:1024, :])
    acc = acc + _dot(ym_ref[...], w_ref[1024:MIX_W, :])
    o_ref[...] = x_ref[...] + acc


def _out_proj(y1, y2, ym, x2, w):
    t = x2.shape[0]
    return pl.pallas_call(
        _out_proj_kernel,
        grid=(t // OUT_ROW_TILE,),
        in_specs=[
            pl.BlockSpec((OUT_ROW_TILE, 512), lambda i: (i, 0)),
            pl.BlockSpec((OUT_ROW_TILE, 512), lambda i: (i, 0)),
            pl.BlockSpec((OUT_ROW_TILE, 256), lambda i: (i, 0)),
            pl.BlockSpec((OUT_ROW_TILE, D_MODEL), lambda i: (i, 0)),
            pl.BlockSpec((MIX_W, D_MODEL), lambda i: (0, 0)),
        ],
        out_specs=pl.BlockSpec((OUT_ROW_TILE, D_MODEL), lambda i: (i, 0)),
        out_shape=jax.ShapeDtypeStruct((t, D_MODEL), F32),
        compiler_params=_params(1),
        name="out_proj",
    )(y1, y2, ym, x2, w)


def _rope_angles(pos, dim):
    inv = ROPE_THETA ** (-jnp.arange(0, dim, 2, dtype=F32) / dim)
    return pos.astype(F32)[:, None] * inv[None, :]


def _rope_tables(seq):
    pos = jnp.arange(seq)
    a1 = _rope_angles(pos, HEAD)
    cos1 = jnp.concatenate([jnp.cos(a1), jnp.cos(a1)], axis=-1)
    sin1 = jnp.concatenate([-jnp.sin(a1), jnp.sin(a1)], axis=-1)
    ar = _rope_angles(pos // GRID_W, HEAD // 2)
    ac = _rope_angles(pos % GRID_W, HEAD // 2)
    cosa = jnp.concatenate([jnp.cos(ar), jnp.cos(ar), jnp.cos(ac), jnp.cos(ac)], axis=-1)
    sina = jnp.concatenate([-jnp.sin(ar), jnp.sin(ar), -jnp.sin(ac), jnp.sin(ac)], axis=-1)
    rep = lambda a: jnp.tile(a, (1, LANES // HEAD))
    return rep(cos1), rep(sin1), rep(cosa), rep(sina)


def _scores_bounded(qk_g):
    bound = HEAD * Q_SCALE * 1.01 * jnp.max(jnp.abs(qk_g[0])) * jnp.max(jnp.abs(qk_g[1]))
    return bound <= SAFE_SCORE_BOUND


def _tile_gain(g, width):
    return jnp.tile(g.astype(F32), width // HEAD).reshape(1, width)


def _trunk(x, mem, p):
    batch, seq, _ = x.shape
    x2 = x.reshape(batch * seq, D_MODEL)
    cos1, sin1, cosa, sina = _rope_tables(seq)
    bd = p["bd"]
    gk_mem = jnp.stack([_tile_gain(p["mem_qk_g"][l, 1], 256) for l in range(DEPTH)])
    mk_all, mvt_all = _memkv(mem, p["mem_norm_g"], p["w_mem_kv"], gk_mem, bd)
    for l in range(DEPTH):
        g_in, w_in = p["norm_g"][l].reshape(1, D_MODEL), p["w_in"][l]
        gmq = _tile_gain(p["mem_qk_g"][l, 0], 256)
        mk, mvt = mk_all[l], mvt_all[l]
        if l % 2 == 0:
            e = l // 2
            y1, qz, k, vt, mqz, gate = _proj_even(
                x2, seq, g_in, w_in, cos1, sin1, _tile_gain(p["swa_qk_g"][e, 0], 512),
                _tile_gain(p["swa_qk_g"][e, 1], LANES), gmq, p["conv_w"][e], bd)
            sink_ok = jnp.max(jnp.abs(p["swa_sink"][e])) * LOG2E <= SAFE_SCORE_BOUND
            def tail_any(*a):
                y2, ym = _window_attention(*a[:-3], batch, seq)
                return _out_proj(a[-3], y2, ym, a[-2], a[-1])

            x2 = lax.cond(
                _scores_bounded(p["swa_qk_g"][e]) & _scores_bounded(p["mem_qk_g"][l]) & sink_ok,
                functools.partial(_window_out_proj, batch=batch, seq=seq), tail_any,
                p["swa_sink"][e], qz, k, vt, mqz, mk, mvt, gate, y1, x2, p["w_out"][l])
            continue
        else:
            o = l // 2
            qz, k, vt, dqz, dk, dvt, mqz, gate = _proj_odd(
                x2, seq, g_in, w_in, cos1, sin1, cosa, sina, _tile_gain(p["ax_qk_g"][o, 0], 512),
                _tile_gain(p["ax_qk_g"][o, 1], LANES), _tile_gain(p["diff_qk_g"][o, 0], 512),
                _tile_gain(p["diff_qk_g"][o, 1], 512), gmq, bd)
            y1, ym = lax.cond(
                _scores_bounded(p["ax_qk_g"][o]) & _scores_bounded(p["mem_qk_g"][l]),
                functools.partial(_axial_attention, batch=batch, seq=seq, bounded=True),
                functools.partial(_axial_attention, batch=batch, seq=seq, bounded=False),
                qz, k, vt, mqz, mk, mvt, gate)
            y2 = lax.cond(
                _scores_bounded(p["diff_qk_g"][o]),
                functools.partial(_diff_attention, batch=batch, seq=seq, layer=l, bounded=True),
                functools.partial(_diff_attention, batch=batch, seq=seq, layer=l, bounded=False),
                dqz, dk, dvt, p["diff_lambda"][o], p["diff_subln_g"][o], gate)
        x2 = _out_proj(y1, y2, ym, x2, p["w_out"][l])
    return x2.reshape(batch, seq, D_MODEL)


def kernel(x_prompt, x_sample, mem_prompt, mem_sample, norm_g, w_in, w_out, mem_norm_g, w_mem_kv,
           mem_qk_g, conv_w, swa_qk_g, swa_sink, ax_qk_g, diff_qk_g, diff_lambda, diff_subln_g):
    grp = jnp.arange(2 * LANES) // HEAD
    p = dict(
        norm_g=norm_g, w_in=w_in.astype(BF16), w_out=w_out.astype(BF16), mem_norm_g=mem_norm_g,
        w_mem_kv=w_mem_kv.astype(BF16), mem_qk_g=mem_qk_g, conv_w=conv_w, swa_qk_g=swa_qk_g,
        swa_sink=swa_sink, ax_qk_g=ax_qk_g, diff_qk_g=diff_qk_g, diff_lambda=diff_lambda,
        diff_subln_g=diff_subln_g, bd=(grp[:, None] == grp[None, :]).astype(BF16))
    return (_trunk(x_prompt, mem_prompt, p), _trunk(x_sample, mem_sample, p))
```

```python
import functools
import math

import jax
import jax.numpy as jnp
from jax import lax
from jax.experimental import pallas as pl
from jax.experimental.pallas import tpu as pltpu

F32 = jnp.float32
BF16 = jnp.bfloat16

D_MODEL = 1024
DEPTH = 4
HEAD = 64
LANES = 128
SUBLANES = 8
N_Q_HEADS = 8
GQA_GROUP = 4
DIFF_HEADS = 4
MEM_HEADS = 4
CONV_W = 512
WINDOW = 128
GRID_W = 64
ROPE_THETA = 10000.0
EPS = 1e-6
NEG = -1e30
LOG2E = 1.4426950408889634
Q_SCALE = HEAD ** -0.5 * LOG2E
SAFE_SCORE_BOUND = 60.0

MIX_W = 1280


def _offsets(widths):
    out, start = {}, 0
    for name, w in widths:
        out[name] = (start, start + w)
        start += w
    return out


EVEN_COLS = _offsets([("gb", 512), ("gchc", 1024), ("q", 512), ("kv", 256), ("mq", 256), ("z", MIX_W)])
ODD_COLS = _offsets([("q", 512), ("kv", 256), ("dq", 512), ("dk", 512), ("dv", 512), ("mq", 256),
                     ("z", MIX_W)])
IN_W = EVEN_COLS["z"][1]
assert IN_W == ODD_COLS["z"][1] == 3840

ROW_TILE = 512
OUT_ROW_TILE = 1024
Q_TILE = 256
STREAM_Q_TILE = 512
WINDOW_Q_TILE = 1024
DIFF_Q_TILE = 1024
KV_CHUNK = 256
LOOP_GROUP = 32
WINDOW_SKEW = 2
KV_SKEW = 3
VMEM_LIMIT = 56 * 1024 * 1024

_NT = (((1,), (1,)), ((), ()))


def _params(n_axes):
    return pltpu.CompilerParams(dimension_semantics=("arbitrary",) * n_axes, vmem_limit_bytes=VMEM_LIMIT)


def _dot(a, b):
    return jnp.dot(a, b, preferred_element_type=F32)


def _dot_nt(a, b):
    return lax.dot_general(a, b, _NT, preferred_element_type=F32)


def _rms_rows(x, g):
    ms = jnp.mean(x * x, axis=-1, keepdims=True)
    return x * lax.rsqrt(ms + EPS) * g


def _head_sumsq(x, bd):
    w = x.shape[1]
    outs = []
    step = 2 * LANES if w % (2 * LANES) == 0 else LANES
    for c in range(0, w, step):
        sq = x[:, c:c + step]
        outs.append(_dot((sq * sq).astype(BF16), bd[:step, :step]))
    return outs[0] if len(outs) == 1 else jnp.concatenate(outs, axis=1)


def _head_rms(x, g, bd):
    return x * lax.rsqrt(_head_sumsq(x, bd) * (1.0 / HEAD) + EPS) * g


def _rope(x, cos, sin_signed, half):
    r = x.shape[0]
    lane = lax.broadcasted_iota(jnp.int32, (r, LANES), 1)
    first = (lane & (2 * half - 1)) < half
    outs = []
    for c in range(0, x.shape[1], LANES):
        xc = x[:, c:c + LANES]
        partner = jnp.where(first, pltpu.roll(xc, LANES - half, 1), pltpu.roll(xc, half, 1))
        outs.append(xc * cos + partner * sin_signed)
    return outs[0] if len(outs) == 1 else jnp.concatenate(outs, axis=1)


def _place_heads(x, target_half, out_ref):
    r = x.shape[0]
    low = lax.broadcasted_iota(jnp.int32, (r, LANES), 1) < HEAD
    for h in range(x.shape[1] // HEAD):
        xc = x[:, (h // 2) * LANES:(h // 2 + 1) * LANES]
        th = target_half(h)
        if th != h % 2:
            xc = pltpu.roll(xc, HEAD, 1)
        keep = low if th == 0 else jnp.logical_not(low)
        out_ref[h] = jnp.where(keep, xc, 0.0).astype(BF16)


def _silu(z):
    return z * (1.0 / (1.0 + jnp.exp(-z)))


def _branch_proj(h, w_ref, cols, name):
    a, b = cols[name]
    return _dot(h, w_ref[:, a:b])


def _memkv_kernel(mem_ref, g_ref, w_ref, gk_ref, bd_ref, mk_ref, mvt_ref):
    h = _rms_rows(mem_ref[0], g_ref[0]).astype(BF16)
    mkv = _dot(h, w_ref[0])
    half = mkv.shape[1] // 2
    mk = _head_rms(mkv[:, :half], gk_ref[0], bd_ref[...])
    mk_ref[0, 0] = mk.astype(BF16)
    mvt_ref[0, 0] = mkv[:, half:].T.astype(BF16)


def _memkv(mem, mem_norm_g, w_mem_kv, gk_tiled, bd):
    b, n_mem, _ = mem.shape
    width = MEM_HEADS * HEAD
    return pl.pallas_call(
        _memkv_kernel,
        grid=(DEPTH, b),
        in_specs=[
            pl.BlockSpec((1, n_mem, D_MODEL), lambda l, i: (i, 0, 0)),
            pl.BlockSpec((1, 1, D_MODEL), lambda l, i: (l, 0, 0)),
            pl.BlockSpec((1, D_MODEL, 2 * width), lambda l, i: (l, 0, 0)),
            pl.BlockSpec((1, 1, width), lambda l, i: (l, 0, 0)),
            pl.BlockSpec((2 * LANES, 2 * LANES), lambda l, i: (0, 0)),
        ],
        out_specs=[
            pl.BlockSpec((1, 1, n_mem, width), lambda l, i: (l, i, 0, 0)),
            pl.BlockSpec((1, 1, width, n_mem), lambda l, i: (l, i, 0, 0)),
        ],
        out_shape=[
            jax.ShapeDtypeStruct((DEPTH, b, n_mem, width), BF16),
            jax.ShapeDtypeStruct((DEPTH, b, width, n_mem), BF16),
        ],
        compiler_params=_params(2),
        name="mem_kv",
    )(mem, mem_norm_g.reshape(DEPTH, 1, D_MODEL), w_mem_kv, gk_tiled, bd)


def _proj_even_kernel(x_ref, xp_ref, xn_ref, g_ref, w_ref, cos_ref, sin_ref, gq_ref, gk_ref,
                      gmq_ref, cw_ref, bd_ref,
                      y1_ref, qz_ref, k_ref, vt_ref, mqz_ref, gate_ref, conv_sc, *, tiles_per_seq):
    i = pl.program_id(0)
    pos_tile = i % tiles_per_seq
    bd = bd_ref[...]
    cos = cos_ref[...]
    sin = sin_ref[...]
    rows = x_ref.shape[0]

    x_ext = jnp.concatenate([x_ref[...], xp_ref[...], xn_ref[...]], axis=0)
    h_ext = _rms_rows(x_ext, g_ref[...]).astype(BF16)
    h = h_ext[:rows]
    proj = functools.partial(_branch_proj, h, w_ref, EVEN_COLS)

    z = proj("z")
    gchc = _branch_proj(h_ext, w_ref, EVEN_COLS, "gchc")
    gate_ref[...] = _silu(z[:, CONV_W:MIX_W])
    gb = proj("gb")
    q = proj("q")

    inner_ext = gchc[:, 0:CONV_W] * gchc[:, CONV_W:2 * CONV_W]
    inner = inner_ext[:rows]
    prev_row = inner_ext[rows + SUBLANES - 1:rows + SUBLANES]
    next_row = inner_ext[rows + SUBLANES:rows + SUBLANES + 1]
    prev_row = jnp.where(pos_tile == 0, 0.0, prev_row)
    next_row = jnp.where(pos_tile == tiles_per_seq - 1, 0.0, next_row)
    conv_sc[pl.ds(SUBLANES, rows), :] = inner
    conv_sc[pl.ds(SUBLANES - 1, 1), :] = prev_row
    conv_sc[pl.ds(SUBLANES + rows, 1), :] = next_row
    cw = cw_ref[...]
    conv = (conv_sc[pl.ds(SUBLANES - 1, rows), :] * cw[0:1, :] + inner * cw[1:2, :]
            + conv_sc[pl.ds(SUBLANES + 1, rows), :] * cw[2:3, :])
    y1_ref[...] = (gb * conv * _silu(z[:, 0:CONV_W])).astype(BF16)

    mq = proj("mq")
    q = _rope(_head_rms(q, gq_ref[...], bd), cos, sin, HEAD // 2) * Q_SCALE
    _place_heads(q, lambda h: h // GQA_GROUP, qz_ref)

    kv = proj("kv")
    mq = _head_rms(mq, gmq_ref[...], bd) * Q_SCALE
    _place_heads(mq, lambda h: h % 2, mqz_ref)

    k = _rope(_head_rms(kv[:, 0:LANES], gk_ref[...], bd), cos, sin, HEAD // 2)
    k_ref[...] = k.astype(BF16)
    vt_ref[0] = kv[:, LANES:2 * LANES].T.astype(BF16)


def _proj_even(x2, seq, g, w, cos, sin, gq, gk, gmq, conv_w, bd):
    t = x2.shape[0]
    n_tiles = t // ROW_TILE
    tiles_per_seq = seq // ROW_TILE
    sub_per_tile = ROW_TILE // SUBLANES
    last_sub = t // SUBLANES - 1
    row = lambda w: pl.BlockSpec((1, w), lambda i: (0, 0))
    tab = pl.BlockSpec((ROW_TILE, LANES), lambda i: (i % tiles_per_seq, 0))
    return pl.pallas_call(
        functools.partial(_proj_even_kernel, tiles_per_seq=tiles_per_seq),
        grid=(n_tiles,),
        in_specs=[
            pl.BlockSpec((ROW_TILE, D_MODEL), lambda i: (i, 0)),
            pl.BlockSpec((SUBLANES, D_MODEL), lambda i: (jnp.maximum(i * sub_per_tile - 1, 0), 0)),
            pl.BlockSpec((SUBLANES, D_MODEL), lambda i: (jnp.minimum((i + 1) * sub_per_tile, last_sub), 0)),
            row(D_MODEL),
            pl.BlockSpec((D_MODEL, IN_W), lambda i: (0, 0)),
            tab, tab, row(512), row(LANES), row(256),
            pl.BlockSpec((3, CONV_W), lambda i: (0, 0)),
            pl.BlockSpec((2 * LANES, 2 * LANES), lambda i: (0, 0)),
        ],
        out_specs=[
            pl.BlockSpec((ROW_TILE, CONV_W), lambda i: (i, 0)),
            pl.BlockSpec((N_Q_HEADS, ROW_TILE, LANES), lambda i: (0, i, 0)),
            pl.BlockSpec((ROW_TILE, LANES), lambda i: (i, 0)),
            pl.BlockSpec((1, LANES, ROW_TILE), lambda i: (i, 0, 0)),
            pl.BlockSpec((MEM_HEADS, ROW_TILE, LANES), lambda i: (0, i, 0)),
            pl.BlockSpec((ROW_TILE, 768), lambda i: (i, 0)),
        ],
        out_shape=[
            jax.ShapeDtypeStruct((t, CONV_W), BF16),
            jax.ShapeDtypeStruct((N_Q_HEADS, t, LANES), BF16),
            jax.ShapeDtypeStruct((t, LANES), BF16),
            jax.ShapeDtypeStruct((n_tiles, LANES, ROW_TILE), BF16),
            jax.ShapeDtypeStruct((MEM_HEADS, t, LANES), BF16),
            jax.ShapeDtypeStruct((t, 768), F32),
        ],
        scratch_shapes=[pltpu.VMEM((ROW_TILE + 2 * SUBLANES, CONV_W), F32)],
        compiler_params=_params(1),
        name="proj_even",
    )(x2, x2, x2, g, w, cos, sin, gq, gk, gmq, conv_w, bd)


def _mem_scores(mqz_ref, mk_ref):
    pairs = []
    for grp in range(MEM_HEADS // 2):
        q2 = jnp.concatenate([mqz_ref[2 * grp], mqz_ref[2 * grp + 1]], axis=0)
        pairs.append(_dot_nt(mk_ref[0, :, grp * LANES:(grp + 1) * LANES], q2))
    return pairs


def _mem_finish(pairs, mvt_ref):
    tq = pairs[0].shape[1] // 2
    outs = []
    for h in range(MEM_HEADS):
        p = jnp.exp2(pairs[h // 2][:, (h % 2) * tq:(h % 2 + 1) * tq])
        l = jnp.sum(p, axis=0, keepdims=True)
        o = _dot(mvt_ref[0, h * HEAD:(h + 1) * HEAD, :], p.astype(BF16))
        outs.append(o * (1.0 / l))
    return jnp.concatenate(outs, axis=0)


def _mem_attention_t(mqz_ref, mk_ref, mvt_ref):
    outs = []
    for h in range(MEM_HEADS):
        grp = h // 2
        s = _dot_nt(mk_ref[0, :, grp * LANES:(grp + 1) * LANES], mqz_ref[h])
        m = jnp.max(s, axis=0, keepdims=True)
        p = jnp.exp2(s - m)
        l = jnp.sum(p, axis=0, keepdims=True)
        o = _dot(mvt_ref[0, h * HEAD:(h + 1) * HEAD, :], p.astype(BF16))
        outs.append(o * (1.0 / l))
    return jnp.concatenate(outs, axis=0)


def _window_kernel(sink_ref, qz_ref, kp_ref, kc_ref, kn_ref, vp_ref, vc_ref, vn_ref, mqz_ref, mk_ref,
                   mvt_ref, g2_ref, gm_ref, y2_ref, ym_ref, *, seq):
    tq = qz_ref.shape[1]
    t0 = pl.program_id(1) * tq
    kband = jnp.concatenate([kp_ref[...], kc_ref[...], kn_ref[...]], axis=0)
    vband = jnp.concatenate([vp_ref[0]] + [vc_ref[c] for c in range(vc_ref.shape[0])] + [vn_ref[0]],
                            axis=1)
    nk = tq + 2 * WINDOW
    j = lax.broadcasted_iota(jnp.int32, (nk, tq), 0)
    i = lax.broadcasted_iota(jnp.int32, (nk, tq), 1)
    kpos = j + (t0 - WINDOW)
    mask = (j >= i) & (j <= i + 2 * WINDOW) & (kpos >= 0) & (kpos < seq)
    outs = []
    for g in range(N_Q_HEADS):
        kv = g // GQA_GROUP
        s = jnp.where(mask, _dot_nt(kband, qz_ref[g]), NEG)
        sink = sink_ref[g] * LOG2E
        m = jnp.maximum(jnp.max(s, axis=0, keepdims=True), sink)
        p = jnp.exp2(s - m)
        l = jnp.sum(p, axis=0, keepdims=True) + jnp.exp2(sink - m)
        o = _dot(vband[kv * HEAD:(kv + 1) * HEAD, :], p.astype(BF16))
        outs.append(o * (1.0 / l))
    y2 = jnp.concatenate(outs, axis=0).T
    y2_ref[...] = (y2 * g2_ref[...]).astype(BF16)
    ym = _mem_attention_t(mqz_ref, mk_ref, mvt_ref).T
    ym_ref[...] = (ym * gm_ref[...]).astype(BF16)


def _window_out_kernel(sink_ref, qz_ref, kp_ref, kc_ref, kn_ref, vp_ref, vc_ref, vn_ref, mqz_ref,
                       mk_ref, mvt_ref, g2_ref, gm_ref, y1_ref, x_ref, w_ref, o_ref, y2_sc, ym_sc,
                       *, seq, n_tiles, tiles_per_seq):
    step = pl.program_id(0)
    tq = qz_ref.shape[1]
    t0 = (jnp.minimum(step, n_tiles - 1) % tiles_per_seq) * tq

    @pl.when(step == 0)
    def _():
        y2_sc[...] = jnp.zeros(y2_sc.shape, BF16)
        ym_sc[...] = jnp.zeros(ym_sc.shape, BF16)

    y2_prev = y2_sc[...]
    ym_prev = ym_sc[...]
    kband = jnp.concatenate([kp_ref[...], kc_ref[...], kn_ref[...]], axis=0)
    vband = jnp.concatenate([vp_ref[0]] + [vc_ref[c] for c in range(vc_ref.shape[0])] + [vn_ref[0]],
                            axis=1)
    nk = 3 * WINDOW
    j = lax.broadcasted_iota(jnp.int32, (nk, WINDOW), 0)
    i = lax.broadcasted_iota(jnp.int32, (nk, WINDOW), 1)
    in_band = (j >= i) & (j <= i + 2 * WINDOW)
    sink_p = [jnp.exp2(jnp.full((1, WINDOW), sink_ref[g] * LOG2E, F32)) for g in range(N_Q_HEADS)]
    n_kv = N_Q_HEADS // GQA_GROUP
    units = [(a, kv) for a in range(tq // WINDOW) for kv in range(n_kv)]

    def stacked_qt(a, kv):
        q4 = jnp.concatenate([qz_ref[kv * GQA_GROUP + r, a * WINDOW:(a + 1) * WINDOW, :]
                              for r in range(GQA_GROUP)], axis=0)
        return q4.astype(F32).T.astype(BF16)

    qts = {u: stacked_qt(*u) for u in units}

    def score(a, kv):
        return _dot(kband[a * WINDOW:a * WINDOW + nk], qts[(a, kv)])

    pending = [score(*u) for u in units[:WINDOW_SKEW]]
    mem_pairs = _mem_scores(mqz_ref, mk_ref)
    acc = _dot(y1_ref[...], w_ref[0:512, :])
    acc = acc + _dot(y2_prev, w_ref[512:1024, :])
    acc = acc + _dot(ym_prev, w_ref[1024:MIX_W, :])
    o_ref[...] = x_ref[...] + acc
    blocks = [[None] * (tq // WINDOW) for _ in range(N_Q_HEADS)]
    for n, (a, kv) in enumerate(units):
        if n + WINDOW_SKEW < len(units):
            pending.append(score(*units[n + WINDOW_SKEW]))
        kpos = j + (t0 + (a - 1) * WINDOW)
        bias1 = jnp.where(in_band & (kpos >= 0) & (kpos < seq), 0.0, NEG)
        bias = jnp.concatenate([bias1] * GQA_GROUP, axis=1)
        p = jnp.exp2(pending.pop(0) + bias)
        sinks = jnp.concatenate(sink_p[kv * GQA_GROUP:(kv + 1) * GQA_GROUP], axis=1)
        l = jnp.sum(p, axis=0, keepdims=True) + sinks
        o = _dot(vband[kv * HEAD:(kv + 1) * HEAD, a * WINDOW:a * WINDOW + nk], p.astype(BF16))
        o = o * (1.0 / l)
        for r in range(GQA_GROUP):
            blocks[kv * GQA_GROUP + r][a] = o[:, r * WINDOW:(r + 1) * WINDOW]
    y2 = jnp.concatenate([jnp.concatenate(b, axis=1) for b in blocks], axis=0).T
    y2_sc[...] = (y2 * g2_ref[...]).astype(BF16)
    ym = _mem_finish(mem_pairs, mvt_ref).T
    ym_sc[...] = (ym * gm_ref[...]).astype(BF16)


def _window_out_proj(sink, qz, k, vt, mqz, mk, mvt, gate, y1, x2, w_out, batch, seq):
    t = k.shape[0]
    tq = WINDOW_Q_TILE
    nq = seq // tq
    n_tiles = batch * nq
    w_per_q = tq // WINDOW
    w_per_seq = seq // WINDOW
    w_per_chunk = ROW_TILE // WINDOW
    n_mem = mk.shape[1]
    att = lambda i: jnp.minimum(i, n_tiles - 1)
    out = lambda i: jnp.maximum(i - 1, 0)

    def prev_w(i):
        b, q = att(i) // nq, att(i) % nq
        return b * w_per_seq + jnp.maximum(q * w_per_q - 1, 0)

    def next_w(i):
        b, q = att(i) // nq, att(i) % nq
        return b * w_per_seq + jnp.minimum((q + 1) * w_per_q, w_per_seq - 1)

    return pl.pallas_call(
        functools.partial(_window_out_kernel, seq=seq, n_tiles=n_tiles, tiles_per_seq=nq),
        grid=(n_tiles + 1,),
        in_specs=[
            pl.BlockSpec(memory_space=pltpu.SMEM),
            pl.BlockSpec((N_Q_HEADS, tq, LANES), lambda i: (0, att(i), 0)),
            pl.BlockSpec((WINDOW, LANES), lambda i: (prev_w(i), 0)),
            pl.BlockSpec((tq, LANES), lambda i: (att(i), 0)),
            pl.BlockSpec((WINDOW, LANES), lambda i: (next_w(i), 0)),
            pl.BlockSpec((1, LANES, WINDOW), lambda i: (prev_w(i) // w_per_chunk, 0, prev_w(i) % w_per_chunk)),
            pl.BlockSpec((tq // ROW_TILE, LANES, ROW_TILE), lambda i: (att(i), 0, 0)),
            pl.BlockSpec((1, LANES, WINDOW), lambda i: (next_w(i) // w_per_chunk, 0, next_w(i) % w_per_chunk)),
            pl.BlockSpec((MEM_HEADS, tq, LANES), lambda i: (0, att(i), 0)),
            pl.BlockSpec((1, n_mem, MEM_HEADS * HEAD), lambda i: (att(i) // nq, 0, 0)),
            pl.BlockSpec((1, MEM_HEADS * HEAD, n_mem), lambda i: (att(i) // nq, 0, 0)),
            pl.BlockSpec((tq, 512), lambda i: (att(i), 0)),
            pl.BlockSpec((tq, 256), lambda i: (att(i), 2)),
            pl.BlockSpec((tq, 512), lambda i: (out(i), 0)),
            pl.BlockSpec((tq, D_MODEL), lambda i: (out(i), 0)),
            pl.BlockSpec((MIX_W, D_MODEL), lambda i: (0, 0)),
        ],
        out_specs=pl.BlockSpec((tq, D_MODEL), lambda i: (out(i), 0)),
        out_shape=jax.ShapeDtypeStruct((t, D_MODEL), F32),
        scratch_shapes=[pltpu.VMEM((tq, 512), BF16), pltpu.VMEM((tq, 256), BF16)],
        compiler_params=_params(1),
        name="window_out_proj",
    )(sink, qz, k, k, k, vt, vt, vt, mqz, mk, mvt, gate, gate, y1, x2, w_out)


def _window_attention(sink, qz, k, vt, mqz, mk, mvt, gate, batch, seq):
    t = k.shape[0]
    tq = Q_TILE
    nq = seq // tq
    w_per_q = tq // WINDOW
    w_per_seq = seq // WINDOW
    w_per_chunk = ROW_TILE // WINDOW
    n_mem = mk.shape[1]
    q_per_chunk = ROW_TILE // tq
    cur_v = pl.BlockSpec((1, LANES, tq),
                         lambda b, q: ((b * nq + q) // q_per_chunk, 0, (b * nq + q) % q_per_chunk))

    def prev_w(b, q):
        return b * w_per_seq + jnp.maximum(q * w_per_q - 1, 0)

    def next_w(b, q):
        return b * w_per_seq + jnp.minimum((q + 1) * w_per_q, w_per_seq - 1)

    tok = lambda b, q: b * nq + q
    return pl.pallas_call(
        functools.partial(_window_kernel, seq=seq),
        grid=(batch, nq),
        in_specs=[
            pl.BlockSpec(memory_space=pltpu.SMEM),
            pl.BlockSpec((N_Q_HEADS, tq, LANES), lambda b, q: (0, tok(b, q), 0)),
            pl.BlockSpec((WINDOW, LANES), lambda b, q: (prev_w(b, q), 0)),
            pl.BlockSpec((tq, LANES), lambda b, q: (tok(b, q), 0)),
            pl.BlockSpec((WINDOW, LANES), lambda b, q: (next_w(b, q), 0)),
            pl.BlockSpec((1, LANES, WINDOW), lambda b, q: (prev_w(b, q) // w_per_chunk, 0, prev_w(b, q) % w_per_chunk)),
            cur_v,
            pl.BlockSpec((1, LANES, WINDOW), lambda b, q: (next_w(b, q) // w_per_chunk, 0, next_w(b, q) % w_per_chunk)),
            pl.BlockSpec((MEM_HEADS, tq, LANES), lambda b, q: (0, tok(b, q), 0)),
            pl.BlockSpec((1, n_mem, MEM_HEADS * HEAD), lambda b, q: (b, 0, 0)),
            pl.BlockSpec((1, MEM_HEADS * HEAD, n_mem), lambda b, q: (b, 0, 0)),
            pl.BlockSpec((tq, 512), lambda b, q: (tok(b, q), 0)),
            pl.BlockSpec((tq, 256), lambda b, q: (tok(b, q), 2)),
        ],
        out_specs=[
            pl.BlockSpec((tq, 512), lambda b, q: (tok(b, q), 0)),
            pl.BlockSpec((tq, 256), lambda b, q: (tok(b, q), 0)),
        ],
        out_shape=[
            jax.ShapeDtypeStruct((t, 512), BF16),
            jax.ShapeDtypeStruct((t, 256), BF16),
        ],
        compiler_params=_params(2),
        name="window_attention_max",
    )(sink, qz, k, k, k, vt, vt, vt, mqz, mk, mvt, gate, gate)


def _proj_odd_kernel(x_ref, g_ref, w_ref, cos1_ref, sin1_ref, cosa_ref, sina_ref, gq_ref, gk_ref,
                     gdq_ref, gdk_ref, gmq_ref, bd_ref,
                     qz_ref, k_ref, vt_ref, dqz_ref, dk_ref, dvt_ref, mqz_ref, gate_ref):
    bd = bd_ref[...]
    cos1, sin1 = cos1_ref[...], sin1_ref[...]
    cosa, sina = cosa_ref[...], sina_ref[...]
    h = _rms_rows(x_ref[...], g_ref[...]).astype(BF16)
    proj = functools.partial(_branch_proj, h, w_ref, ODD_COLS)

    z = proj("z")
    q = proj("q")
    gate_ref[...] = _silu(z)

    dq = proj("dq")
    q = _rope(_head_rms(q, gq_ref[...], bd), cosa, sina, HEAD // 4) * Q_SCALE
    _place_heads(q, lambda h: h // GQA_GROUP, qz_ref)

    dk = proj("dk")
    dq = _rope(_head_rms(dq, gdq_ref[...], bd), cos1, sin1, HEAD // 2) * Q_SCALE
    _place_heads(dq, lambda h: h % 2, dqz_ref)

    dv = proj("dv")
    dk = _rope(_head_rms(dk, gdk_ref[...], bd), cos1, sin1, HEAD // 2)
    dk_ref[...] = dk.astype(BF16)

    mq = proj("mq")
    dvt_ref[0] = dv.T.astype(BF16)

    kv = proj("kv")
    mq = _head_rms(mq, gmq_ref[...], bd) * Q_SCALE
    _place_heads(mq, lambda h: h % 2, mqz_ref)

    k = _rope(_head_rms(kv[:, 0:LANES], gk_ref[...], bd), cosa, sina, HEAD // 4)
    k_ref[...] = k.astype(BF16)
    vt_ref[0] = kv[:, LANES:2 * LANES].T.astype(BF16)


def _proj_odd(x2, seq, g, w, cos1, sin1, cosa, sina, gq, gk, gdq, gdk, gmq, bd):
    t = x2.shape[0]
    n_tiles = t // ROW_TILE
    tiles_per_seq = seq // ROW_TILE
    row = lambda w: pl.BlockSpec((1, w), lambda i: (0, 0))
    tab = pl.BlockSpec((ROW_TILE, LANES), lambda i: (i % tiles_per_seq, 0))
    return pl.pallas_call(
        _proj_odd_kernel,
        grid=(n_tiles,),
        in_specs=[
            pl.BlockSpec((ROW_TILE, D_MODEL), lambda i: (i, 0)),
            row(D_MODEL),
            pl.BlockSpec((D_MODEL, IN_W), lambda i: (0, 0)),
            tab, tab, tab, tab, row(512), row(LANES), row(512), row(512), row(256),
            pl.BlockSpec((2 * LANES, 2 * LANES), lambda i: (0, 0)),
        ],
        out_specs=[
            pl.BlockSpec((N_Q_HEADS, ROW_TILE, LANES), lambda i: (0, i, 0)),
            pl.BlockSpec((ROW_TILE, LANES), lambda i: (i, 0)),
            pl.BlockSpec((1, LANES, ROW_TILE), lambda i: (i, 0, 0)),
            pl.BlockSpec((N_Q_HEADS, ROW_TILE, LANES), lambda i: (0, i, 0)),
            pl.BlockSpec((ROW_TILE, 512), lambda i: (i, 0)),
            pl.BlockSpec((1, 512, ROW_TILE), lambda i: (i, 0, 0)),
            pl.BlockSpec((MEM_HEADS, ROW_TILE, LANES), lambda i: (0, i, 0)),
            pl.BlockSpec((ROW_TILE, MIX_W), lambda i: (i, 0)),
        ],
        out_shape=[
            jax.ShapeDtypeStruct((N_Q_HEADS, t, LANES), BF16),
            jax.ShapeDtypeStruct((t, LANES), BF16),
            jax.ShapeDtypeStruct((n_tiles, LANES, ROW_TILE), BF16),
            jax.ShapeDtypeStruct((N_Q_HEADS, t, LANES), BF16),
            jax.ShapeDtypeStruct((t, 512), BF16),
            jax.ShapeDtypeStruct((n_tiles, 512, ROW_TILE), BF16),
            jax.ShapeDtypeStruct((MEM_HEADS, t, LANES), BF16),
            jax.ShapeDtypeStruct((t, MIX_W), F32),
        ],
        compiler_params=_params(1),
        name="proj_odd",
    )(x2, g, w, cos1, sin1, cosa, sina, gq, gk, gdq, gdk, gmq, bd)


def _flash_t(qz_ref, k_ref, vt_ref, v_rows, m_sc, l_sc, acc_sc):
    n_maps = qz_ref.shape[0]
    n_chunks, _, chunk = vt_ref.shape
    m_sc[...] = jnp.full(m_sc.shape, NEG, F32)
    l_sc[...] = jnp.zeros(l_sc.shape, F32)
    acc_sc[...] = jnp.zeros(acc_sc.shape, F32)

    def body(c, carry):
        kc = k_ref[pl.ds(pl.multiple_of(c * chunk, chunk), chunk), :]
        vc = vt_ref[c]
        for g in range(n_maps):
            r0, nr = v_rows(g)
            s = _dot_nt(kc, qz_ref[g])
            m_old = m_sc[g]
            m_new = jnp.maximum(m_old, jnp.max(s, axis=0, keepdims=True))
            alpha = jnp.exp2(m_old - m_new)
            p = jnp.exp2(s - m_new)
            l_sc[g] = alpha * l_sc[g] + jnp.sum(p, axis=0, keepdims=True)
            acc_sc[g] = alpha * acc_sc[g] + _dot(vc[r0:r0 + nr, :], p.astype(BF16))
            m_sc[g] = m_new
        return carry

    lax.fori_loop(0, n_chunks, body, 0)


def _stream_t(qz_ref, k_ref, vt_ref, v_rows, emit):
    n_maps, tq, _ = qz_ref.shape
    n_tiles, _, tile = vt_ref.shape
    per_tile = tile // KV_CHUNK
    n_chunks = n_tiles * per_tile

    for g in range(n_maps):
        qt = qz_ref[g].astype(F32).T.astype(BF16)
        r0, nr = v_rows(g)

        def score(c, qt=qt):
            return _dot(k_ref[c * KV_CHUNK:(c + 1) * KV_CHUNK, :], qt)

        pending = [score(c) for c in range(min(KV_SKEW, n_chunks))]
        l8 = jnp.zeros((SUBLANES, tq), F32)
        pv = None
        for c in range(n_chunks):
            if c + KV_SKEW < n_chunks:
                pending.append(score(c + KV_SKEW))
            off = (c % per_tile) * KV_CHUNK
            p = jnp.exp2(pending.pop(0))
            l8 = l8 + jnp.sum(p.reshape(KV_CHUNK // SUBLANES, SUBLANES, tq), axis=0)
            d = _dot(vt_ref[c // per_tile, r0:r0 + nr, off:off + KV_CHUNK], p.astype(BF16))
            pv = d if pv is None else pv + d
        emit(g, pv * (1.0 / jnp.sum(l8, axis=0, keepdims=True)))


def _stream_loop_t(qz_ref, k_ref, vt_ref, nr, acc_sc):
    n_maps, tq, _ = qz_ref.shape
    n_tiles, _, tile = vt_ref.shape
    per_tile = tile // KV_CHUNK
    group = math.gcd(LOOP_GROUP, n_tiles * per_tile)
    assert group % per_tile == 0
    outs = []
    for g in range(n_maps):
        qg = qz_ref[g]
        acc_sc[g] = jnp.zeros((nr, tq), F32)

        def body(i, l8, g=g, qg=qg):
            where = [(i * (group // per_tile) + u // per_tile, (u % per_tile) * KV_CHUNK)
                     for u in range(group)]
            scores = []
            for t, off in where:
                kc = k_ref[pl.ds(pl.multiple_of(t * tile + off, KV_CHUNK), KV_CHUNK), :]
                scores.append(_dot_nt(kc, qg))
            pv = None
            for (t, off), s in zip(where, scores):
                p = jnp.exp2(s)
                l8 = l8 + jnp.sum(p.reshape(KV_CHUNK // SUBLANES, SUBLANES, tq), axis=0)
                d = _dot(vt_ref[t, 0:nr, off:off + KV_CHUNK], p.astype(BF16))
                pv = d if pv is None else pv + d
            acc_sc[g] += pv
            return l8

        l8 = lax.fori_loop(0, n_tiles * per_tile // group, body, jnp.zeros((SUBLANES, tq), F32))
        outs.append(acc_sc[g] * (1.0 / jnp.sum(l8, axis=0, keepdims=True)))
    return outs


def _axial_kernel(qz_ref, k_ref, vt_ref, mqz_ref, mk_ref, mvt_ref, g1_ref, gm_ref, y1_ref, ym_ref,
                  *scratch, bounded):
    v_rows = lambda g: ((g // GQA_GROUP) * HEAD, HEAD)
    if bounded:
        mem_pairs = _mem_scores(mqz_ref, mk_ref)
        ym = _mem_finish(mem_pairs, mvt_ref).T
        ym_ref[...] = (ym * gm_ref[...]).astype(BF16)
        held = {}

        def emit(g, o):
            held[g] = o
            if g % 2 == 1:
                cols = slice((g // 2) * LANES, (g // 2 + 1) * LANES)
                y = jnp.concatenate([held.pop(g - 1), held.pop(g)], axis=0).T
                y1_ref[:, cols] = (y * g1_ref[:, cols]).astype(BF16)

        _stream_t(qz_ref, k_ref, vt_ref, v_rows, emit)
    else:
        ym = _mem_attention_t(mqz_ref, mk_ref, mvt_ref).T
        ym_ref[...] = (ym * gm_ref[...]).astype(BF16)
        m_sc, l_sc, acc_sc = scratch
        _flash_t(qz_ref, k_ref, vt_ref, v_rows, m_sc, l_sc, acc_sc)
        outs = [acc_sc[g] * (1.0 / l_sc[g]) for g in range(N_Q_HEADS)]
        y1 = jnp.concatenate(outs, axis=0).T
        y1_ref[...] = (y1 * g1_ref[...]).astype(BF16)


def _axial_attention(qz, k, vt, mqz, mk, mvt, gate, batch, seq, bounded):
    t = k.shape[0]
    tq = STREAM_Q_TILE if bounded else Q_TILE
    nq = seq // tq
    n_chunks = seq // ROW_TILE
    n_mem = mk.shape[1]
    tok = lambda b, q: b * nq + q
    scratch = [] if bounded else [
        pltpu.VMEM((N_Q_HEADS, 1, tq), F32),
        pltpu.VMEM((N_Q_HEADS, 1, tq), F32),
        pltpu.VMEM((N_Q_HEADS, HEAD, tq), F32),
    ]
    return pl.pallas_call(
        functools.partial(_axial_kernel, bounded=bounded),
        grid=(batch, nq),
        in_specs=[
            pl.BlockSpec((N_Q_HEADS, tq, LANES), lambda b, q: (0, tok(b, q), 0)),
            pl.BlockSpec((seq, LANES), lambda b, q: (b, 0)),
            pl.BlockSpec((n_chunks, LANES, ROW_TILE), lambda b, q: (b, 0, 0)),
            pl.BlockSpec((MEM_HEADS, tq, LANES), lambda b, q: (0, tok(b, q), 0)),
            pl.BlockSpec((1, n_mem, MEM_HEADS * HEAD), lambda b, q: (b, 0, 0)),
            pl.BlockSpec((1, MEM_HEADS * HEAD, n_mem), lambda b, q: (b, 0, 0)),
            pl.BlockSpec((tq, 512), lambda b, q: (tok(b, q), 0)),
            pl.BlockSpec((tq, 256), lambda b, q: (tok(b, q), 4)),
        ],
        out_specs=[
            pl.BlockSpec((tq, 512), lambda b, q: (tok(b, q), 0)),
            pl.BlockSpec((tq, 256), lambda b, q: (tok(b, q), 0)),
        ],
        out_shape=[
            jax.ShapeDtypeStruct((t, 512), BF16),
            jax.ShapeDtypeStruct((t, 256), BF16),
        ],
        scratch_shapes=scratch,
        compiler_params=_params(2),
        name="axial_attention" if bounded else "axial_attention_online",
    )(qz, k, vt, mqz, mk, mvt, gate, gate)


def _diff_kernel(qz_ref, k_ref, vt_ref, lam_ref, sg_ref, g2_ref, y2_ref, *scratch, lambda_init, bounded):
    v_rows = lambda g: (0, 2 * HEAD)
    if bounded:
        o0, o1 = _stream_loop_t(qz_ref, k_ref, vt_ref, 2 * HEAD, *scratch)
    else:
        m_sc, l_sc, acc_sc = scratch
        _flash_t(qz_ref, k_ref, vt_ref, v_rows, m_sc, l_sc, acc_sc)
        o0, o1 = (acc_sc[g] * (1.0 / l_sc[g]) for g in range(2))
    lv = lam_ref[...]
    lam = (jnp.exp(jnp.sum(lv[0:1] * lv[1:2], axis=-1, keepdims=True))
           - jnp.exp(jnp.sum(lv[2:3] * lv[3:4], axis=-1, keepdims=True)) + lambda_init)
    o = o0 - lam * o1
    ms = jnp.mean(o * o, axis=0, keepdims=True)
    on = (o * lax.rsqrt(ms + EPS)).T
    y2_ref[...] = (on * sg_ref[...] * (1.0 - lambda_init) * g2_ref[...]).astype(BF16)


def _diff_attention(dqz, dk, dvt, lam, subln_g, gate, batch, seq, layer, bounded):
    t = dk.shape[0]
    tq = DIFF_Q_TILE if bounded else Q_TILE
    nq = seq // tq
    n_chunks = seq // ROW_TILE
    lambda_init = 0.8 - 0.6 * math.exp(-0.3 * layer)
    tok = lambda b, q: b * nq + q
    scratch = [pltpu.VMEM((2, 2 * HEAD, tq), F32)] if bounded else [
        pltpu.VMEM((2, 1, tq), F32),
        pltpu.VMEM((2, 1, tq), F32),
        pltpu.VMEM((2, 2 * HEAD, tq), F32),
    ]
    return pl.pallas_call(
        functools.partial(_diff_kernel, lambda_init=lambda_init, bounded=bounded),
        grid=(batch, DIFF_HEADS, nq),
        in_specs=[
            pl.BlockSpec((2, tq, LANES), lambda b, h, q: (h, tok(b, q), 0)),
            pl.BlockSpec((seq, LANES), lambda b, h, q: (b, h)),
            pl.BlockSpec((n_chunks, LANES, ROW_TILE), lambda b, h, q: (b, h, 0)),
            pl.BlockSpec((4, HEAD), lambda b, h, q: (0, 0)),
            pl.BlockSpec((1, 2 * HEAD), lambda b, h, q: (0, 0)),
            pl.BlockSpec((tq, LANES), lambda b, h, q: (tok(b, q), 4 + h)),
        ],
        out_specs=pl.BlockSpec((tq, LANES), lambda b, h, q: (tok(b, q), h)),
        out_shape=jax.ShapeDtypeStruct((t, 512), BF16),
        scratch_shapes=scratch,
        compiler_params=_params(3),
        name="diff_attention" if bounded else "diff_attention_online",
    )(dqz, dk, dvt, lam, subln_g.reshape(1, 2 * HEAD), gate)


def _out_proj_kernel(y1_ref, y2_ref, ym_ref, x_ref, w_ref, o_ref):
    acc = _dot(y1_ref[...], w_ref[0:512, :])
    acc = acc + _dot(y2_ref[...], w_ref[512:1024, :])
    acc = acc + _dot(ym_ref[...], w_ref[1024:MIX_W, :])
    o_ref[...] = x_ref[...] + acc


def _out_proj(y1, y2, ym, x2, w):
    t = x2.shape[0]
    return pl.pallas_call(
        _out_proj_kernel,
        grid=(t // OUT_ROW_TILE,),
        in_specs=[
            pl.BlockSpec((OUT_ROW_TILE, 512), lambda i: (i, 0)),
            pl.BlockSpec((OUT_ROW_TILE, 512), lambda i: (i, 0)),
            pl.BlockSpec((OUT_ROW_TILE, 256), lambda i: (i, 0)),
            pl.BlockSpec((OUT_ROW_TILE, D_MODEL), lambda i: (i, 0)),
            pl.BlockSpec((MIX_W, D_MODEL), lambda i: (0, 0)),
        ],
        out_specs=pl.BlockSpec((OUT_ROW_TILE, D_MODEL), lambda i: (i, 0)),
        out_shape=jax.ShapeDtypeStruct((t, D_MODEL), F32),
        compiler_params=_params(1),
        name="out_proj",
    )(y1, y2, ym, x2, w)


def _rope_angles(pos, dim):
    inv = ROPE_THETA ** (-jnp.arange(0, dim, 2, dtype=F32) / dim)
    return pos.astype(F32)[:, None] * inv[None, :]


def _rope_tables(seq):
    pos = jnp.arange(seq)
    a1 = _rope_angles(pos, HEAD)
    cos1 = jnp.concatenate([jnp.cos(a1), jnp.cos(a1)], axis=-1)
    sin1 = jnp.concatenate([-jnp.sin(a1), jnp.sin(a1)], axis=-1)
    ar = _rope_angles(pos // GRID_W, HEAD // 2)
    ac = _rope_angles(pos % GRID_W, HEAD // 2)
    cosa = jnp.concatenate([jnp.cos(ar), jnp.cos(ar), jnp.cos(ac), jnp.cos(ac)], axis=-1)
    sina = jnp.concatenate([-jnp.sin(ar), jnp.sin(ar), -jnp.sin(ac), jnp.sin(ac)], axis=-1)
    rep = lambda a: jnp.tile(a, (1, LANES // HEAD))
    return rep(cos1), rep(sin1), rep(cosa), rep(sina)


def _scores_bounded(qk_g):
    bound = HEAD * Q_SCALE * 1.01 * jnp.max(jnp.abs(qk_g[0])) * jnp.max(jnp.abs(qk_g[1]))
    return bound <= SAFE_SCORE_BOUND


def _tile_gain(g, width):
    return jnp.tile(g.astype(F32), width // HEAD).reshape(1, width)


def _even_layer(x2, batch, seq, p, l, tables, mk, mvt):
    e = l // 2
    cos1, sin1, _, _ = tables
    y1, qz, k, vt, mqz, gate = _proj_even(
        x2, seq, p["norm_g"][l].reshape(1, D_MODEL), p["w_in"][l], cos1, sin1,
        _tile_gain(p["swa_qk_g"][e, 0], 512), _tile_gain(p["swa_qk_g"][e, 1], LANES),
        _tile_gain(p["mem_qk_g"][l, 0], 256), p["conv_w"][e], p["bd"])

    def tail_with_max(sink, qz, k, vt, mqz, mk, mvt, gate, y1, x2, w_out):
        y2, ym = _window_attention(sink, qz, k, vt, mqz, mk, mvt, gate, batch, seq)
        return _out_proj(y1, y2, ym, x2, w_out)

    sink_ok = jnp.max(jnp.abs(p["swa_sink"][e])) * LOG2E <= SAFE_SCORE_BOUND
    return lax.cond(
        _scores_bounded(p["swa_qk_g"][e]) & _scores_bounded(p["mem_qk_g"][l]) & sink_ok,
        functools.partial(_window_out_proj, batch=batch, seq=seq), tail_with_max,
        p["swa_sink"][e], qz, k, vt, mqz, mk, mvt, gate, y1, x2, p["w_out"][l])


def _odd_layer(x2, batch, seq, p, l, tables, mk, mvt):
    o = l // 2
    qz, k, vt, dqz, dk, dvt, mqz, gate = _proj_odd(
        x2, seq, p["norm_g"][l].reshape(1, D_MODEL), p["w_in"][l], *tables,
        _tile_gain(p["ax_qk_g"][o, 0], 512), _tile_gain(p["ax_qk_g"][o, 1], LANES),
        _tile_gain(p["diff_qk_g"][o, 0], 512), _tile_gain(p["diff_qk_g"][o, 1], 512),
        _tile_gain(p["mem_qk_g"][l, 0], 256), p["bd"])
    y1, ym = lax.cond(
        _scores_bounded(p["ax_qk_g"][o]) & _scores_bounded(p["mem_qk_g"][l]),
        functools.partial(_axial_attention, batch=batch, seq=seq, bounded=True),
        functools.partial(_axial_attention, batch=batch, seq=seq, bounded=False),
        qz, k, vt, mqz, mk, mvt, gate)
    y2 = lax.cond(
        _scores_bounded(p["diff_qk_g"][o]),
        functools.partial(_diff_attention, batch=batch, seq=seq, layer=l, bounded=True),
        functools.partial(_diff_attention, batch=batch, seq=seq, layer=l, bounded=False),
        dqz, dk, dvt, p["diff_lambda"][o], p["diff_subln_g"][o], gate)
    return _out_proj(y1, y2, ym, x2, p["w_out"][l])


def _trunk(x, mem, p):
    batch, seq, _ = x.shape
    x2 = x.reshape(batch * seq, D_MODEL)
    tables = _rope_tables(seq)
    gk_mem = jnp.stack([_tile_gain(p["mem_qk_g"][l, 1], 256) for l in range(DEPTH)])
    mk_all, mvt_all = _memkv(mem, p["mem_norm_g"], p["w_mem_kv"], gk_mem, p["bd"])
    for l in range(DEPTH):
        layer = _even_layer if l % 2 == 0 else _odd_layer
        x2 = layer(x2, batch, seq, p, l, tables, mk_all[l], mvt_all[l])
    return x2.reshape(batch, seq, D_MODEL)


def kernel(x_prompt, x_sample, mem_prompt, mem_sample, norm_g, w_in, w_out, mem_norm_g, w_mem_kv,
           mem_qk_g, conv_w, swa_qk_g, swa_sink, ax_qk_g, diff_qk_g, diff_lambda, diff_subln_g):
    grp = jnp.arange(2 * LANES) // HEAD
    p = dict(
        norm_g=norm_g, w_in=w_in.astype(BF16), w_out=w_out.astype(BF16), mem_norm_g=mem_norm_g,
        w_mem_kv=w_mem_kv.astype(BF16), mem_qk_g=mem_qk_g, conv_w=conv_w, swa_qk_g=swa_qk_g,
        swa_sink=swa_sink, ax_qk_g=ax_qk_g, diff_qk_g=diff_qk_g, diff_lambda=diff_lambda,
        diff_subln_g=diff_subln_g, bd=(grp[:, None] == grp[None, :]).astype(BF16))
    return (_trunk(x_prompt, mem_prompt, p), _trunk(x_sample, mem_sample, p))
```

```python
import functools
import math

import jax
import jax.numpy as jnp
from jax import lax
from jax.experimental import pallas as pl
from jax.experimental.pallas import tpu as pltpu

F32 = jnp.float32
BF16 = jnp.bfloat16

D_MODEL = 1024
DEPTH = 4
HEAD = 64
LANES = 128
SUBLANES = 8
N_Q_HEADS = 8
GQA_GROUP = 4
DIFF_HEADS = 4
MEM_HEADS = 4
CONV_W = 512
WINDOW = 128
GRID_W = 64
ROPE_THETA = 10000.0
EPS = 1e-6
NEG = -1e30
LOG2E = 1.4426950408889634
Q_SCALE = HEAD ** -0.5 * LOG2E
SAFE_SCORE_BOUND = 60.0

MIX_W = 1280


def _offsets(widths):
    out, start = {}, 0
    for name, w in widths:
        out[name] = (start, start + w)
        start += w
    return out


EVEN_COLS = _offsets([("gb", 512), ("gchc", 1024), ("q", 512), ("kv", 256), ("mq", 256), ("z", MIX_W)])
ODD_COLS = _offsets([("q", 512), ("kv", 256), ("dq", 512), ("dk", 512), ("dv", 512), ("mq", 256),
                     ("z", MIX_W)])
IN_W = EVEN_COLS["z"][1]
assert IN_W == ODD_COLS["z"][1] == 3840

ROW_TILE = 512
OUT_ROW_TILE = 1024
Q_TILE = 256
STREAM_Q_TILE = 512
WINDOW_Q_TILE = 1024
DIFF_Q_TILE = 1024
KV_CHUNK = 256
LOOP_GROUP = 32
WINDOW_SKEW = 2
KV_SKEW = 3
VMEM_LIMIT = 56 * 1024 * 1024

_NT = (((1,), (1,)), ((), ()))


def _params(n_axes):
    return pltpu.CompilerParams(dimension_semantics=("arbitrary",) * n_axes, vmem_limit_bytes=VMEM_LIMIT)


def _dot(a, b):
    return jnp.dot(a, b, preferred_element_type=F32)


def _dot_nt(a, b):
    return lax.dot_general(a, b, _NT, preferred_element_type=F32)


def _rms_rows(x, g):
    ms = jnp.mean(x * x, axis=-1, keepdims=True)
    return x * lax.rsqrt(ms + EPS) * g


def _head_sumsq(x, bd):
    w = x.shape[1]
    outs = []
    step = 2 * LANES if w % (2 * LANES) == 0 else LANES
    for c in range(0, w, step):
        sq = x[:, c:c + step]
        outs.append(_dot((sq * sq).astype(BF16), bd[:step, :step]))
    return outs[0] if len(outs) == 1 else jnp.concatenate(outs, axis=1)


def _head_rms(x, g, bd):
    return x * lax.rsqrt(_head_sumsq(x, bd) * (1.0 / HEAD) + EPS) * g


def _rope(x, cos, sin_signed, half):
    r = x.shape[0]
    lane = lax.broadcasted_iota(jnp.int32, (r, LANES), 1)
    first = (lane & (2 * half - 1)) < half
    outs = []
    for c in range(0, x.shape[1], LANES):
        xc = x[:, c:c + LANES]
        partner = jnp.where(first, pltpu.roll(xc, LANES - half, 1), pltpu.roll(xc, half, 1))
        outs.append(xc * cos + partner * sin_signed)
    return outs[0] if len(outs) == 1 else jnp.concatenate(outs, axis=1)


def _place_heads(x, target_half, out_ref):
    r = x.shape[0]
    low = lax.broadcasted_iota(jnp.int32, (r, LANES), 1) < HEAD
    for h in range(x.shape[1] // HEAD):
        xc = x[:, (h // 2) * LANES:(h // 2 + 1) * LANES]
        th = target_half(h)
        if th != h % 2:
            xc = pltpu.roll(xc, HEAD, 1)
        keep = low if th == 0 else jnp.logical_not(low)
        out_ref[h] = jnp.where(keep, xc, 0.0).astype(BF16)


def _silu(z):
    return z * (1.0 / (1.0 + jnp.exp(-z)))


def _branch_proj(h, w_ref, cols, name):
    a, b = cols[name]
    return _dot(h, w_ref[:, a:b])


def _memkv_kernel(mem_ref, g_ref, w_ref, gk_ref, bd_ref, mk_ref, mvt_ref):
    h = _rms_rows(mem_ref[0], g_ref[0]).astype(BF16)
    mkv = _dot(h, w_ref[0])
    half = mkv.shape[1] // 2
    mk = _head_rms(mkv[:, :half], gk_ref[0], bd_ref[...])
    mk_ref[0, 0] = mk.astype(BF16)
    mvt_ref[0, 0] = mkv[:, half:].T.astype(BF16)


def _memkv(mem, mem_norm_g, w_mem_kv, gk_tiled, bd):
    b, n_mem, _ = mem.shape
    width = MEM_HEADS * HEAD
    return pl.pallas_call(
        _memkv_kernel,
        grid=(DEPTH, b),
        in_specs=[
            pl.BlockSpec((1, n_mem, D_MODEL), lambda l, i: (i, 0, 0)),
            pl.BlockSpec((1, 1, D_MODEL), lambda l, i: (l, 0, 0)),
            pl.BlockSpec((1, D_MODEL, 2 * width), lambda l, i: (l, 0, 0)),
            pl.BlockSpec((1, 1, width), lambda l, i: (l, 0, 0)),
            pl.BlockSpec((2 * LANES, 2 * LANES), lambda l, i: (0, 0)),
        ],
        out_specs=[
            pl.BlockSpec((1, 1, n_mem, width), lambda l, i: (l, i, 0, 0)),
            pl.BlockSpec((1, 1, width, n_mem), lambda l, i: (l, i, 0, 0)),
        ],
        out_shape=[
            jax.ShapeDtypeStruct((DEPTH, b, n_mem, width), BF16),
            jax.ShapeDtypeStruct((DEPTH, b, width, n_mem), BF16),
        ],
        compiler_params=_params(2),
        name="mem_kv",
    )(mem, mem_norm_g.reshape(DEPTH, 1, D_MODEL), w_mem_kv, gk_tiled, bd)


def _proj_even_kernel(x_ref, xp_ref, xn_ref, g_ref, w_ref, cos_ref, sin_ref, gq_ref, gk_ref,
                      gmq_ref, cw_ref, bd_ref,
                      y1_ref, qz_ref, k_ref, vt_ref, mqz_ref, gate_ref, conv_sc, *, tiles_per_seq):
    i = pl.program_id(0)
    pos_tile = i % tiles_per_seq
    bd = bd_ref[...]
    cos = cos_ref[...]
    sin = sin_ref[...]
    rows = x_ref.shape[0]

    x_ext = jnp.concatenate([x_ref[...], xp_ref[...], xn_ref[...]], axis=0)
    h_ext = _rms_rows(x_ext, g_ref[...]).astype(BF16)
    h = h_ext[:rows]
    proj = functools.partial(_branch_proj, h, w_ref, EVEN_COLS)

    z = proj("z")
    gchc = _branch_proj(h_ext, w_ref, EVEN_COLS, "gchc")
    gate_ref[...] = _silu(z[:, CONV_W:MIX_W])
    gb = proj("gb")
    q = proj("q")

    inner_ext = gchc[:, 0:CONV_W] * gchc[:, CONV_W:2 * CONV_W]
    inner = inner_ext[:rows]
    prev_row = inner_ext[rows + SUBLANES - 1:rows + SUBLANES]
    next_row = inner_ext[rows + SUBLANES:rows + SUBLANES + 1]
    prev_row = jnp.where(pos_tile == 0, 0.0, prev_row)
    next_row = jnp.where(pos_tile == tiles_per_seq - 1, 0.0, next_row)
    conv_sc[pl.ds(SUBLANES, rows), :] = inner
    conv_sc[pl.ds(SUBLANES - 1, 1), :] = prev_row
    conv_sc[pl.ds(SUBLANES + rows, 1), :] = next_row
    cw = cw_ref[...]
    conv = (conv_sc[pl.ds(SUBLANES - 1, rows), :] * cw[0:1, :] + inner * cw[1:2, :]
            + conv_sc[pl.ds(SUBLANES + 1, rows), :] * cw[2:3, :])
    y1_ref[...] = (gb * conv * _silu(z[:, 0:CONV_W])).astype(BF16)

    mq = proj("mq")
    q = _rope(_head_rms(q, gq_ref[...], bd), cos, sin, HEAD // 2) * Q_SCALE
    _place_heads(q, lambda h: h // GQA_GROUP, qz_ref)

    kv = proj("kv")
    mq = _head_rms(mq, gmq_ref[...], bd) * Q_SCALE
    _place_heads(mq, lambda h: h % 2, mqz_ref)

    k = _rope(_head_rms(kv[:, 0:LANES], gk_ref[...], bd), cos, sin, HEAD // 2)
    k_ref[...] = k.astype(BF16)
    vt_ref[0] = kv[:, LANES:2 * LANES].T.astype(BF16)


def _proj_even(x2, seq, g, w, cos, sin, gq, gk, gmq, conv_w, bd):
    t = x2.shape[0]
    n_tiles = t // ROW_TILE
    tiles_per_seq = seq // ROW_TILE
    sub_per_tile = ROW_TILE // SUBLANES
    last_sub = t // SUBLANES - 1
    row = lambda w: pl.BlockSpec((1, w), lambda i: (0, 0))
    tab = pl.BlockSpec((ROW_TILE, LANES), lambda i: (i % tiles_per_seq, 0))
    return pl.pallas_call(
        functools.partial(_proj_even_kernel, tiles_per_seq=tiles_per_seq),
        grid=(n_tiles,),
        in_specs=[
            pl.BlockSpec((ROW_TILE, D_MODEL), lambda i: (i, 0)),
            pl.BlockSpec((SUBLANES, D_MODEL), lambda i: (jnp.maximum(i * sub_per_tile - 1, 0), 0)),
            pl.BlockSpec((SUBLANES, D_MODEL), lambda i: (jnp.minimum((i + 1) * sub_per_tile, last_sub), 0)),
            row(D_MODEL),
            pl.BlockSpec((D_MODEL, IN_W), lambda i: (0, 0)),
            tab, tab, row(512), row(LANES), row(256),
            pl.BlockSpec((3, CONV_W), lambda i: (0, 0)),
            pl.BlockSpec((2 * LANES, 2 * LANES), lambda i: (0, 0)),
        ],
        out_specs=[
            pl.BlockSpec((ROW_TILE, CONV_W), lambda i: (i, 0)),
            pl.BlockSpec((N_Q_HEADS, ROW_TILE, LANES), lambda i: (0, i, 0)),
            pl.BlockSpec((ROW_TILE, LANES), lambda i: (i, 0)),
            pl.BlockSpec((1, LANES, ROW_TILE), lambda i: (i, 0, 0)),
            pl.BlockSpec((MEM_HEADS, ROW_TILE, LANES), lambda i: (0, i, 0)),
            pl.BlockSpec((ROW_TILE, 768), lambda i: (i, 0)),
        ],
        out_shape=[
            jax.ShapeDtypeStruct((t, CONV_W), BF16),
            jax.ShapeDtypeStruct((N_Q_HEADS, t, LANES), BF16),
            jax.ShapeDtypeStruct((t, LANES), BF16),
            jax.ShapeDtypeStruct((n_tiles, LANES, ROW_TILE), BF16),
            jax.ShapeDtypeStruct((MEM_HEADS, t, LANES), BF16),
            jax.ShapeDtypeStruct((t, 768), F32),
        ],
        scratch_shapes=[pltpu.VMEM((ROW_TILE + 2 * SUBLANES, CONV_W), F32)],
        compiler_params=_params(1),
        name="proj_even",
    )(x2, x2, x2, g, w, cos, sin, gq, gk, gmq, conv_w, bd)


def _mem_scores(mqz_ref, mk_ref):
    pairs = []
    for grp in range(MEM_HEADS // 2):
        q2 = jnp.concatenate([mqz_ref[2 * grp], mqz_ref[2 * grp + 1]], axis=0)
        pairs.append(_dot_nt(mk_ref[0, :, grp * LANES:(grp + 1) * LANES], q2))
    return pairs


def _mem_finish(pairs, mvt_ref):
    tq = pairs[0].shape[1] // 2
    outs = []
    for h in range(MEM_HEADS):
        p = jnp.exp2(pairs[h // 2][:, (h % 2) * tq:(h % 2 + 1) * tq])
        l = jnp.sum(p, axis=0, keepdims=True)
        o = _dot(mvt_ref[0, h * HEAD:(h + 1) * HEAD, :], p.astype(BF16))
        outs.append(o * (1.0 / l))
    return jnp.concatenate(outs, axis=0)


def _mem_attention_t(mqz_ref, mk_ref, mvt_ref):
    outs = []
    for h in range(MEM_HEADS):
        grp = h // 2
        s = _dot_nt(mk_ref[0, :, grp * LANES:(grp + 1) * LANES], mqz_ref[h])
        m = jnp.max(s, axis=0, keepdims=True)
        p = jnp.exp2(s - m)
        l = jnp.sum(p, axis=0, keepdims=True)
        o = _dot(mvt_ref[0, h * HEAD:(h + 1) * HEAD, :], p.astype(BF16))
        outs.append(o * (1.0 / l))
    return jnp.concatenate(outs, axis=0)


def _window_kernel(sink_ref, qz_ref, kp_ref, kc_ref, kn_ref, vp_ref, vc_ref, vn_ref, mqz_ref, mk_ref,
                   mvt_ref, g2_ref, gm_ref, y2_ref, ym_ref, *, seq):
    tq = qz_ref.shape[1]
    t0 = pl.program_id(1) * tq
    kband = jnp.concatenate([kp_ref[...], kc_ref[...], kn_ref[...]], axis=0)
    vband = jnp.concatenate([vp_ref[0]] + [vc_ref[c] for c in range(vc_ref.shape[0])] + [vn_ref[0]],
                            axis=1)
    nk = tq + 2 * WINDOW
    j = lax.broadcasted_iota(jnp.int32, (nk, tq), 0)
    i = lax.broadcasted_iota(jnp.int32, (nk, tq), 1)
    kpos = j + (t0 - WINDOW)
    mask = (j >= i) & (j <= i + 2 * WINDOW) & (kpos >= 0) & (kpos < seq)
    outs = []
    for g in range(N_Q_HEADS):
        kv = g // GQA_GROUP
        s = jnp.where(mask, _dot_nt(kband, qz_ref[g]), NEG)
        sink = sink_ref[g] * LOG2E
        m = jnp.maximum(jnp.max(s, axis=0, keepdims=True), sink)
        p = jnp.exp2(s - m)
        l = jnp.sum(p, axis=0, keepdims=True) + jnp.exp2(sink - m)
        o = _dot(vband[kv * HEAD:(kv + 1) * HEAD, :], p.astype(BF16))
        outs.append(o * (1.0 / l))
    y2 = jnp.concatenate(outs, axis=0).T
    y2_ref[...] = (y2 * g2_ref[...]).astype(BF16)
    ym = _mem_attention_t(mqz_ref, mk_ref, mvt_ref).T
    ym_ref[...] = (ym * gm_ref[...]).astype(BF16)


def _window_out_kernel(sink_ref, qz_ref, kp_ref, kc_ref, kn_ref, vp_ref, vc_ref, vn_ref, mqz_ref,
                       mk_ref, mvt_ref, g2_ref, gm_ref, y1_ref, x_ref, w_ref, o_ref, y2_sc, ym_sc,
                       *, seq, n_tiles, tiles_per_seq):
    step = pl.program_id(0)
    tq = qz_ref.shape[1]
    t0 = (jnp.minimum(step, n_tiles - 1) % tiles_per_seq) * tq

    @pl.when(step == 0)
    def _():
        y2_sc[...] = jnp.zeros(y2_sc.shape, BF16)
        ym_sc[...] = jnp.zeros(ym_sc.shape, BF16)

    y2_prev = y2_sc[...]
    ym_prev = ym_sc[...]
    kband = jnp.concatenate([kp_ref[...], kc_ref[...], kn_ref[...]], axis=0)
    vband = jnp.concatenate([vp_ref[0]] + [vc_ref[c] for c in range(vc_ref.shape[0])] + [vn_ref[0]],
                            axis=1)
    nk = 3 * WINDOW
    j = lax.broadcasted_iota(jnp.int32, (nk, WINDOW), 0)
    i = lax.broadcasted_iota(jnp.int32, (nk, WINDOW), 1)
    in_band = (j >= i) & (j <= i + 2 * WINDOW)
    sink_p = [jnp.exp2(jnp.full((1, WINDOW), sink_ref[g] * LOG2E, F32)) for g in range(N_Q_HEADS)]
    n_kv = N_Q_HEADS // GQA_GROUP
    units = [(a, kv) for a in range(tq // WINDOW) for kv in range(n_kv)]

    def stacked_qt(a, kv):
        q4 = jnp.concatenate([qz_ref[kv * GQA_GROUP + r, a * WINDOW:(a + 1) * WINDOW, :]
                              for r in range(GQA_GROUP)], axis=0)
        return q4.astype(F32).T.astype(BF16)

    qts = {u: stacked_qt(*u) for u in units}

    def score(a, kv):
        return _dot(kband[a * WINDOW:a * WINDOW + nk], qts[(a, kv)])

    pending = [score(*u) for u in units[:WINDOW_SKEW]]
    mem_pairs = _mem_scores(mqz_ref, mk_ref)
    acc = _dot(y1_ref[...], w_ref[0:512, :])
    acc = acc + _dot(y2_prev, w_ref[512:1024, :])
    acc = acc + _dot(ym_prev, w_ref[1024:MIX_W, :])
    o_ref[...] = x_ref[...] + acc
    blocks = [[None] * (tq // WINDOW) for _ in range(N_Q_HEADS)]
    for n, (a, kv) in enumerate(units):
        if n + WINDOW_SKEW < len(units):
            pending.append(score(*units[n + WINDOW_SKEW]))
        kpos = j + (t0 + (a - 1) * WINDOW)
        bias1 = jnp.where(in_band & (kpos >= 0) & (kpos < seq), 0.0, NEG)
        bias = jnp.concatenate([bias1] * GQA_GROUP, axis=1)
        p = jnp.exp2(pending.pop(0) + bias)
        sinks = jnp.concatenate(sink_p[kv * GQA_GROUP:(kv + 1) * GQA_GROUP], axis=1)
        l = jnp.sum(p, axis=0, keepdims=True) + sinks
        o = _dot(vband[kv * HEAD:(kv + 1) * HEAD, a * WINDOW:a * WINDOW + nk], p.astype(BF16))
        o = o * (1.0 / l)
        for r in range(GQA_GROUP):
            blocks[kv * GQA_GROUP + r][a] = o[:, r * WINDOW:(r + 1) * WINDOW]
    y2 = jnp.concatenate([jnp.concatenate(b, axis=1) for b in blocks], axis=0).T
    y2_sc[...] = (y2 * g2_ref[...]).astype(BF16)
    ym = _mem_finish(mem_pairs, mvt_ref).T
    ym_sc[...] = (ym * gm_ref[...]).astype(BF16)


def _window_out_proj(sink, qz, k, vt, mqz, mk, mvt, gate, y1, x2, w_out, batch, seq):
    t = k.shape[0]
    tq = WINDOW_Q_TILE
    nq = seq // tq
    n_tiles = batch * nq
    w_per_q = tq // WINDOW
    w_per_seq = seq // WINDOW
    w_per_chunk = ROW_TILE // WINDOW
    n_mem = mk.shape[1]
    att = lambda i: jnp.minimum(i, n_tiles - 1)
    out = lambda i: jnp.maximum(i - 1, 0)

    def prev_w(i):
        b, q = att(i) // nq, att(i) % nq
        return b * w_per_seq + jnp.maximum(q * w_per_q - 1, 0)

    def next_w(i):
        b, q = att(i) // nq, att(i) % nq
        return b * w_per_seq + jnp.minimum((q + 1) * w_per_q, w_per_seq - 1)

    return pl.pallas_call(
        functools.partial(_window_out_kernel, seq=seq, n_tiles=n_tiles, tiles_per_seq=nq),
        grid=(n_tiles + 1,),
        in_specs=[
            pl.BlockSpec(memory_space=pltpu.SMEM),
            pl.BlockSpec((N_Q_HEADS, tq, LANES), lambda i: (0, att(i), 0)),
            pl.BlockSpec((WINDOW, LANES), lambda i: (prev_w(i), 0)),
            pl.BlockSpec((tq, LANES), lambda i: (att(i), 0)),
            pl.BlockSpec((WINDOW, LANES), lambda i: (next_w(i), 0)),
            pl.BlockSpec((1, LANES, WINDOW), lambda i: (prev_w(i) // w_per_chunk, 0, prev_w(i) % w_per_chunk)),
            pl.BlockSpec((tq // ROW_TILE, LANES, ROW_TILE), lambda i: (att(i), 0, 0)),
            pl.BlockSpec((1, LANES, WINDOW), lambda i: (next_w(i) // w_per_chunk, 0, next_w(i) % w_per_chunk)),
            pl.BlockSpec((MEM_HEADS, tq, LANES), lambda i: (0, att(i), 0)),
            pl.BlockSpec((1, n_mem, MEM_HEADS * HEAD), lambda i: (att(i) // nq, 0, 0)),
            pl.BlockSpec((1, MEM_HEADS * HEAD, n_mem), lambda i: (att(i) // nq, 0, 0)),
            pl.BlockSpec((tq, 512), lambda i: (att(i), 0)),
            pl.BlockSpec((tq, 256), lambda i: (att(i), 2)),
            pl.BlockSpec((tq, 512), lambda i: (out(i), 0)),
            pl.BlockSpec((tq, D_MODEL), lambda i: (out(i), 0)),
            pl.BlockSpec((MIX_W, D_MODEL), lambda i: (0, 0)),
        ],
        out_specs=pl.BlockSpec((tq, D_MODEL), lambda i: (out(i), 0)),
        out_shape=jax.ShapeDtypeStruct((t, D_MODEL), F32),
        scratch_shapes=[pltpu.VMEM((tq, 512), BF16), pltpu.VMEM((tq, 256), BF16)],
        compiler_params=_params(1),
        name="window_out_proj",
    )(sink, qz, k, k, k, vt, vt, vt, mqz, mk, mvt, gate, gate, y1, x2, w_out)


def _window_attention(sink, qz, k, vt, mqz, mk, mvt, gate, batch, seq):
    t = k.shape[0]
    tq = Q_TILE
    nq = seq // tq
    w_per_q = tq // WINDOW
    w_per_seq = seq // WINDOW
    w_per_chunk = ROW_TILE // WINDOW
    n_mem = mk.shape[1]
    q_per_chunk = ROW_TILE // tq
    cur_v = pl.BlockSpec((1, LANES, tq),
                         lambda b, q: ((b * nq + q) // q_per_chunk, 0, (b * nq + q) % q_per_chunk))

    def prev_w(b, q):
        return b * w_per_seq + jnp.maximum(q * w_per_q - 1, 0)

    def next_w(b, q):
        return b * w_per_seq + jnp.minimum((q + 1) * w_per_q, w_per_seq - 1)

    tok = lambda b, q: b * nq + q
    return pl.pallas_call(
        functools.partial(_window_kernel, seq=seq),
        grid=(batch, nq),
        in_specs=[
            pl.BlockSpec(memory_space=pltpu.SMEM),
            pl.BlockSpec((N_Q_HEADS, tq, LANES), lambda b, q: (0, tok(b, q), 0)),
            pl.BlockSpec((WINDOW, LANES), lambda b, q: (prev_w(b, q), 0)),
            pl.BlockSpec((tq, LANES), lambda b, q: (tok(b, q), 0)),
            pl.BlockSpec((WINDOW, LANES), lambda b, q: (next_w(b, q), 0)),
            pl.BlockSpec((1, LANES, WINDOW), lambda b, q: (prev_w(b, q) // w_per_chunk, 0, prev_w(b, q) % w_per_chunk)),
            cur_v,
            pl.BlockSpec((1, LANES, WINDOW), lambda b, q: (next_w(b, q) // w_per_chunk, 0, next_w(b, q) % w_per_chunk)),
            pl.BlockSpec((MEM_HEADS, tq, LANES), lambda b, q: (0, tok(b, q), 0)),
            pl.BlockSpec((1, n_mem, MEM_HEADS * HEAD), lambda b, q: (b, 0, 0)),
            pl.BlockSpec((1, MEM_HEADS * HEAD, n_mem), lambda b, q: (b, 0, 0)),
            pl.BlockSpec((tq, 512), lambda b, q: (tok(b, q), 0)),
            pl.BlockSpec((tq, 256), lambda b, q: (tok(b, q), 2)),
        ],
        out_specs=[
            pl.BlockSpec((tq, 512), lambda b, q: (tok(b, q), 0)),
            pl.BlockSpec((tq, 256), lambda b, q: (tok(b, q), 0)),
        ],
        out_shape=[
            jax.ShapeDtypeStruct((t, 512), BF16),
            jax.ShapeDtypeStruct((t, 256), BF16),
        ],
        compiler_params=_params(2),
        name="window_attention_max",
    )(sink, qz, k, k, k, vt, vt, vt, mqz, mk, mvt, gate, gate)


def _proj_odd_kernel(x_ref, g_ref, w_ref, cos1_ref, sin1_ref, cosa_ref, sina_ref, gq_ref, gk_ref,
                     gdq_ref, gdk_ref, gmq_ref, bd_ref,
                     qz_ref, k_ref, vt_ref, dqz_ref, dk_ref, dvt_ref, mqz_ref, gate_ref):
    bd = bd_ref[...]
    cos1, sin1 = cos1_ref[...], sin1_ref[...]
    cosa, sina = cosa_ref[...], sina_ref[...]
    h = _rms_rows(x_ref[...], g_ref[...]).astype(BF16)
    proj = functools.partial(_branch_proj, h, w_ref, ODD_COLS)

    z = proj("z")
    q = proj("q")
    gate_ref[...] = _silu(z)

    dq = proj("dq")
    q = _rope(_head_rms(q, gq_ref[...], bd), cosa, sina, HEAD // 4) * Q_SCALE
    _place_heads(q, lambda h: h // GQA_GROUP, qz_ref)

    dk = proj("dk")
    dq = _rope(_head_rms(dq, gdq_ref[...], bd), cos1, sin1, HEAD // 2) * Q_SCALE
    _place_heads(dq, lambda h: h % 2, dqz_ref)

    dv = proj("dv")
    dk = _rope(_head_rms(dk, gdk_ref[...], bd), cos1, sin1, HEAD // 2)
    dk_ref[...] = dk.astype(BF16)

    mq = proj("mq")
    dvt_ref[0] = dv.T.astype(BF16)

    kv = proj("kv")
    mq = _head_rms(mq, gmq_ref[...], bd) * Q_SCALE
    _place_heads(mq, lambda h: h % 2, mqz_ref)

    k = _rope(_head_rms(kv[:, 0:LANES], gk_ref[...], bd), cosa, sina, HEAD // 4)
    k_ref[...] = k.astype(BF16)
    vt_ref[0] = kv[:, LANES:2 * LANES].T.astype(BF16)


def _proj_odd(x2, seq, g, w, cos1, sin1, cosa, sina, gq, gk, gdq, gdk, gmq, bd):
    t = x2.shape[0]
    n_tiles = t // ROW_TILE
    tiles_per_seq = seq // ROW_TILE
    row = lambda w: pl.BlockSpec((1, w), lambda i: (0, 0))
    tab = pl.BlockSpec((ROW_TILE, LANES), lambda i: (i % tiles_per_seq, 0))
    return pl.pallas_call(
        _proj_odd_kernel,
        grid=(n_tiles,),
        in_specs=[
            pl.BlockSpec((ROW_TILE, D_MODEL), lambda i: (i, 0)),
            row(D_MODEL),
            pl.BlockSpec((D_MODEL, IN_W), lambda i: (0, 0)),
            tab, tab, tab, tab, row(512), row(LANES), row(512), row(512), row(256),
            pl.BlockSpec((2 * LANES, 2 * LANES), lambda i: (0, 0)),
        ],
        out_specs=[
            pl.BlockSpec((N_Q_HEADS, ROW_TILE, LANES), lambda i: (0, i, 0)),
            pl.BlockSpec((ROW_TILE, LANES), lambda i: (i, 0)),
            pl.BlockSpec((1, LANES, ROW_TILE), lambda i: (i, 0, 0)),
            pl.BlockSpec((N_Q_HEADS, ROW_TILE, LANES), lambda i: (0, i, 0)),
            pl.BlockSpec((ROW_TILE, 512), lambda i: (i, 0)),
            pl.BlockSpec((1, 512, ROW_TILE), lambda i: (i, 0, 0)),
            pl.BlockSpec((MEM_HEADS, ROW_TILE, LANES), lambda i: (0, i, 0)),
            pl.BlockSpec((ROW_TILE, MIX_W), lambda i: (i, 0)),
        ],
        out_shape=[
            jax.ShapeDtypeStruct((N_Q_HEADS, t, LANES), BF16),
            jax.ShapeDtypeStruct((t, LANES), BF16),
            jax.ShapeDtypeStruct((n_tiles, LANES, ROW_TILE), BF16),
            jax.ShapeDtypeStruct((N_Q_HEADS, t, LANES), BF16),
            jax.ShapeDtypeStruct((t, 512), BF16),
            jax.ShapeDtypeStruct((n_tiles, 512, ROW_TILE), BF16),
            jax.ShapeDtypeStruct((MEM_HEADS, t, LANES), BF16),
            jax.ShapeDtypeStruct((t, MIX_W), F32),
        ],
        compiler_params=_params(1),
        name="proj_odd",
    )(x2, g, w, cos1, sin1, cosa, sina, gq, gk, gdq, gdk, gmq, bd)


def _flash_t(qz_ref, k_ref, vt_ref, v_rows, m_sc, l_sc, acc_sc):
    n_maps = qz_ref.shape[0]
    n_chunks, _, chunk = vt_ref.shape
    m_sc[...] = jnp.full(m_sc.shape, NEG, F32)
    l_sc[...] = jnp.zeros(l_sc.shape, F32)
    acc_sc[...] = jnp.zeros(acc_sc.shape, F32)

    def body(c, carry):
        kc = k_ref[pl.ds(pl.multiple_of(c * chunk, chunk), chunk), :]
        vc = vt_ref[c]
        for g in range(n_maps):
            r0, nr = v_rows(g)
            s = _dot_nt(kc, qz_ref[g])
            m_old = m_sc[g]
            m_new = jnp.maximum(m_old, jnp.max(s, axis=0, keepdims=True))
            alpha = jnp.exp2(m_old - m_new)
            p = jnp.exp2(s - m_new)
            l_sc[g] = alpha * l_sc[g] + jnp.sum(p, axis=0, keepdims=True)
            acc_sc[g] = alpha * acc_sc[g] + _dot(vc[r0:r0 + nr, :], p.astype(BF16))
            m_sc[g] = m_new
        return carry

    lax.fori_loop(0, n_chunks, body, 0)


def _stream_t(qz_ref, k_ref, vt_ref, v_rows, emit):
    n_maps, tq, _ = qz_ref.shape
    n_tiles, _, tile = vt_ref.shape
    per_tile = tile // KV_CHUNK
    n_chunks = n_tiles * per_tile

    for g in range(n_maps):
        qt = qz_ref[g].astype(F32).T.astype(BF16)
        r0, nr = v_rows(g)

        def score(c, qt=qt):
            return _dot(k_ref[c * KV_CHUNK:(c + 1) * KV_CHUNK, :], qt)

        pending = [score(c) for c in range(min(KV_SKEW, n_chunks))]
        l8 = jnp.zeros((SUBLANES, tq), F32)
        pv = None
        for c in range(n_chunks):
            if c + KV_SKEW < n_chunks:
                pending.append(score(c + KV_SKEW))
            off = (c % per_tile) * KV_CHUNK
            p = jnp.exp2(pending.pop(0))
            l8 = l8 + jnp.sum(p.reshape(KV_CHUNK // SUBLANES, SUBLANES, tq), axis=0)
            d = _dot(vt_ref[c // per_tile, r0:r0 + nr, off:off + KV_CHUNK], p.astype(BF16))
            pv = d if pv is None else pv + d
        emit(g, pv * (1.0 / jnp.sum(l8, axis=0, keepdims=True)))


def _stream_loop_t(qz_ref, k_ref, vt_ref, nr, acc_sc):
    n_maps, tq, _ = qz_ref.shape
    n_tiles, _, tile = vt_ref.shape
    per_tile = tile // KV_CHUNK
    group = math.gcd(LOOP_GROUP, n_tiles * per_tile)
    assert group % per_tile == 0
    outs = []
    for g in range(n_maps):
        qg = qz_ref[g]
        acc_sc[g] = jnp.zeros((nr, tq), F32)

        def body(i, l8, g=g, qg=qg):
            where = [(i * (group // per_tile) + u // per_tile, (u % per_tile) * KV_CHUNK)
                     for u in range(group)]
            scores = []
            for t, off in where:
                kc = k_ref[pl.ds(pl.multiple_of(t * tile + off, KV_CHUNK), KV_CHUNK), :]
                scores.append(_dot_nt(kc, qg))
            pv = None
            for (t, off), s in zip(where, scores):
                p = jnp.exp2(s)
                l8 = l8 + jnp.sum(p.reshape(KV_CHUNK // SUBLANES, SUBLANES, tq), axis=0)
                d = _dot(vt_ref[t, 0:nr, off:off + KV_CHUNK], p.astype(BF16))
                pv = d if pv is None else pv + d
            acc_sc[g] += pv
            return l8

        l8 = lax.fori_loop(0, n_tiles * per_tile // group, body, jnp.zeros((SUBLANES, tq), F32))
        outs.append(acc_sc[g] * (1.0 / jnp.sum(l8, axis=0, keepdims=True)))
    return outs


def _axial_rows(g):
    return (g // GQA_GROUP) * HEAD, HEAD


def _axial_out_kernel(qz_ref, k_ref, vt_ref, mqz_ref, mk_ref, mvt_ref, g1_ref, gm_ref, y2_ref, x_ref,
                      w_ref, o_ref, y1_sc, ym_sc):
    @pl.when(pl.program_id(0) == 0)
    def _():
        y1_sc[...] = jnp.zeros(y1_sc.shape, BF16)
        ym_sc[...] = jnp.zeros(ym_sc.shape, BF16)

    acc = _dot(y1_sc[...], w_ref[0:512, :])
    acc = acc + _dot(y2_ref[...], w_ref[512:1024, :])
    acc = acc + _dot(ym_sc[...], w_ref[1024:MIX_W, :])
    o_ref[...] = x_ref[...] + acc

    mem_pairs = _mem_scores(mqz_ref, mk_ref)
    ym = _mem_finish(mem_pairs, mvt_ref).T
    ym_sc[...] = (ym * gm_ref[...]).astype(BF16)
    held = {}

    def emit(g, o):
        held[g] = o
        if g % 2 == 1:
            cols = slice((g // 2) * LANES, (g // 2 + 1) * LANES)
            y = jnp.concatenate([held.pop(g - 1), held.pop(g)], axis=0).T
            y1_sc[:, cols] = (y * g1_ref[:, cols]).astype(BF16)

    _stream_t(qz_ref, k_ref, vt_ref, _axial_rows, emit)


def _axial_out_proj(qz, k, vt, mqz, mk, mvt, gate, y2, x2, w_out, batch, seq):
    t = k.shape[0]
    tq = STREAM_Q_TILE
    nq = seq // tq
    n_tiles = batch * nq
    n_chunks = seq // ROW_TILE
    n_mem = mk.shape[1]
    att = lambda i: jnp.minimum(i, n_tiles - 1)
    out = lambda i: jnp.maximum(i - 1, 0)
    return pl.pallas_call(
        _axial_out_kernel,
        grid=(n_tiles + 1,),
        in_specs=[
            pl.BlockSpec((N_Q_HEADS, tq, LANES), lambda i: (0, att(i), 0)),
            pl.BlockSpec((seq, LANES), lambda i: (att(i) // nq, 0)),
            pl.BlockSpec((n_chunks, LANES, ROW_TILE), lambda i: (att(i) // nq, 0, 0)),
            pl.BlockSpec((MEM_HEADS, tq, LANES), lambda i: (0, att(i), 0)),
            pl.BlockSpec((1, n_mem, MEM_HEADS * HEAD), lambda i: (att(i) // nq, 0, 0)),
            pl.BlockSpec((1, MEM_HEADS * HEAD, n_mem), lambda i: (att(i) // nq, 0, 0)),
            pl.BlockSpec((tq, 512), lambda i: (att(i), 0)),
            pl.BlockSpec((tq, 256), lambda i: (att(i), 4)),
            pl.BlockSpec((tq, 512), lambda i: (out(i), 0)),
            pl.BlockSpec((tq, D_MODEL), lambda i: (out(i), 0)),
            pl.BlockSpec((MIX_W, D_MODEL), lambda i: (0, 0)),
        ],
        out_specs=pl.BlockSpec((tq, D_MODEL), lambda i: (out(i), 0)),
        out_shape=jax.ShapeDtypeStruct((t, D_MODEL), F32),
        scratch_shapes=[pltpu.VMEM((tq, 512), BF16), pltpu.VMEM((tq, 256), BF16)],
        compiler_params=_params(1),
        name="axial_out_proj",
    )(qz, k, vt, mqz, mk, mvt, gate, gate, y2, x2, w_out)


def _axial_kernel(qz_ref, k_ref, vt_ref, mqz_ref, mk_ref, mvt_ref, g1_ref, gm_ref, y1_ref, ym_ref,
                  m_sc, l_sc, acc_sc):
    ym = _mem_attention_t(mqz_ref, mk_ref, mvt_ref).T
    ym_ref[...] = (ym * gm_ref[...]).astype(BF16)
    _flash_t(qz_ref, k_ref, vt_ref, _axial_rows, m_sc, l_sc, acc_sc)
    outs = [acc_sc[g] * (1.0 / l_sc[g]) for g in range(N_Q_HEADS)]
    y1 = jnp.concatenate(outs, axis=0).T
    y1_ref[...] = (y1 * g1_ref[...]).astype(BF16)


def _axial_attention(qz, k, vt, mqz, mk, mvt, gate, batch, seq):
    t = k.shape[0]
    tq = Q_TILE
    nq = seq // tq
    n_chunks = seq // ROW_TILE
    n_mem = mk.shape[1]
    tok = lambda b, q: b * nq + q
    scratch = [
        pltpu.VMEM((N_Q_HEADS, 1, tq), F32),
        pltpu.VMEM((N_Q_HEADS, 1, tq), F32),
        pltpu.VMEM((N_Q_HEADS, HEAD, tq), F32),
    ]
    return pl.pallas_call(
        _axial_kernel,
        grid=(batch, nq),
        in_specs=[
            pl.BlockSpec((N_Q_HEADS, tq, LANES), lambda b, q: (0, tok(b, q), 0)),
            pl.BlockSpec((seq, LANES), lambda b, q: (b, 0)),
            pl.BlockSpec((n_chunks, LANES, ROW_TILE), lambda b, q: (b, 0, 0)),
            pl.BlockSpec((MEM_HEADS, tq, LANES), lambda b, q: (0, tok(b, q), 0)),
            pl.BlockSpec((1, n_mem, MEM_HEADS * HEAD), lambda b, q: (b, 0, 0)),
            pl.BlockSpec((1, MEM_HEADS * HEAD, n_mem), lambda b, q: (b, 0, 0)),
            pl.BlockSpec((tq, 512), lambda b, q: (tok(b, q), 0)),
            pl.BlockSpec((tq, 256), lambda b, q: (tok(b, q), 4)),
        ],
        out_specs=[
            pl.BlockSpec((tq, 512), lambda b, q: (tok(b, q), 0)),
            pl.BlockSpec((tq, 256), lambda b, q: (tok(b, q), 0)),
        ],
        out_shape=[
            jax.ShapeDtypeStruct((t, 512), BF16),
            jax.ShapeDtypeStruct((t, 256), BF16),
        ],
        scratch_shapes=scratch,
        compiler_params=_params(2),
        name="axial_attention_online",
    )(qz, k, vt, mqz, mk, mvt, gate, gate)


def _diff_kernel(qz_ref, k_ref, vt_ref, lam_ref, sg_ref, g2_ref, y2_ref, *scratch, lambda_init, bounded):
    v_rows = lambda g: (0, 2 * HEAD)
    if bounded:
        o0, o1 = _stream_loop_t(qz_ref, k_ref, vt_ref, 2 * HEAD, *scratch)
    else:
        m_sc, l_sc, acc_sc = scratch
        _flash_t(qz_ref, k_ref, vt_ref, v_rows, m_sc, l_sc, acc_sc)
        o0, o1 = (acc_sc[g] * (1.0 / l_sc[g]) for g in range(2))
    lv = lam_ref[...]
    lam = (jnp.exp(jnp.sum(lv[0:1] * lv[1:2], axis=-1, keepdims=True))
           - jnp.exp(jnp.sum(lv[2:3] * lv[3:4], axis=-1, keepdims=True)) + lambda_init)
    o = o0 - lam * o1
    ms = jnp.mean(o * o, axis=0, keepdims=True)
    on = (o * lax.rsqrt(ms + EPS)).T
    y2_ref[...] = (on * sg_ref[...] * (1.0 - lambda_init) * g2_ref[...]).astype(BF16)


def _diff_attention(dqz, dk, dvt, lam, subln_g, gate, batch, seq, layer, bounded):
    t = dk.shape[0]
    tq = DIFF_Q_TILE if bounded else Q_TILE
    nq = seq // tq
    n_chunks = seq // ROW_TILE
    lambda_init = 0.8 - 0.6 * math.exp(-0.3 * layer)
    tok = lambda b, q: b * nq + q
    scratch = [pltpu.VMEM((2, 2 * HEAD, tq), F32)] if bounded else [
        pltpu.VMEM((2, 1, tq), F32),
        pltpu.VMEM((2, 1, tq), F32),
        pltpu.VMEM((2, 2 * HEAD, tq), F32),
    ]
    return pl.pallas_call(
        functools.partial(_diff_kernel, lambda_init=lambda_init, bounded=bounded),
        grid=(batch, DIFF_HEADS, nq),
        in_specs=[
            pl.BlockSpec((2, tq, LANES), lambda b, h, q: (h, tok(b, q), 0)),
            pl.BlockSpec((seq, LANES), lambda b, h, q: (b, h)),
            pl.BlockSpec((n_chunks, LANES, ROW_TILE), lambda b, h, q: (b, h, 0)),
            pl.BlockSpec((4, HEAD), lambda b, h, q: (0, 0)),
            pl.BlockSpec((1, 2 * HEAD), lambda b, h, q: (0, 0)),
            pl.BlockSpec((tq, LANES), lambda b, h, q: (tok(b, q), 4 + h)),
        ],
        out_specs=pl.BlockSpec((tq, LANES), lambda b, h, q: (tok(b, q), h)),
        out_shape=jax.ShapeDtypeStruct((t, 512), BF16),
        scratch_shapes=scratch,
        compiler_params=_params(3),
        name="diff_attention" if bounded else "diff_attention_online",
    )(dqz, dk, dvt, lam, subln_g.reshape(1, 2 * HEAD), gate)


def _out_proj_kernel(y1_ref, y2_ref, ym_ref, x_ref, w_ref, o_ref):
    acc = _dot(y1_ref[...], w_ref[0:512, :])
    acc = acc + _dot(y2_ref[...], w_ref[512:1024, :])
    acc = acc + _dot(ym_ref[...], w_ref[1024:MIX_W, :])
    o_ref[...] = x_ref[...] + acc


def _out_proj(y1, y2, ym, x2, w):
    t = x2.shape[0]
    return pl.pallas_call(
        _out_proj_kernel,
        grid=(t // OUT_ROW_TILE,),
        in_specs=[
            pl.BlockSpec((OUT_ROW_TILE, 512), lambda i: (i, 0)),
            pl.BlockSpec((OUT_ROW_TILE, 512), lambda i: (i, 0)),
            pl.BlockSpec((OUT_ROW_TILE, 256), lambda i: (i, 0)),
            pl.BlockSpec((OUT_ROW_TILE, D_MODEL), lambda i: (i, 0)),
            pl.BlockSpec((MIX_W, D_MODEL), lambda i: (0, 0)),
        ],
        out_specs=pl.BlockSpec((OUT_ROW_TILE, D_MODEL), lambda i: (i, 0)),
        out_shape=jax.ShapeDtypeStruct((t, D_MODEL), F32),
        compiler_params=_params(1),
        name="out_proj",
    )(y1, y2, ym, x2, w)


def _rope_angles(pos, dim):
    inv = ROPE_THETA ** (-jnp.arange(0, dim, 2, dtype=F32) / dim)
    return pos.astype(F32)[:, None] * inv[None, :]


def _rope_tables(seq):
    pos = jnp.arange(seq)
    a1 = _rope_angles(pos, HEAD)
    cos1 = jnp.concatenate([jnp.cos(a1), jnp.cos(a1)], axis=-1)
    sin1 = jnp.concatenate([-jnp.sin(a1), jnp.sin(a1)], axis=-1)
    ar = _rope_angles(pos // GRID_W, HEAD // 2)
    ac = _rope_angles(pos % GRID_W, HEAD // 2)
    cosa = jnp.concatenate([jnp.cos(ar), jnp.cos(ar), jnp.cos(ac), jnp.cos(ac)], axis=-1)
    sina = jnp.concatenate([-jnp.sin(ar), jnp.sin(ar), -jnp.sin(ac), jnp.sin(ac)], axis=-1)
    rep = lambda a: jnp.tile(a, (1, LANES // HEAD))
    return rep(cos1), rep(sin1), rep(cosa), rep(sina)


def _scores_bounded(qk_g):
    bound = HEAD * Q_SCALE * 1.01 * jnp.max(jnp.abs(qk_g[0])) * jnp.max(jnp.abs(qk_g[1]))
    return bound <= SAFE_SCORE_BOUND


def _tile_gain(g, width):
    return jnp.tile(g.astype(F32), width // HEAD).reshape(1, width)


def _even_layer(x2, batch, seq, p, l, tables, mk, mvt):
    e = l // 2
    cos1, sin1, _, _ = tables
    y1, qz, k, vt, mqz, gate = _proj_even(
        x2, seq, p["norm_g"][l].reshape(1, D_MODEL), p["w_in"][l], cos1, sin1,
        _tile_gain(p["swa_qk_g"][e, 0], 512), _tile_gain(p["swa_qk_g"][e, 1], LANES),
        _tile_gain(p["mem_qk_g"][l, 0], 256), p["conv_w"][e], p["bd"])

    def tail_with_max(sink, qz, k, vt, mqz, mk, mvt, gate, y1, x2, w_out):
        y2, ym = _window_attention(sink, qz, k, vt, mqz, mk, mvt, gate, batch, seq)
        return _out_proj(y1, y2, ym, x2, w_out)

    sink_ok = jnp.max(jnp.abs(p["swa_sink"][e])) * LOG2E <= SAFE_SCORE_BOUND
    return lax.cond(
        _scores_bounded(p["swa_qk_g"][e]) & _scores_bounded(p["mem_qk_g"][l]) & sink_ok,
        functools.partial(_window_out_proj, batch=batch, seq=seq), tail_with_max,
        p["swa_sink"][e], qz, k, vt, mqz, mk, mvt, gate, y1, x2, p["w_out"][l])


def _odd_layer(x2, batch, seq, p, l, tables, mk, mvt):
    o = l // 2
    qz, k, vt, dqz, dk, dvt, mqz, gate = _proj_odd(
        x2, seq, p["norm_g"][l].reshape(1, D_MODEL), p["w_in"][l], *tables,
        _tile_gain(p["ax_qk_g"][o, 0], 512), _tile_gain(p["ax_qk_g"][o, 1], LANES),
        _tile_gain(p["diff_qk_g"][o, 0], 512), _tile_gain(p["diff_qk_g"][o, 1], 512),
        _tile_gain(p["mem_qk_g"][l, 0], 256), p["bd"])
    y2 = lax.cond(
        _scores_bounded(p["diff_qk_g"][o]),
        functools.partial(_diff_attention, batch=batch, seq=seq, layer=l, bounded=True),
        functools.partial(_diff_attention, batch=batch, seq=seq, layer=l, bounded=False),
        dqz, dk, dvt, p["diff_lambda"][o], p["diff_subln_g"][o], gate)

    def tail_with_max(qz, k, vt, mqz, mk, mvt, gate, y2, x2, w_out):
        y1, ym = _axial_attention(qz, k, vt, mqz, mk, mvt, gate, batch, seq)
        return _out_proj(y1, y2, ym, x2, w_out)

    return lax.cond(
        _scores_bounded(p["ax_qk_g"][o]) & _scores_bounded(p["mem_qk_g"][l]),
        functools.partial(_axial_out_proj, batch=batch, seq=seq), tail_with_max,
        qz, k, vt, mqz, mk, mvt, gate, y2, x2, p["w_out"][l])


def _trunk(x, mem, p):
    batch, seq, _ = x.shape
    x2 = x.reshape(batch * seq, D_MODEL)
    tables = _rope_tables(seq)
    gk_mem = jnp.stack([_tile_gain(p["mem_qk_g"][l, 1], 256) for l in range(DEPTH)])
    mk_all, mvt_all = _memkv(mem, p["mem_norm_g"], p["w_mem_kv"], gk_mem, p["bd"])
    for l in range(DEPTH):
        layer = _even_layer if l % 2 == 0 else _odd_layer
        x2 = layer(x2, batch, seq, p, l, tables, mk_all[l], mvt_all[l])
    return x2.reshape(batch, seq, D_MODEL)


def kernel(x_prompt, x_sample, mem_prompt, mem_sample, norm_g, w_in, w_out, mem_norm_g, w_mem_kv,
           mem_qk_g, conv_w, swa_qk_g, swa_sink, ax_qk_g, diff_qk_g, diff_lambda, diff_subln_g):
    grp = jnp.arange(2 * LANES) // HEAD
    p = dict(
        norm_g=norm_g, w_in=w_in.astype(BF16), w_out=w_out.astype(BF16), mem_norm_g=mem_norm_g,
        w_mem_kv=w_mem_kv.astype(BF16), mem_qk_g=mem_qk_g, conv_w=conv_w, swa_qk_g=swa_qk_g,
        swa_sink=swa_sink, ax_qk_g=ax_qk_g, diff_qk_g=diff_qk_g, diff_lambda=diff_lambda,
        diff_subln_g=diff_subln_g, bd=(grp[:, None] == grp[None, :]).astype(BF16))
    return (_trunk(x_prompt, mem_prompt, p), _trunk(x_sample, mem_sample, p))
```

```python
import functools
import math

import jax
import jax.numpy as jnp
from jax import lax
from jax.experimental import pallas as pl
from jax.experimental.pallas import tpu as pltpu

F32 = jnp.float32
BF16 = jnp.bfloat16

D_MODEL = 1024
DEPTH = 4
HEAD = 64
LANES = 128
SUBLANES = 8
N_Q_HEADS = 8
GQA_GROUP = 4
DIFF_HEADS = 4
MEM_HEADS = 4
CONV_W = 512
WINDOW = 128
GRID_W = 64
ROPE_THETA = 10000.0
EPS = 1e-6
NEG = -1e30
LOG2E = 1.4426950408889634
Q_SCALE = HEAD ** -0.5 * LOG2E
SAFE_SCORE_BOUND = 60.0

MIX_W = 1280


def _offsets(widths):
    out, start = {}, 0
    for name, w in widths:
        out[name] = (start, start + w)
        start += w
    return out


EVEN_COLS = _offsets([("gb", 512), ("gchc", 1024), ("q", 512), ("kv", 256), ("mq", 256), ("z", MIX_W)])
ODD_COLS = _offsets([("q", 512), ("kv", 256), ("dq", 512), ("dk", 512), ("dv", 512), ("mq", 256),
                     ("z", MIX_W)])
IN_W = EVEN_COLS["z"][1]
assert IN_W == ODD_COLS["z"][1] == 3840

ROW_TILE = 512
OUT_ROW_TILE = 1024
Q_TILE = 256
STREAM_Q_TILE = 512
WINDOW_Q_TILE = 1024
DIFF_Q_TILE = 1024
KV_CHUNK = 256
LOOP_GROUP = 32
WINDOW_SKEW = 2
KV_SKEW = 3
VMEM_LIMIT = 56 * 1024 * 1024

_NT = (((1,), (1,)), ((), ()))


def _params(n_axes):
    return pltpu.CompilerParams(dimension_semantics=("arbitrary",) * n_axes, vmem_limit_bytes=VMEM_LIMIT)


def _dot(a, b):
    return jnp.dot(a, b, preferred_element_type=F32)


def _dot_nt(a, b):
    return lax.dot_general(a, b, _NT, preferred_element_type=F32)


def _rms_rows(x, g):
    ms = jnp.mean(x * x, axis=-1, keepdims=True)
    return x * lax.rsqrt(ms + EPS) * g


def _head_sumsq(x, bd):
    w = x.shape[1]
    outs = []
    step = 2 * LANES if w % (2 * LANES) == 0 else LANES
    for c in range(0, w, step):
        sq = x[:, c:c + step]
        outs.append(_dot((sq * sq).astype(BF16), bd[:step, :step]))
    return outs[0] if len(outs) == 1 else jnp.concatenate(outs, axis=1)


def _head_rms(x, g, bd):
    return x * lax.rsqrt(_head_sumsq(x, bd) * (1.0 / HEAD) + EPS) * g


def _rope(x, cos, sin_signed, half):
    r = x.shape[0]
    lane = lax.broadcasted_iota(jnp.int32, (r, LANES), 1)
    first = (lane & (2 * half - 1)) < half
    outs = []
    for c in range(0, x.shape[1], LANES):
        xc = x[:, c:c + LANES]
        partner = jnp.where(first, pltpu.roll(xc, LANES - half, 1), pltpu.roll(xc, half, 1))
        outs.append(xc * cos + partner * sin_signed)
    return outs[0] if len(outs) == 1 else jnp.concatenate(outs, axis=1)


def _place_heads(x, target_half, out_ref):
    r = x.shape[0]
    low = lax.broadcasted_iota(jnp.int32, (r, LANES), 1) < HEAD
    for h in range(x.shape[1] // HEAD):
        xc = x[:, (h // 2) * LANES:(h // 2 + 1) * LANES]
        th = target_half(h)
        if th != h % 2:
            xc = pltpu.roll(xc, HEAD, 1)
        keep = low if th == 0 else jnp.logical_not(low)
        out_ref[h] = jnp.where(keep, xc, 0.0).astype(BF16)


def _silu(z):
    return z * (1.0 / (1.0 + jnp.exp(-z)))


def _branch_proj(h, w_ref, cols, name):
    a, b = cols[name]
    return _dot(h, w_ref[:, a:b])


def _memkv_kernel(mem_ref, g_ref, w_ref, gk_ref, bd_ref, mk_ref, mvt_ref):
    h = _rms_rows(mem_ref[0], g_ref[0]).astype(BF16)
    mkv = _dot(h, w_ref[0])
    half = mkv.shape[1] // 2
    mk = _head_rms(mkv[:, :half], gk_ref[0], bd_ref[...])
    mk_ref[0, 0] = mk.astype(BF16)
    mvt_ref[0, 0] = mkv[:, half:].T.astype(BF16)


def _memkv(mem, mem_norm_g, w_mem_kv, gk_tiled, bd):
    b, n_mem, _ = mem.shape
    width = MEM_HEADS * HEAD
    return pl.pallas_call(
        _memkv_kernel,
        grid=(DEPTH, b),
        in_specs=[
            pl.BlockSpec((1, n_mem, D_MODEL), lambda l, i: (i, 0, 0)),
            pl.BlockSpec((1, 1, D_MODEL), lambda l, i: (l, 0, 0)),
            pl.BlockSpec((1, D_MODEL, 2 * width), lambda l, i: (l, 0, 0)),
            pl.BlockSpec((1, 1, width), lambda l, i: (l, 0, 0)),
            pl.BlockSpec((2 * LANES, 2 * LANES), lambda l, i: (0, 0)),
        ],
        out_specs=[
            pl.BlockSpec((1, 1, n_mem, width), lambda l, i: (l, i, 0, 0)),
            pl.BlockSpec((1, 1, width, n_mem), lambda l, i: (l, i, 0, 0)),
        ],
        out_shape=[
            jax.ShapeDtypeStruct((DEPTH, b, n_mem, width), BF16),
            jax.ShapeDtypeStruct((DEPTH, b, width, n_mem), BF16),
        ],
        compiler_params=_params(2),
        name="mem_kv",
    )(mem, mem_norm_g.reshape(DEPTH, 1, D_MODEL), w_mem_kv, gk_tiled, bd)


def _proj_even_kernel(x_ref, xp_ref, xn_ref, g_ref, w_ref, cos_ref, sin_ref, gq_ref, gk_ref,
                      gmq_ref, cw_ref, bd_ref,
                      y1_ref, qz_ref, k_ref, vt_ref, mqz_ref, gate_ref, conv_sc, *, tiles_per_seq):
    i = pl.program_id(0)
    pos_tile = i % tiles_per_seq
    bd = bd_ref[...]
    cos = cos_ref[...]
    sin = sin_ref[...]
    rows = x_ref.shape[0]

    x_ext = jnp.concatenate([x_ref[...], xp_ref[...], xn_ref[...]], axis=0)
    h_ext = _rms_rows(x_ext, g_ref[...]).astype(BF16)
    h = h_ext[:rows]
    proj = functools.partial(_branch_proj, h, w_ref, EVEN_COLS)

    z = proj("z")
    gchc = _branch_proj(h_ext, w_ref, EVEN_COLS, "gchc")
    gate_ref[...] = _silu(z[:, CONV_W:MIX_W])
    q = proj("q")

    inner_ext = gchc[:, 0:CONV_W] * gchc[:, CONV_W:2 * CONV_W]
    inner = inner_ext[:rows]
    prev_row = inner_ext[rows + SUBLANES - 1:rows + SUBLANES]
    next_row = inner_ext[rows + SUBLANES:rows + SUBLANES + 1]
    prev_row = jnp.where(pos_tile == 0, 0.0, prev_row)
    next_row = jnp.where(pos_tile == tiles_per_seq - 1, 0.0, next_row)
    conv_sc[pl.ds(SUBLANES, rows), :] = inner
    conv_sc[pl.ds(SUBLANES - 1, 1), :] = prev_row
    conv_sc[pl.ds(SUBLANES + rows, 1), :] = next_row
    cw = cw_ref[...]
    conv = (conv_sc[pl.ds(SUBLANES - 1, rows), :] * cw[0:1, :] + inner * cw[1:2, :]
            + conv_sc[pl.ds(SUBLANES + 1, rows), :] * cw[2:3, :])
    gated_conv = conv * _silu(z[:, 0:CONV_W])

    mq = proj("mq")
    q = _rope(_head_rms(q, gq_ref[...], bd), cos, sin, HEAD // 2) * Q_SCALE
    _place_heads(q, lambda h: h // GQA_GROUP, qz_ref)

    kv = proj("kv")
    mq = _head_rms(mq, gmq_ref[...], bd) * Q_SCALE
    _place_heads(mq, lambda h: h % 2, mqz_ref)

    gb = proj("gb")
    k = _rope(_head_rms(kv[:, 0:LANES], gk_ref[...], bd), cos, sin, HEAD // 2)
    k_ref[...] = k.astype(BF16)
    vt_ref[0] = kv[:, LANES:2 * LANES].T.astype(BF16)

    y1_ref[...] = (gb * gated_conv).astype(BF16)


def _proj_even(x2, seq, g, w, cos, sin, gq, gk, gmq, conv_w, bd):
    t = x2.shape[0]
    n_tiles = t // ROW_TILE
    tiles_per_seq = seq // ROW_TILE
    sub_per_tile = ROW_TILE // SUBLANES
    last_sub = t // SUBLANES - 1
    row = lambda w: pl.BlockSpec((1, w), lambda i: (0, 0))
    tab = pl.BlockSpec((ROW_TILE, LANES), lambda i: (i % tiles_per_seq, 0))
    return pl.pallas_call(
        functools.partial(_proj_even_kernel, tiles_per_seq=tiles_per_seq),
        grid=(n_tiles,),
        in_specs=[
            pl.BlockSpec((ROW_TILE, D_MODEL), lambda i: (i, 0)),
            pl.BlockSpec((SUBLANES, D_MODEL), lambda i: (jnp.maximum(i * sub_per_tile - 1, 0), 0)),
            pl.BlockSpec((SUBLANES, D_MODEL), lambda i: (jnp.minimum((i + 1) * sub_per_tile, last_sub), 0)),
            row(D_MODEL),
            pl.BlockSpec((D_MODEL, IN_W), lambda i: (0, 0)),
            tab, tab, row(512), row(LANES), row(256),
            pl.BlockSpec((3, CONV_W), lambda i: (0, 0)),
            pl.BlockSpec((2 * LANES, 2 * LANES), lambda i: (0, 0)),
        ],
        out_specs=[
            pl.BlockSpec((ROW_TILE, CONV_W), lambda i: (i, 0)),
            pl.BlockSpec((N_Q_HEADS, ROW_TILE, LANES), lambda i: (0, i, 0)),
            pl.BlockSpec((ROW_TILE, LANES), lambda i: (i, 0)),
            pl.BlockSpec((1, LANES, ROW_TILE), lambda i: (i, 0, 0)),
            pl.BlockSpec((MEM_HEADS, ROW_TILE, LANES), lambda i: (0, i, 0)),
            pl.BlockSpec((ROW_TILE, 768), lambda i: (i, 0)),
        ],
        out_shape=[
            jax.ShapeDtypeStruct((t, CONV_W), BF16),
            jax.ShapeDtypeStruct((N_Q_HEADS, t, LANES), BF16),
            jax.ShapeDtypeStruct((t, LANES), BF16),
            jax.ShapeDtypeStruct((n_tiles, LANES, ROW_TILE), BF16),
            jax.ShapeDtypeStruct((MEM_HEADS, t, LANES), BF16),
            jax.ShapeDtypeStruct((t, 768), F32),
        ],
        scratch_shapes=[pltpu.VMEM((ROW_TILE + 2 * SUBLANES, CONV_W), F32)],
        compiler_params=_params(1),
        name="proj_even",
    )(x2, x2, x2, g, w, cos, sin, gq, gk, gmq, conv_w, bd)


def _mem_scores(mqz_ref, mk_ref):
    pairs = []
    for grp in range(MEM_HEADS // 2):
        q2 = jnp.concatenate([mqz_ref[2 * grp], mqz_ref[2 * grp + 1]], axis=0)
        pairs.append(_dot_nt(mk_ref[0, :, grp * LANES:(grp + 1) * LANES], q2))
    return pairs


def _mem_finish(pairs, mvt_ref):
    tq = pairs[0].shape[1] // 2
    outs = []
    for h in range(MEM_HEADS):
        p = jnp.exp2(pairs[h // 2][:, (h % 2) * tq:(h % 2 + 1) * tq])
        l = jnp.sum(p, axis=0, keepdims=True)
        o = _dot(mvt_ref[0, h * HEAD:(h + 1) * HEAD, :], p.astype(BF16))
        outs.append(o * (1.0 / l))
    return jnp.concatenate(outs, axis=0)


def _mem_attention_t(mqz_ref, mk_ref, mvt_ref):
    outs = []
    for h in range(MEM_HEADS):
        grp = h // 2
        s = _dot_nt(mk_ref[0, :, grp * LANES:(grp + 1) * LANES], mqz_ref[h])
        m = jnp.max(s, axis=0, keepdims=True)
        p = jnp.exp2(s - m)
        l = jnp.sum(p, axis=0, keepdims=True)
        o = _dot(mvt_ref[0, h * HEAD:(h + 1) * HEAD, :], p.astype(BF16))
        outs.append(o * (1.0 / l))
    return jnp.concatenate(outs, axis=0)


def _window_kernel(sink_ref, qz_ref, kp_ref, kc_ref, kn_ref, vp_ref, vc_ref, vn_ref, mqz_ref, mk_ref,
                   mvt_ref, g2_ref, gm_ref, y2_ref, ym_ref, *, seq):
    tq = qz_ref.shape[1]
    t0 = pl.program_id(1) * tq
    kband = jnp.concatenate([kp_ref[...], kc_ref[...], kn_ref[...]], axis=0)
    vband = jnp.concatenate([vp_ref[0]] + [vc_ref[c] for c in range(vc_ref.shape[0])] + [vn_ref[0]],
                            axis=1)
    nk = tq + 2 * WINDOW
    j = lax.broadcasted_iota(jnp.int32, (nk, tq), 0)
    i = lax.broadcasted_iota(jnp.int32, (nk, tq), 1)
    kpos = j + (t0 - WINDOW)
    mask = (j >= i) & (j <= i + 2 * WINDOW) & (kpos >= 0) & (kpos < seq)
    outs = []
    for g in range(N_Q_HEADS):
        kv = g // GQA_GROUP
        s = jnp.where(mask, _dot_nt(kband, qz_ref[g]), NEG)
        sink = sink_ref[g] * LOG2E
        m = jnp.maximum(jnp.max(s, axis=0, keepdims=True), sink)
        p = jnp.exp2(s - m)
        l = jnp.sum(p, axis=0, keepdims=True) + jnp.exp2(sink - m)
        o = _dot(vband[kv * HEAD:(kv + 1) * HEAD, :], p.astype(BF16))
        outs.append(o * (1.0 / l))
    y2 = jnp.concatenate(outs, axis=0).T
    y2_ref[...] = (y2 * g2_ref[...]).astype(BF16)
    ym = _mem_attention_t(mqz_ref, mk_ref, mvt_ref).T
    ym_ref[...] = (ym * gm_ref[...]).astype(BF16)


def _window_out_kernel(sink_ref, qz_ref, kp_ref, kc_ref, kn_ref, vp_ref, vc_ref, vn_ref, mqz_ref,
                       mk_ref, mvt_ref, g2_ref, gm_ref, y1_ref, x_ref, w_ref, o_ref, y2_sc, ym_sc,
                       *, seq, n_tiles, tiles_per_seq):
    step = pl.program_id(0)
    tq = qz_ref.shape[1]
    t0 = (jnp.minimum(step, n_tiles - 1) % tiles_per_seq) * tq

    @pl.when(step == 0)
    def _():
        y2_sc[...] = jnp.zeros(y2_sc.shape, BF16)
        ym_sc[...] = jnp.zeros(ym_sc.shape, BF16)

    y2_prev = y2_sc[...]
    ym_prev = ym_sc[...]
    kband = jnp.concatenate([kp_ref[...], kc_ref[...], kn_ref[...]], axis=0)
    vband = jnp.concatenate([vp_ref[0]] + [vc_ref[c] for c in range(vc_ref.shape[0])] + [vn_ref[0]],
                            axis=1)
    nk = 3 * WINDOW
    j = lax.broadcasted_iota(jnp.int32, (nk, WINDOW), 0)
    i = lax.broadcasted_iota(jnp.int32, (nk, WINDOW), 1)
    in_band = (j >= i) & (j <= i + 2 * WINDOW)
    sink_p = [jnp.exp2(jnp.full((1, WINDOW), sink_ref[g] * LOG2E, F32)) for g in range(N_Q_HEADS)]
    n_kv = N_Q_HEADS // GQA_GROUP
    units = [(a, kv) for a in range(tq // WINDOW) for kv in range(n_kv)]

    def stacked_qt(a, kv):
        q4 = jnp.concatenate([qz_ref[kv * GQA_GROUP + r, a * WINDOW:(a + 1) * WINDOW, :]
                              for r in range(GQA_GROUP)], axis=0)
        return q4.astype(F32).T.astype(BF16)

    qts = {u: stacked_qt(*u) for u in units}

    def score(a, kv):
        return _dot(kband[a * WINDOW:a * WINDOW + nk], qts[(a, kv)])

    pending = [score(*u) for u in units[:WINDOW_SKEW]]
    mem_pairs = _mem_scores(mqz_ref, mk_ref)
    acc = _dot(y1_ref[...], w_ref[0:512, :])
    acc = acc + _dot(y2_prev, w_ref[512:1024, :])
    acc = acc + _dot(ym_prev, w_ref[1024:MIX_W, :])
    o_ref[...] = x_ref[...] + acc
    blocks = [[None] * (tq // WINDOW) for _ in range(N_Q_HEADS)]
    for n, (a, kv) in enumerate(units):
        if n + WINDOW_SKEW < len(units):
            pending.append(score(*units[n + WINDOW_SKEW]))
        kpos = j + (t0 + (a - 1) * WINDOW)
        bias1 = jnp.where(in_band & (kpos >= 0) & (kpos < seq), 0.0, NEG)
        bias = jnp.concatenate([bias1] * GQA_GROUP, axis=1)
        p = jnp.exp2(pending.pop(0) + bias)
        sinks = jnp.concatenate(sink_p[kv * GQA_GROUP:(kv + 1) * GQA_GROUP], axis=1)
        l = jnp.sum(p, axis=0, keepdims=True) + sinks
        o = _dot(vband[kv * HEAD:(kv + 1) * HEAD, a * WINDOW:a * WINDOW + nk], p.astype(BF16))
        o = o * (1.0 / l)
        for r in range(GQA_GROUP):
            blocks[kv * GQA_GROUP + r][a] = o[:, r * WINDOW:(r + 1) * WINDOW]
    y2 = jnp.concatenate([jnp.concatenate(b, axis=1) for b in blocks], axis=0).T
    y2_sc[...] = (y2 * g2_ref[...]).astype(BF16)
    ym = _mem_finish(mem_pairs, mvt_ref).T
    ym_sc[...] = (ym * gm_ref[...]).astype(BF16)


def _window_out_proj(sink, qz, k, vt, mqz, mk, mvt, gate, y1, x2, w_out, batch, seq):
    t = k.shape[0]
    tq = WINDOW_Q_TILE
    nq = seq // tq
    n_tiles = batch * nq
    w_per_q = tq // WINDOW
    w_per_seq = seq // WINDOW
    w_per_chunk = ROW_TILE // WINDOW
    n_mem = mk.shape[1]
    att = lambda i: jnp.minimum(i, n_tiles - 1)
    out = lambda i: jnp.maximum(i - 1, 0)

    def prev_w(i):
        b, q = att(i) // nq, att(i) % nq
        return b * w_per_seq + jnp.maximum(q * w_per_q - 1, 0)

    def next_w(i):
        b, q = att(i) // nq, att(i) % nq
        return b * w_per_seq + jnp.minimum((q + 1) * w_per_q, w_per_seq - 1)

    return pl.pallas_call(
        functools.partial(_window_out_kernel, seq=seq, n_tiles=n_tiles, tiles_per_seq=nq),
        grid=(n_tiles + 1,),
        in_specs=[
            pl.BlockSpec(memory_space=pltpu.SMEM),
            pl.BlockSpec((N_Q_HEADS, tq, LANES), lambda i: (0, att(i), 0)),
            pl.BlockSpec((WINDOW, LANES), lambda i: (prev_w(i), 0)),
            pl.BlockSpec((tq, LANES), lambda i: (att(i), 0)),
            pl.BlockSpec((WINDOW, LANES), lambda i: (next_w(i), 0)),
            pl.BlockSpec((1, LANES, WINDOW), lambda i: (prev_w(i) // w_per_chunk, 0, prev_w(i) % w_per_chunk)),
            pl.BlockSpec((tq // ROW_TILE, LANES, ROW_TILE), lambda i: (att(i), 0, 0)),
            pl.BlockSpec((1, LANES, WINDOW), lambda i: (next_w(i) // w_per_chunk, 0, next_w(i) % w_per_chunk)),
            pl.BlockSpec((MEM_HEADS, tq, LANES), lambda i: (0, att(i), 0)),
            pl.BlockSpec((1, n_mem, MEM_HEADS * HEAD), lambda i: (att(i) // nq, 0, 0)),
            pl.BlockSpec((1, MEM_HEADS * HEAD, n_mem), lambda i: (att(i) // nq, 0, 0)),
            pl.BlockSpec((tq, 512), lambda i: (att(i), 0)),
            pl.BlockSpec((tq, 256), lambda i: (att(i), 2)),
            pl.BlockSpec((tq, 512), lambda i: (out(i), 0)),
            pl.BlockSpec((tq, D_MODEL), lambda i: (out(i), 0)),
            pl.BlockSpec((MIX_W, D_MODEL), lambda i: (0, 0)),
        ],
        out_specs=pl.BlockSpec((tq, D_MODEL), lambda i: (out(i), 0)),
        out_shape=jax.ShapeDtypeStruct((t, D_MODEL), F32),
        scratch_shapes=[pltpu.VMEM((tq, 512), BF16), pltpu.VMEM((tq, 256), BF16)],
        compiler_params=_params(1),
        name="window_out_proj",
    )(sink, qz, k, k, k, vt, vt, vt, mqz, mk, mvt, gate, gate, y1, x2, w_out)


def _window_attention(sink, qz, k, vt, mqz, mk, mvt, gate, batch, seq):
    t = k.shape[0]
    tq = Q_TILE
    nq = seq // tq
    w_per_q = tq // WINDOW
    w_per_seq = seq // WINDOW
    w_per_chunk = ROW_TILE // WINDOW
    n_mem = mk.shape[1]
    q_per_chunk = ROW_TILE // tq
    cur_v = pl.BlockSpec((1, LANES, tq),
                         lambda b, q: ((b * nq + q) // q_per_chunk, 0, (b * nq + q) % q_per_chunk))

    def prev_w(b, q):
        return b * w_per_seq + jnp.maximum(q * w_per_q - 1, 0)

    def next_w(b, q):
        return b * w_per_seq + jnp.minimum((q + 1) * w_per_q, w_per_seq - 1)

    tok = lambda b, q: b * nq + q
    return pl.pallas_call(
        functools.partial(_window_kernel, seq=seq),
        grid=(batch, nq),
        in_specs=[
            pl.BlockSpec(memory_space=pltpu.SMEM),
            pl.BlockSpec((N_Q_HEADS, tq, LANES), lambda b, q: (0, tok(b, q), 0)),
            pl.BlockSpec((WINDOW, LANES), lambda b, q: (prev_w(b, q), 0)),
            pl.BlockSpec((tq, LANES), lambda b, q: (tok(b, q), 0)),
            pl.BlockSpec((WINDOW, LANES), lambda b, q: (next_w(b, q), 0)),
            pl.BlockSpec((1, LANES, WINDOW), lambda b, q: (prev_w(b, q) // w_per_chunk, 0, prev_w(b, q) % w_per_chunk)),
            cur_v,
            pl.BlockSpec((1, LANES, WINDOW), lambda b, q: (next_w(b, q) // w_per_chunk, 0, next_w(b, q) % w_per_chunk)),
            pl.BlockSpec((MEM_HEADS, tq, LANES), lambda b, q: (0, tok(b, q), 0)),
            pl.BlockSpec((1, n_mem, MEM_HEADS * HEAD), lambda b, q: (b, 0, 0)),
            pl.BlockSpec((1, MEM_HEADS * HEAD, n_mem), lambda b, q: (b, 0, 0)),
            pl.BlockSpec((tq, 512), lambda b, q: (tok(b, q), 0)),
            pl.BlockSpec((tq, 256), lambda b, q: (tok(b, q), 2)),
        ],
        out_specs=[
            pl.BlockSpec((tq, 512), lambda b, q: (tok(b, q), 0)),
            pl.BlockSpec((tq, 256), lambda b, q: (tok(b, q), 0)),
        ],
        out_shape=[
            jax.ShapeDtypeStruct((t, 512), BF16),
            jax.ShapeDtypeStruct((t, 256), BF16),
        ],
        compiler_params=_params(2),
        name="window_attention_max",
    )(sink, qz, k, k, k, vt, vt, vt, mqz, mk, mvt, gate, gate)


def _proj_odd_kernel(x_ref, g_ref, w_ref, cos1_ref, sin1_ref, cosa_ref, sina_ref, gq_ref, gk_ref,
                     gdq_ref, gdk_ref, gmq_ref, bd_ref,
                     qz_ref, k_ref, vt_ref, dqz_ref, dk_ref, dvt_ref, mqz_ref, gate_ref):
    bd = bd_ref[...]
    cos1, sin1 = cos1_ref[...], sin1_ref[...]
    cosa, sina = cosa_ref[...], sina_ref[...]
    h = _rms_rows(x_ref[...], g_ref[...]).astype(BF16)
    proj = functools.partial(_branch_proj, h, w_ref, ODD_COLS)

    z = proj("z")
    q = proj("q")
    gate_ref[...] = _silu(z)

    dq = proj("dq")
    q = _rope(_head_rms(q, gq_ref[...], bd), cosa, sina, HEAD // 4) * Q_SCALE
    _place_heads(q, lambda h: h // GQA_GROUP, qz_ref)

    dk = proj("dk")
    dq = _rope(_head_rms(dq, gdq_ref[...], bd), cos1, sin1, HEAD // 2) * Q_SCALE
    _place_heads(dq, lambda h: h % 2, dqz_ref)

    kv = proj("kv")
    dk = _rope(_head_rms(dk, gdk_ref[...], bd), cos1, sin1, HEAD // 2)
    dk_ref[...] = dk.astype(BF16)

    mq = proj("mq")
    k = _rope(_head_rms(kv[:, 0:LANES], gk_ref[...], bd), cosa, sina, HEAD // 4)
    k_ref[...] = k.astype(BF16)
    vt_ref[0] = kv[:, LANES:2 * LANES].T.astype(BF16)

    dv = proj("dv")
    mq = _head_rms(mq, gmq_ref[...], bd) * Q_SCALE
    _place_heads(mq, lambda h: h % 2, mqz_ref)

    dvt_ref[0] = dv.T.astype(BF16)


def _proj_odd(x2, seq, g, w, cos1, sin1, cosa, sina, gq, gk, gdq, gdk, gmq, bd):
    t = x2.shape[0]
    n_tiles = t // ROW_TILE
    tiles_per_seq = seq // ROW_TILE
    row = lambda w: pl.BlockSpec((1, w), lambda i: (0, 0))
    tab = pl.BlockSpec((ROW_TILE, LANES), lambda i: (i % tiles_per_seq, 0))
    return pl.pallas_call(
        _proj_odd_kernel,
        grid=(n_tiles,),
        in_specs=[
            pl.BlockSpec((ROW_TILE, D_MODEL), lambda i: (i, 0)),
            row(D_MODEL),
            pl.BlockSpec((D_MODEL, IN_W), lambda i: (0, 0)),
            tab, tab, tab, tab, row(512), row(LANES), row(512), row(512), row(256),
            pl.BlockSpec((2 * LANES, 2 * LANES), lambda i: (0, 0)),
        ],
        out_specs=[
            pl.BlockSpec((N_Q_HEADS, ROW_TILE, LANES), lambda i: (0, i, 0)),
            pl.BlockSpec((ROW_TILE, LANES), lambda i: (i, 0)),
            pl.BlockSpec((1, LANES, ROW_TILE), lambda i: (i, 0, 0)),
            pl.BlockSpec((N_Q_HEADS, ROW_TILE, LANES), lambda i: (0, i, 0)),
            pl.BlockSpec((ROW_TILE, 512), lambda i: (i, 0)),
            pl.BlockSpec((1, 512, ROW_TILE), lambda i: (i, 0, 0)),
            pl.BlockSpec((MEM_HEADS, ROW_TILE, LANES), lambda i: (0, i, 0)),
            pl.BlockSpec((ROW_TILE, MIX_W), lambda i: (i, 0)),
        ],
        out_shape=[
            jax.ShapeDtypeStruct((N_Q_HEADS, t, LANES), BF16),
            jax.ShapeDtypeStruct((t, LANES), BF16),
            jax.ShapeDtypeStruct((n_tiles, LANES, ROW_TILE), BF16),
            jax.ShapeDtypeStruct((N_Q_HEADS, t, LANES), BF16),
            jax.ShapeDtypeStruct((t, 512), BF16),
            jax.ShapeDtypeStruct((n_tiles, 512, ROW_TILE), BF16),
            jax.ShapeDtypeStruct((MEM_HEADS, t, LANES), BF16),
            jax.ShapeDtypeStruct((t, MIX_W), F32),
        ],
        compiler_params=_params(1),
        name="proj_odd",
    )(x2, g, w, cos1, sin1, cosa, sina, gq, gk, gdq, gdk, gmq, bd)


def _flash_t(qz_ref, k_ref, vt_ref, v_rows, m_sc, l_sc, acc_sc):
    n_maps = qz_ref.shape[0]
    n_chunks, _, chunk = vt_ref.shape
    m_sc[...] = jnp.full(m_sc.shape, NEG, F32)
    l_sc[...] = jnp.zeros(l_sc.shape, F32)
    acc_sc[...] = jnp.zeros(acc_sc.shape, F32)

    def body(c, carry):
        kc = k_ref[pl.ds(pl.multiple_of(c * chunk, chunk), chunk), :]
        vc = vt_ref[c]
        for g in range(n_maps):
            r0, nr = v_rows(g)
            s = _dot_nt(kc, qz_ref[g])
            m_old = m_sc[g]
            m_new = jnp.maximum(m_old, jnp.max(s, axis=0, keepdims=True))
            alpha = jnp.exp2(m_old - m_new)
            p = jnp.exp2(s - m_new)
            l_sc[g] = alpha * l_sc[g] + jnp.sum(p, axis=0, keepdims=True)
            acc_sc[g] = alpha * acc_sc[g] + _dot(vc[r0:r0 + nr, :], p.astype(BF16))
            m_sc[g] = m_new
        return carry

    lax.fori_loop(0, n_chunks, body, 0)


def _stream_t(qz_ref, k_ref, vt_ref, v_rows, emit):
    n_maps, tq, _ = qz_ref.shape
    n_tiles, _, tile = vt_ref.shape
    per_tile = tile // KV_CHUNK
    n_chunks = n_tiles * per_tile

    for g in range(n_maps):
        qt = qz_ref[g].astype(F32).T.astype(BF16)
        r0, nr = v_rows(g)

        def score(c, qt=qt):
            return _dot(k_ref[c * KV_CHUNK:(c + 1) * KV_CHUNK, :], qt)

        pending = [score(c) for c in range(min(KV_SKEW, n_chunks))]
        l8 = jnp.zeros((SUBLANES, tq), F32)
        pv = None
        for c in range(n_chunks):
            if c + KV_SKEW < n_chunks:
                pending.append(score(c + KV_SKEW))
            off = (c % per_tile) * KV_CHUNK
            p = jnp.exp2(pending.pop(0))
            l8 = l8 + jnp.sum(p.reshape(KV_CHUNK // SUBLANES, SUBLANES, tq), axis=0)
            d = _dot(vt_ref[c // per_tile, r0:r0 + nr, off:off + KV_CHUNK], p.astype(BF16))
            pv = d if pv is None else pv + d
        emit(g, pv * (1.0 / jnp.sum(l8, axis=0, keepdims=True)))


def _stream_loop_t(qz_ref, k_ref, vt_ref, nr, acc_sc):
    n_maps, tq, _ = qz_ref.shape
    n_tiles, _, tile = vt_ref.shape
    per_tile = tile // KV_CHUNK
    group = math.gcd(LOOP_GROUP, n_tiles * per_tile)
    assert group % per_tile == 0
    outs = []
    for g in range(n_maps):
        qg = qz_ref[g]
        acc_sc[g] = jnp.zeros((nr, tq), F32)

        def body(i, l8, g=g, qg=qg):
            where = [(i * (group // per_tile) + u // per_tile, (u % per_tile) * KV_CHUNK)
                     for u in range(group)]
            scores = []
            for t, off in where:
                kc = k_ref[pl.ds(pl.multiple_of(t * tile + off, KV_CHUNK), KV_CHUNK), :]
                scores.append(_dot_nt(kc, qg))
            pv = None
            for (t, off), s in zip(where, scores):
                p = jnp.exp2(s)
                l8 = l8 + jnp.sum(p.reshape(KV_CHUNK // SUBLANES, SUBLANES, tq), axis=0)
                d = _dot(vt_ref[t, 0:nr, off:off + KV_CHUNK], p.astype(BF16))
                pv = d if pv is None else pv + d
            acc_sc[g] += pv
            return l8

        l8 = lax.fori_loop(0, n_tiles * per_tile // group, body, jnp.zeros((SUBLANES, tq), F32))
        outs.append(acc_sc[g] * (1.0 / jnp.sum(l8, axis=0, keepdims=True)))
    return outs


def _axial_rows(g):
    return (g // GQA_GROUP) * HEAD, HEAD


def _axial_out_kernel(qz_ref, k_ref, vt_ref, mqz_ref, mk_ref, mvt_ref, g1_ref, gm_ref, y2_ref, x_ref,
                      w_ref, o_ref, y1_sc, ym_sc):
    @pl.when(pl.program_id(0) == 0)
    def _():
        y1_sc[...] = jnp.zeros(y1_sc.shape, BF16)
        ym_sc[...] = jnp.zeros(ym_sc.shape, BF16)

    acc = _dot(y1_sc[...], w_ref[0:512, :])
    acc = acc + _dot(y2_ref[...], w_ref[512:1024, :])
    acc = acc + _dot(ym_sc[...], w_ref[1024:MIX_W, :])
    o_ref[...] = x_ref[...] + acc

    mem_pairs = _mem_scores(mqz_ref, mk_ref)
    ym = _mem_finish(mem_pairs, mvt_ref).T
    ym_sc[...] = (ym * gm_ref[...]).astype(BF16)
    held = {}

    def emit(g, o):
        held[g] = o
        if g % 2 == 1:
            cols = slice((g // 2) * LANES, (g // 2 + 1) * LANES)
            y = jnp.concatenate([held.pop(g - 1), held.pop(g)], axis=0).T
            y1_sc[:, cols] = (y * g1_ref[:, cols]).astype(BF16)

    _stream_t(qz_ref, k_ref, vt_ref, _axial_rows, emit)


def _axial_out_proj(qz, k, vt, mqz, mk, mvt, gate, y2, x2, w_out, batch, seq):
    t = k.shape[0]
    tq = STREAM_Q_TILE
    nq = seq // tq
    n_tiles = batch * nq
    n_chunks = seq // ROW_TILE
    n_mem = mk.shape[1]
    att = lambda i: jnp.minimum(i, n_tiles - 1)
    out = lambda i: jnp.maximum(i - 1, 0)
    return pl.pallas_call(
        _axial_out_kernel,
        grid=(n_tiles + 1,),
        in_specs=[
            pl.BlockSpec((N_Q_HEADS, tq, LANES), lambda i: (0, att(i), 0)),
            pl.BlockSpec((seq, LANES), lambda i: (att(i) // nq, 0)),
            pl.BlockSpec((n_chunks, LANES, ROW_TILE), lambda i: (att(i) // nq, 0, 0)),
            pl.BlockSpec((MEM_HEADS, tq, LANES), lambda i: (0, att(i), 0)),
            pl.BlockSpec((1, n_mem, MEM_HEADS * HEAD), lambda i: (att(i) // nq, 0, 0)),
            pl.BlockSpec((1, MEM_HEADS * HEAD, n_mem), lambda i: (att(i) // nq, 0, 0)),
            pl.BlockSpec((tq, 512), lambda i: (att(i), 0)),
            pl.BlockSpec((tq, 256), lambda i: (att(i), 4)),
            pl.BlockSpec((tq, 512), lambda i: (out(i), 0)),
            pl.BlockSpec((tq, D_MODEL), lambda i: (out(i), 0)),
            pl.BlockSpec((MIX_W, D_MODEL), lambda i: (0, 0)),
        ],
        out_specs=pl.BlockSpec((tq, D_MODEL), lambda i: (out(i), 0)),
        out_shape=jax.ShapeDtypeStruct((t, D_MODEL), F32),
        scratch_shapes=[pltpu.VMEM((tq, 512), BF16), pltpu.VMEM((tq, 256), BF16)],
        compiler_params=_params(1),
        name="axial_out_proj",
    )(qz, k, vt, mqz, mk, mvt, gate, gate, y2, x2, w_out)


def _axial_kernel(qz_ref, k_ref, vt_ref, mqz_ref, mk_ref, mvt_ref, g1_ref, gm_ref, y1_ref, ym_ref,
                  m_sc, l_sc, acc_sc):
    ym = _mem_attention_t(mqz_ref, mk_ref, mvt_ref).T
    ym_ref[...] = (ym * gm_ref[...]).astype(BF16)
    _flash_t(qz_ref, k_ref, vt_ref, _axial_rows, m_sc, l_sc, acc_sc)
    outs = [acc_sc[g] * (1.0 / l_sc[g]) for g in range(N_Q_HEADS)]
    y1 = jnp.concatenate(outs, axis=0).T
    y1_ref[...] = (y1 * g1_ref[...]).astype(BF16)


def _axial_attention(qz, k, vt, mqz, mk, mvt, gate, batch, seq):
    t = k.shape[0]
    tq = Q_TILE
    nq = seq // tq
    n_chunks = seq // ROW_TILE
    n_mem = mk.shape[1]
    tok = lambda b, q: b * nq + q
    scratch = [
        pltpu.VMEM((N_Q_HEADS, 1, tq), F32),
        pltpu.VMEM((N_Q_HEADS, 1, tq), F32),
        pltpu.VMEM((N_Q_HEADS, HEAD, tq), F32),
    ]
    return pl.pallas_call(
        _axial_kernel,
        grid=(batch, nq),
        in_specs=[
            pl.BlockSpec((N_Q_HEADS, tq, LANES), lambda b, q: (0, tok(b, q), 0)),
            pl.BlockSpec((seq, LANES), lambda b, q: (b, 0)),
            pl.BlockSpec((n_chunks, LANES, ROW_TILE), lambda b, q: (b, 0, 0)),
            pl.BlockSpec((MEM_HEADS, tq, LANES), lambda b, q: (0, tok(b, q), 0)),
            pl.BlockSpec((1, n_mem, MEM_HEADS * HEAD), lambda b, q: (b, 0, 0)),
            pl.BlockSpec((1, MEM_HEADS * HEAD, n_mem), lambda b, q: (b, 0, 0)),
            pl.BlockSpec((tq, 512), lambda b, q: (tok(b, q), 0)),
            pl.BlockSpec((tq, 256), lambda b, q: (tok(b, q), 4)),
        ],
        out_specs=[
            pl.BlockSpec((tq, 512), lambda b, q: (tok(b, q), 0)),
            pl.BlockSpec((tq, 256), lambda b, q: (tok(b, q), 0)),
        ],
        out_shape=[
            jax.ShapeDtypeStruct((t, 512), BF16),
            jax.ShapeDtypeStruct((t, 256), BF16),
        ],
        scratch_shapes=scratch,
        compiler_params=_params(2),
        name="axial_attention_online",
    )(qz, k, vt, mqz, mk, mvt, gate, gate)


def _diff_kernel(qz_ref, k_ref, vt_ref, lam_ref, sg_ref, g2_ref, y2_ref, *scratch, lambda_init, bounded):
    v_rows = lambda g: (0, 2 * HEAD)
    if bounded:
        o0, o1 = _stream_loop_t(qz_ref, k_ref, vt_ref, 2 * HEAD, *scratch)
    else:
        m_sc, l_sc, acc_sc = scratch
        _flash_t(qz_ref, k_ref, vt_ref, v_rows, m_sc, l_sc, acc_sc)
        o0, o1 = (acc_sc[g] * (1.0 / l_sc[g]) for g in range(2))
    lv = lam_ref[...]
    lam = (jnp.exp(jnp.sum(lv[0:1] * lv[1:2], axis=-1, keepdims=True))
           - jnp.exp(jnp.sum(lv[2:3] * lv[3:4], axis=-1, keepdims=True)) + lambda_init)
    o = o0 - lam * o1
    ms = jnp.mean(o * o, axis=0, keepdims=True)
    on = (o * lax.rsqrt(ms + EPS)).T
    y2_ref[...] = (on * sg_ref[...] * (1.0 - lambda_init) * g2_ref[...]).astype(BF16)


def _diff_attention(dqz, dk, dvt, lam, subln_g, gate, batch, seq, layer, bounded):
    t = dk.shape[0]
    tq = DIFF_Q_TILE if bounded else Q_TILE
    nq = seq // tq
    n_chunks = seq // ROW_TILE
    lambda_init = 0.8 - 0.6 * math.exp(-0.3 * layer)
    tok = lambda b, q: b * nq + q
    scratch = [pltpu.VMEM((2, 2 * HEAD, tq), F32)] if bounded else [
        pltpu.VMEM((2, 1, tq), F32),
        pltpu.VMEM((2, 1, tq), F32),
        pltpu.VMEM((2, 2 * HEAD, tq), F32),
    ]
    return pl.pallas_call(
        functools.partial(_diff_kernel, lambda_init=lambda_init, bounded=bounded),
        grid=(batch, DIFF_HEADS, nq),
        in_specs=[
            pl.BlockSpec((2, tq, LANES), lambda b, h, q: (h, tok(b, q), 0)),
            pl.BlockSpec((seq, LANES), lambda b, h, q: (b, h)),
            pl.BlockSpec((n_chunks, LANES, ROW_TILE), lambda b, h, q: (b, h, 0)),
            pl.BlockSpec((4, HEAD), lambda b, h, q: (0, 0)),
            pl.BlockSpec((1, 2 * HEAD), lambda b, h, q: (0, 0)),
            pl.BlockSpec((tq, LANES), lambda b, h, q: (tok(b, q), 4 + h)),
        ],
        out_specs=pl.BlockSpec((tq, LANES), lambda b, h, q: (tok(b, q), h)),
        out_shape=jax.ShapeDtypeStruct((t, 512), BF16),
        scratch_shapes=scratch,
        compiler_params=_params(3),
        name="diff_attention" if bounded else "diff_attention_online",
    )(dqz, dk, dvt, lam, subln_g.reshape(1, 2 * HEAD), gate)


def _out_proj_kernel(y1_ref, y2_ref, ym_ref, x_ref, w_ref, o_ref):
    acc = _dot(y1_ref[...], w_ref[0:512, :])
    acc = acc + _dot(y2_ref[...], w_ref[512:1024, :])
    acc = acc + _dot(ym_ref[...], w_ref[1024:MIX_W, :])
    o_ref[...] = x_ref[...] + acc


def _out_proj(y1, y2, ym, x2, w):
    t = x2.shape[0]
    return pl.pallas_call(
        _out_proj_kernel,
        grid=(t // OUT_ROW_TILE,),
        in_specs=[
            pl.BlockSpec((OUT_ROW_TILE, 512), lambda i: (i, 0)),
            pl.BlockSpec((OUT_ROW_TILE, 512), lambda i: (i, 0)),
            pl.BlockSpec((OUT_ROW_TILE, 256), lambda i: (i, 0)),
            pl.BlockSpec((OUT_ROW_TILE, D_MODEL), lambda i: (i, 0)),
            pl.BlockSpec((MIX_W, D_MODEL), lambda i: (0, 0)),
        ],
        out_specs=pl.BlockSpec((OUT_ROW_TILE, D_MODEL), lambda i: (i, 0)),
        out_shape=jax.ShapeDtypeStruct((t, D_MODEL), F32),
        compiler_params=_params(1),
        name="out_proj",
    )(y1, y2, ym, x2, w)


def _rope_angles(pos, dim):
    inv = ROPE_THETA ** (-jnp.arange(0, dim, 2, dtype=F32) / dim)
    return pos.astype(F32)[:, None] * inv[None, :]


def _rope_tables(seq):
    pos = jnp.arange(seq)
    a1 = _rope_angles(pos, HEAD)
    cos1 = jnp.concatenate([jnp.cos(a1), jnp.cos(a1)], axis=-1)
    sin1 = jnp.concatenate([-jnp.sin(a1), jnp.sin(a1)], axis=-1)
    ar = _rope_angles(pos // GRID_W, HEAD // 2)
    ac = _rope_angles(pos % GRID_W, HEAD // 2)
    cosa = jnp.concatenate([jnp.cos(ar), jnp.cos(ar), jnp.cos(ac), jnp.cos(ac)], axis=-1)
    sina = jnp.concatenate([-jnp.sin(ar), jnp.sin(ar), -jnp.sin(ac), jnp.sin(ac)], axis=-1)
    rep = lambda a: jnp.tile(a, (1, LANES // HEAD))
    return rep(cos1), rep(sin1), rep(cosa), rep(sina)


def _scores_bounded(qk_g):
    bound = HEAD * Q_SCALE * 1.01 * jnp.max(jnp.abs(qk_g[0])) * jnp.max(jnp.abs(qk_g[1]))
    return bound <= SAFE_SCORE_BOUND


def _tile_gain(g, width):
    return jnp.tile(g.astype(F32), width // HEAD).reshape(1, width)


def _even_layer(x2, batch, seq, p, l, tables, mk, mvt):
    e = l // 2
    cos1, sin1, _, _ = tables
    y1, qz, k, vt, mqz, gate = _proj_even(
        x2, seq, p["norm_g"][l].reshape(1, D_MODEL), p["w_in"][l], cos1, sin1,
        _tile_gain(p["swa_qk_g"][e, 0], 512), _tile_gain(p["swa_qk_g"][e, 1], LANES),
        _tile_gain(p["mem_qk_g"][l, 0], 256), p["conv_w"][e], p["bd"])

    def tail_with_max(sink, qz, k, vt, mqz, mk, mvt, gate, y1, x2, w_out):
        y2, ym = _window_attention(sink, qz, k, vt, mqz, mk, mvt, gate, batch, seq)
        return _out_proj(y1, y2, ym, x2, w_out)

    sink_ok = jnp.max(jnp.abs(p["swa_sink"][e])) * LOG2E <= SAFE_SCORE_BOUND
    return lax.cond(
        _scores_bounded(p["swa_qk_g"][e]) & _scores_bounded(p["mem_qk_g"][l]) & sink_ok,
        functools.partial(_window_out_proj, batch=batch, seq=seq), tail_with_max,
        p["swa_sink"][e], qz, k, vt, mqz, mk, mvt, gate, y1, x2, p["w_out"][l])


def _odd_layer(x2, batch, seq, p, l, tables, mk, mvt):
    o = l // 2
    qz, k, vt, dqz, dk, dvt, mqz, gate = _proj_odd(
        x2, seq, p["norm_g"][l].reshape(1, D_MODEL), p["w_in"][l], *tables,
        _tile_gain(p["ax_qk_g"][o, 0], 512), _tile_gain(p["ax_qk_g"][o, 1], LANES),
        _tile_gain(p["diff_qk_g"][o, 0], 512), _tile_gain(p["diff_qk_g"][o, 1], 512),
        _tile_gain(p["mem_qk_g"][l, 0], 256), p["bd"])
    y2 = lax.cond(
        _scores_bounded(p["diff_qk_g"][o]),
        functools.partial(_diff_attention, batch=batch, seq=seq, layer=l, bounded=True),
        functools.partial(_diff_attention, batch=batch, seq=seq, layer=l, bounded=False),
        dqz, dk, dvt, p["diff_lambda"][o], p["diff_subln_g"][o], gate)

    def tail_with_max(qz, k, vt, mqz, mk, mvt, gate, y2, x2, w_out):
        y1, ym = _axial_attention(qz, k, vt, mqz, mk, mvt, gate, batch, seq)
        return _out_proj(y1, y2, ym, x2, w_out)

    return lax.cond(
        _scores_bounded(p["ax_qk_g"][o]) & _scores_bounded(p["mem_qk_g"][l]),
        functools.partial(_axial_out_proj, batch=batch, seq=seq), tail_with_max,
        qz, k, vt, mqz, mk, mvt, gate, y2, x2, p["w_out"][l])


def _trunk(x, mem, p):
    batch, seq, _ = x.shape
    x2 = x.reshape(batch * seq, D_MODEL)
    tables = _rope_tables(seq)
    gk_mem = jnp.stack([_tile_gain(p["mem_qk_g"][l, 1], 256) for l in range(DEPTH)])
    mk_all, mvt_all = _memkv(mem, p["mem_norm_g"], p["w_mem_kv"], gk_mem, p["bd"])
    for l in range(DEPTH):
        layer = _even_layer if l % 2 == 0 else _odd_layer
        x2 = layer(x2, batch, seq, p, l, tables, mk_all[l], mvt_all[l])
    return x2.reshape(batch, seq, D_MODEL)


def kernel(x_prompt, x_sample, mem_prompt, mem_sample, norm_g, w_in, w_out, mem_norm_g, w_mem_kv,
           mem_qk_g, conv_w, swa_qk_g, swa_sink, ax_qk_g, diff_qk_g, diff_lambda, diff_subln_g):
    grp = jnp.arange(2 * LANES) // HEAD
    p = dict(
        norm_g=norm_g, w_in=w_in.astype(BF16), w_out=w_out.astype(BF16), mem_norm_g=mem_norm_g,
        w_mem_kv=w_mem_kv.astype(BF16), mem_qk_g=mem_qk_g, conv_w=conv_w, swa_qk_g=swa_qk_g,
        swa_sink=swa_sink, ax_qk_g=ax_qk_g, diff_qk_g=diff_qk_g, diff_lambda=diff_lambda,
        diff_subln_g=diff_subln_g, bd=(grp[:, None] == grp[None, :]).astype(BF16))
    return (_trunk(x_prompt, mem_prompt, p), _trunk(x_sample, mem_sample, p))
```
